```python
import jax, jax.numpy as jnp
from jax import lax
import numpy as np

D_MODEL = 1024
BATCH = 4
SEQ = 8192
DEPTH = 1
DEC_BATCH = 8
DEC_SEQ = 32
PAST_LEN = 1024

CHUNK = 64
D_POOL = D_MODEL // 2
POOL_WINDOWS = (2, 4, 8, 16)
N_POOL_GROUPS = len(POOL_WINDOWS)
POOL_GROUP = D_POOL // N_POOL_GROUPS
POOL_STATE = max(POOL_WINDOWS) - 1
SB_HEADS = 8
SB_HEAD_DIM = 64
D_SB = SB_HEADS * SB_HEAD_DIM
Q_BLOCK = 128
D_FF = 4 * D_MODEL
RMS_EPS = 1e-6
D_IN = D_POOL + 3 * D_SB + 2 * D_MODEL
SPLITS = [D_POOL, D_POOL + D_SB, D_POOL + 2 * D_SB, D_POOL + 3 * D_SB, D_POOL + 3 * D_SB + D_MODEL]

kernel_name = "hybrid_pool_stickbreaking_streaming_step"


def rmsnorm(x, g):
    xf = x.astype(jnp.float32)
    xf = xf * lax.rsqrt(jnp.mean(xf * xf, axis=-1, keepdims=True) + RMS_EPS)
    return xf.astype(x.dtype) * g


def swiglu(x, w_gate, w_up, w_down):
    return (jax.nn.silu(x @ w_gate) * (x @ w_up)) @ w_down


def pool_mix(u, prefix, pos, w_grp, scale):
    b, t, _ = u.shape
    ext = jnp.concatenate([prefix, u], axis=1).astype(jnp.float32)
    cs = jnp.concatenate([jnp.zeros((b, 1, D_POOL), jnp.float32), jnp.cumsum(ext, axis=1)], axis=1)
    end = cs[:, POOL_STATE + 1:]
    outs = []
    for gi, w in enumerate(POOL_WINDOWS):
        lo, hi = gi * POOL_GROUP, (gi + 1) * POOL_GROUP
        start = cs[:, POOL_STATE + 1 - w: POOL_STATE + 1 - w + t, lo:hi]
        cnt = jnp.minimum(w, pos + 1).astype(jnp.float32)[None, :, None]
        outs.append((end[..., lo:hi] - start) / cnt)
    mean = jnp.concatenate(outs, axis=-1)
    d = (mean - u.astype(jnp.float32)).astype(u.dtype).reshape(b, t, N_POOL_GROUPS, POOL_GROUP)
    y = jnp.einsum('btgc,gce->btge', d, w_grp).reshape(b, t, D_POOL)
    return y * scale


def stick_breaking(q, k, v, q_pos, k_pos):
    z = jnp.einsum('bhqd,bhkd->bhqk', q, k).astype(jnp.float32) * (SB_HEAD_DIM ** -0.5)
    mask = k_pos[None, :] < q_pos[:, None]
    log_surv = jnp.where(mask, jax.nn.log_sigmoid(-z), 0.0)
    after = lax.cumsum(log_surv, axis=3, reverse=True) - log_surv
    w = jnp.where(mask, jnp.exp(jax.nn.log_sigmoid(z) + after), 0.0)
    return jnp.einsum('bhqk,bhkd->bhqd', w, v.astype(jnp.float32)).astype(v.dtype)


def stick_breaking_prompt(q, k, v):
    b, h, s, hd = q.shape
    k_pos = jnp.arange(s)

    def one_block(i):
        s0 = i * Q_BLOCK
        qb = lax.dynamic_slice_in_dim(q, s0, Q_BLOCK, axis=2)
        return stick_breaking(qb, k, v, s0 + jnp.arange(Q_BLOCK), k_pos)

    o = lax.map(one_block, jnp.arange(s // Q_BLOCK))
    return o.transpose(1, 2, 0, 3, 4).reshape(b, h, s, hd)


def heads(t):
    b, n, _ = t.shape
    return t.reshape(b, n, SB_HEADS, SB_HEAD_DIM).transpose(0, 2, 1, 3)


def layer_forward(x, pos, pool_prefix, k_past, v_past,
                  ffn1_norm, ffn1_gate, ffn1_up, ffn1_down, mix_norm, w_in, pool_w, pool_scale,
                  w_branch_pool, w_branch_sb, w_out, ffn2_norm, ffn2_gate, ffn2_up, ffn2_down):
    b, t, _ = x.shape
    x = x + 0.5 * swiglu(rmsnorm(x, ffn1_norm), ffn1_gate, ffn1_up, ffn1_down)
    h = rmsnorm(x, mix_norm)
    u, q, k, v, g_a, g_b = jnp.split(h @ w_in, SPLITS, axis=-1)
    a = pool_mix(u, pool_prefix, pos, pool_w, pool_scale)
    qh, kh, vh = heads(q), heads(k), heads(v)
    if k_past is None:
        o = stick_breaking_prompt(qh, kh, vh)
    else:
        k_all = jnp.concatenate([k_past, kh], axis=2)
        v_all = jnp.concatenate([v_past, vh], axis=2)
        o = stick_breaking(qh, k_all, v_all, pos, jnp.arange(k_all.shape[2]))
    o = o.transpose(0, 2, 1, 3).reshape(b, t, D_SB)
    merged = jax.nn.sigmoid(g_a) * (a @ w_branch_pool) + jax.nn.sigmoid(g_b) * (o @ w_branch_sb)
    x = x + merged @ w_out
    x = x + 0.5 * swiglu(rmsnorm(x, ffn2_norm), ffn2_gate, ffn2_up, ffn2_down)
    new_pool = jnp.concatenate([pool_prefix, u], axis=1)[:, -POOL_STATE:]
    return x, kh, vh, new_pool


def setup_inputs(seed: int = 0) -> dict:
    key = jax.random.key(seed)
    ks = jax.random.split(key, 24)
    f32 = jnp.float32

    def nrm(k, shape, fan_in):
        return jax.random.normal(k, shape, f32) * (fan_in ** -0.5)

    def gain(k, shape):
        return 1.0 + 0.02 * jax.random.normal(k, shape, f32)

    L = DEPTH
    return {
        "x_prompt": jax.random.normal(ks[0], (BATCH, SEQ, D_MODEL), f32),
        "x_sample": jax.random.normal(ks[1], (DEC_BATCH, DEC_SEQ, D_MODEL), f32),
        "cache_k": jax.random.normal(ks[2], (L, DEC_BATCH, SB_HEADS, PAST_LEN, SB_HEAD_DIM), f32),
        "cache_v": jax.random.normal(ks[3], (L, DEC_BATCH, SB_HEADS, PAST_LEN, SB_HEAD_DIM), f32),
        "state_pool": jax.random.normal(ks[4], (L, DEC_BATCH, POOL_STATE, D_POOL), f32),
        "ffn1_norm": gain(ks[5], (L, D_MODEL)),
        "ffn1_gate": nrm(ks[6], (L, D_MODEL, D_FF), D_MODEL),
        "ffn1_up": nrm(ks[7], (L, D_MODEL, D_FF), D_MODEL),
        "ffn1_down": nrm(ks[8], (L, D_FF, D_MODEL), D_FF),
        "mix_norm": gain(ks[9], (L, D_MODEL)),
        "w_in": nrm(ks[10], (L, D_MODEL, D_IN), D_MODEL),
        "pool_w": nrm(ks[11], (L, N_POOL_GROUPS, POOL_GROUP, POOL_GROUP), POOL_GROUP),
        "pool_scale": 1.0 + 0.1 * jax.random.normal(ks[12], (L, D_POOL), f32),
        "w_branch_pool": nrm(ks[13], (L, D_POOL, D_MODEL), D_POOL),
        "w_branch_sb": nrm(ks[14], (L, D_SB, D_MODEL), D_SB),
        "w_out": nrm(ks[15], (L, D_MODEL, D_MODEL), D_MODEL),
        "ffn2_norm": gain(ks[16], (L, D_MODEL)),
        "ffn2_gate": nrm(ks[17], (L, D_MODEL, D_FF), D_MODEL),
        "ffn2_up": nrm(ks[18], (L, D_MODEL, D_FF), D_MODEL),
        "ffn2_down": nrm(ks[19], (L, D_FF, D_MODEL), D_FF),
        "final_norm": gain(ks[20], (D_MODEL,)),
    }


def reference(x_prompt, x_sample, cache_k, cache_v, state_pool,
              ffn1_norm, ffn1_gate, ffn1_up, ffn1_down, mix_norm, w_in, pool_w, pool_scale,
              w_branch_pool, w_branch_sb, w_out, ffn2_norm, ffn2_gate, ffn2_up, ffn2_down, final_norm):
    b_p, s_p, _ = x_prompt.shape
    s_d = x_sample.shape[1]
    pos_p = jnp.arange(s_p)
    pos_d = PAST_LEN + jnp.arange(s_d)
    xp, xd = x_prompt, x_sample
    kp_l, vp_l, pp_l, kd_l, vd_l, pd_l = [], [], [], [], [], []
    for l in range(DEPTH):
        w = (ffn1_norm[l], ffn1_gate[l], ffn1_up[l], ffn1_down[l], mix_norm[l], w_in[l], pool_w[l],
             pool_scale[l], w_branch_pool[l], w_branch_sb[l], w_out[l], ffn2_norm[l], ffn2_gate[l],
             ffn2_up[l], ffn2_down[l])
        zero_prefix = jnp.zeros((b_p, POOL_STATE, D_POOL), xp.dtype)
        xp, kp, vp, pp = layer_forward(xp, pos_p, zero_prefix, None, None, *w)
        xd, kd, vd, pd = layer_forward(xd, pos_d, state_pool[l], cache_k[l], cache_v[l], *w)
        kp_l.append(kp); vp_l.append(vp); pp_l.append(pp)
        kd_l.append(kd); vd_l.append(vd); pd_l.append(pd)
    y_prompt = rmsnorm(xp, final_norm)
    y_sample = rmsnorm(xd, final_norm)
    new_k_prompt = jnp.stack(kp_l, axis=0)
    new_v_prompt = jnp.stack(vp_l, axis=0)
    new_pool_prompt = jnp.stack(pp_l, axis=0)
    new_k_sample = jnp.stack(kd_l, axis=0)
    new_v_sample = jnp.stack(vd_l, axis=0)
    new_pool_sample = jnp.stack(pd_l, axis=0)
    return (y_prompt, y_sample, new_k_prompt, new_v_prompt, new_pool_prompt, new_k_sample, new_v_sample, new_pool_sample)
```

```python
import functools

import jax
import jax.numpy as jnp
from jax import lax
from jax.experimental import pallas as pl
from jax.experimental.pallas import tpu as pltpu

F32 = jnp.float32
BF16 = jnp.bfloat16

D_MODEL = 1024
D_FF = 4 * D_MODEL
D_POOL = D_MODEL // 2
POOL_WINDOWS = (2, 4, 8, 16)
POOL_GROUP = D_POOL // len(POOL_WINDOWS)
POOL_STATE = max(POOL_WINDOWS) - 1
POOL_HALO = POOL_STATE + 1
SB_HEADS = 8
SB_HEAD_DIM = 64
D_SB = SB_HEADS * SB_HEAD_DIM
D_IN = D_POOL + 3 * D_SB + 2 * D_MODEL
RMS_EPS = 1e-6
LANES = 128
VMEM_LIMIT_BYTES = 56 * 1024 * 1024

SURVIVAL_UNDERFLOW = 104.0


def _rmsnorm(x, g):
    ms = jnp.mean(x * x, axis=-1, keepdims=True)
    return (x * lax.rsqrt(ms + RMS_EPS)) * g


def _ffn_kernel(*refs, n_ff, final):
    if final:
        x_ref, g_ref, wg_ref, wu_ref, wd_ref, fg_ref, o_ref, xn_ref, acc_ref = refs
    else:
        x_ref, g_ref, wg_ref, wu_ref, wd_ref, o_ref, xn_ref, acc_ref = refs
    j = pl.program_id(1)

    @pl.when(j == 0)
    def _():
        xn_ref[...] = _rmsnorm(x_ref[...], g_ref[...]).astype(BF16)
        acc_ref[...] = jnp.zeros_like(acc_ref)

    xn = xn_ref[...]
    gate = jnp.dot(xn, wg_ref[...], preferred_element_type=F32)
    up = jnp.dot(xn, wu_ref[...], preferred_element_type=F32)
    h = (gate * jax.nn.sigmoid(gate) * up).astype(BF16)
    acc_ref[...] += jnp.dot(h, wd_ref[...], preferred_element_type=F32)

    @pl.when(j == n_ff - 1)
    def _():
        y = x_ref[...] + 0.5 * acc_ref[...]
        if final:
            y = _rmsnorm(y, fg_ref[...])
        o_ref[...] = y


def _ffn(x, norm_g, wg, wu, wd, final_g=None, *, tm, tf):
    n = x.shape[0]
    n_ff = D_FF // tf
    final = final_g is not None
    row = pl.BlockSpec((1, D_MODEL), lambda i, j: (0, 0))
    in_specs = [
        pl.BlockSpec((tm, D_MODEL), lambda i, j: (i, 0)),
        row,
        pl.BlockSpec((D_MODEL, tf), lambda i, j: (0, j)),
        pl.BlockSpec((D_MODEL, tf), lambda i, j: (0, j)),
        pl.BlockSpec((tf, D_MODEL), lambda i, j: (j, 0)),
    ]
    args = [x, norm_g, wg, wu, wd]
    if final:
        in_specs.append(row)
        args.append(final_g)
    return pl.pallas_call(
        functools.partial(_ffn_kernel, n_ff=n_ff, final=final),
        grid=(n // tm, n_ff),
        in_specs=in_specs,
        out_specs=pl.BlockSpec((tm, D_MODEL), lambda i, j: (i, 0)),
        out_shape=jax.ShapeDtypeStruct((n, D_MODEL), F32),
        scratch_shapes=[pltpu.VMEM((tm, D_MODEL), BF16), pltpu.VMEM((tm, D_MODEL), F32)],
        compiler_params=pltpu.CompilerParams(
            dimension_semantics=("parallel", "arbitrary"), vmem_limit_bytes=VMEM_LIMIT_BYTES),
        name="ffn_final" if final else "ffn",
    )(*args)


def _proj_kernel(x_ref, g_ref, w_ref, u_ref, qkv_ref, k_ref, v_ref, gate_ref):
    h = _rmsnorm(x_ref[...], g_ref[...]).astype(BF16)

    def mm(c0, c1):
        return jnp.dot(h, w_ref[:, c0:c1], preferred_element_type=F32)

    c = 0
    u_ref[...] = mm(c, c + D_POOL)
    c += D_POOL
    qkv_ref[:, 0:D_SB] = (mm(c, c + D_SB) * (SB_HEAD_DIM ** -0.5)).astype(BF16)
    c += D_SB
    for idx, hm_ref in ((1, k_ref), (2, v_ref)):
        t = mm(c, c + D_SB)
        c += D_SB
        qkv_ref[:, idx * D_SB:(idx + 1) * D_SB] = t.astype(BF16)
        for hd in range(SB_HEADS):
            hm_ref[0, hd, :, :] = t[:, hd * SB_HEAD_DIM:(hd + 1) * SB_HEAD_DIM]
    gate_ref[:, 0:D_MODEL] = mm(c, c + D_MODEL)
    c += D_MODEL
    gate_ref[:, D_MODEL:2 * D_MODEL] = mm(c, c + D_MODEL)


def _proj(x, norm_g, w_in, *, batch, seq, tm):
    n = x.shape[0]
    tiles_per_seq = seq // tm
    hm_spec = pl.BlockSpec((1, SB_HEADS, tm, SB_HEAD_DIM),
                           lambda i: (i // tiles_per_seq, 0, i % tiles_per_seq, 0))
    hm_shape = jax.ShapeDtypeStruct((batch, SB_HEADS, seq, SB_HEAD_DIM), F32)
    return pl.pallas_call(
        _proj_kernel,
        grid=(n // tm,),
        in_specs=[
            pl.BlockSpec((tm, D_MODEL), lambda i: (i, 0)),
            pl.BlockSpec((1, D_MODEL), lambda i: (0, 0)),
            pl.BlockSpec((D_MODEL, D_IN), lambda i: (0, 0)),
        ],
        out_specs=[
            pl.BlockSpec((tm, D_POOL), lambda i: (i, 0)),
            pl.BlockSpec((tm, 3 * D_SB), lambda i: (i, 0)),
            hm_spec,
            hm_spec,
            pl.BlockSpec((tm, 2 * D_MODEL), lambda i: (i, 0)),
        ],
        out_shape=[
            jax.ShapeDtypeStruct((n, D_POOL), F32),
            jax.ShapeDtypeStruct((n, 3 * D_SB), BF16),
            hm_shape,
            hm_shape,
            jax.ShapeDtypeStruct((n, 2 * D_MODEL), F32),
        ],
        compiler_params=pltpu.CompilerParams(
            dimension_semantics=("parallel",), vmem_limit_bytes=VMEM_LIMIT_BYTES),
        name="proj",
    )(x, norm_g, w_in)


def _attn_kernel(q_ref, k_ref, v_ref, o_ref, acc_ref, car_ref, *, tq, tk, pos0):
    qi = pl.program_id(1)
    q_start = pos0 + qi * tq
    t_diag = q_start // tk

    lane = lax.broadcasted_iota(jnp.int32, (1, LANES), 1)
    first_head = lane < SB_HEAD_DIM
    r = lax.broadcasted_iota(jnp.int32, (tk, tk + LANES), 0)
    c = lax.broadcasted_iota(jnp.int32, (tk, tk + LANES), 1)
    newer = jnp.where((c >= tk) | (r > c), 1.0, 0.0).astype(BF16)

    acc_ref[...] = jnp.zeros_like(acc_ref)
    car_ref[...] = jnp.zeros_like(car_ref)

    def process(t, masked):
        start = pl.multiple_of(t * tk, tk)
        k_t = k_ref[pl.ds(start, tk), :]
        v_t = v_ref[pl.ds(start, tk), :]
        if masked:
            k_pos = start + lax.broadcasted_iota(jnp.int32, (1, tk), 1)
            q_pos = q_start + lax.broadcasted_iota(jnp.int32, (tq, 1), 0)
            valid = k_pos < q_pos
        car_min = None
        for pair in range(SB_HEADS // 2):
            cols = slice(pair * LANES, (pair + 1) * LANES)
            q_p = q_ref[:, cols]
            k_p = k_t[:, cols]
            v_p = v_t[:, cols]
            out_p = None
            for half in range(2):
                head = 2 * pair + half
                sel = first_head if half == 0 else jnp.logical_not(first_head)
                q_h = jnp.where(sel, q_p, jnp.zeros_like(q_p))
                v_h = jnp.where(sel, v_p, jnp.zeros_like(v_p))
                z = lax.dot_general(q_h, k_p, (((1,), (1,)), ((), ())),
                                    preferred_element_type=F32)
                s = jnp.maximum(z, 0.0) + jnp.log(1.0 + jnp.exp(-jnp.abs(z)))
                if masked:
                    s = jnp.where(valid, s, 0.0)
                sums = jnp.dot(s.astype(BF16), newer, preferred_element_type=F32)
                car = car_ref[head]
                car_t = car if tk == LANES else jnp.concatenate([car] * (tk // LANES), axis=1)
                w = jnp.exp((z - s) - sums[:, :tk] - car_t)
                if masked:
                    w = jnp.where(valid, w, 0.0)
                o_h = jnp.dot(w.astype(BF16), v_h, preferred_element_type=F32)
                out_p = o_h if out_p is None else out_p + o_h
                car = car + sums[:, tk:]
                car_ref[head] = car
                car_min = car if car_min is None else jnp.minimum(car_min, car)
            acc_ref[:, cols] += out_p
        return (jnp.min(car_min) >= SURVIVAL_UNDERFLOW).astype(jnp.int32)

    done = process(t_diag, True)

    def cond(carry):
        t, done = carry
        return jnp.logical_and(t >= 0, done == 0)

    def body(carry):
        t, _ = carry
        return t - 1, process(t, False)

    lax.while_loop(cond, body, (t_diag - 1, done))
    o_ref[...] = acc_ref[...].astype(BF16)


def _attn(q_arr, q_col, k_arr, k_col, v_arr, v_col, *, batch, q_len, k_len, pos0, tq, tk):
    assert tk % tq == 0 and pos0 % tq == 0 and tk % LANES == 0
    assert q_len % tq == 0 and k_len % tk == 0 and pos0 + q_len <= k_len
    nq = q_len // tq
    return pl.pallas_call(
        functools.partial(_attn_kernel, tq=tq, tk=tk, pos0=pos0),
        grid=(batch, nq),
        in_specs=[
            pl.BlockSpec((tq, D_SB), lambda b, i: (b * nq + i, q_col)),
            pl.BlockSpec((k_len, D_SB), lambda b, i: (b, k_col)),
            pl.BlockSpec((k_len, D_SB), lambda b, i: (b, v_col)),
        ],
        out_specs=pl.BlockSpec((tq, D_SB), lambda b, i: (b * nq + i, 0)),
        out_shape=jax.ShapeDtypeStruct((batch * q_len, D_SB), BF16),
        scratch_shapes=[pltpu.VMEM((tq, D_SB), F32), pltpu.VMEM((SB_HEADS, tq, LANES), F32)],
        compiler_params=pltpu.CompilerParams(
            dimension_semantics=("parallel", "arbitrary"), vmem_limit_bytes=VMEM_LIMIT_BYTES),
        name="attn",
    )(q_arr, k_arr, v_arr)


def _merge_kernel(x_ref, u_ref, halo_ref, o_ref, gate_ref, pw_ref, ps_ref, wbp_ref, wbs_ref,
                  wout_ref, y_ref, ext_ref, *, tm, pos0, tiles_per_seq):
    i = pl.program_id(0)
    ext_ref[0:POOL_HALO, :] = halo_ref[0]
    ext_ref[POOL_HALO:, :] = u_ref[...]
    pos = pos0 + (i % tiles_per_seq) * tm + lax.broadcasted_iota(jnp.int32, (tm, 1), 0)

    mixed = []
    for gi, win in enumerate(POOL_WINDOWS):
        cols = slice(gi * POOL_GROUP, (gi + 1) * POOL_GROUP)
        tot = ext_ref[POOL_HALO:POOL_HALO + tm, cols]
        for back in range(1, win):
            tot = tot + ext_ref[POOL_HALO - back:POOL_HALO - back + tm, cols]
        cnt = jnp.minimum(win, pos + 1).astype(F32)
        d = (tot / cnt - u_ref[:, cols]).astype(BF16)
        mixed.append(jnp.dot(d, pw_ref[gi], preferred_element_type=F32) * ps_ref[:, cols])
    a = jnp.concatenate(mixed, axis=1).astype(BF16)

    br_pool = jnp.dot(a, wbp_ref[...], preferred_element_type=F32)
    br_sb = jnp.dot(o_ref[...], wbs_ref[...], preferred_element_type=F32)
    merged = (jax.nn.sigmoid(gate_ref[:, 0:D_MODEL]) * br_pool
              + jax.nn.sigmoid(gate_ref[:, D_MODEL:2 * D_MODEL]) * br_sb)
    y_ref[...] = x_ref[...] + jnp.dot(merged.astype(BF16), wout_ref[...],
                                      preferred_element_type=F32)


def _merge(x, u, halo, o, gate, pool_w, pool_scale, w_bp, w_bs, w_out, *, tm, pos0, seq):
    n = x.shape[0]
    const2 = lambda i: (0, 0)
    return pl.pallas_call(
        functools.partial(_merge_kernel, tm=tm, pos0=pos0, tiles_per_seq=seq // tm),
        grid=(n // tm,),
        in_specs=[
            pl.BlockSpec((tm, D_MODEL), lambda i: (i, 0)),
            pl.BlockSpec((tm, D_POOL), lambda i: (i, 0)),
            pl.BlockSpec((1, POOL_HALO, D_POOL), lambda i: (i, 0, 0)),
            pl.BlockSpec((tm, D_SB), lambda i: (i, 0)),
            pl.BlockSpec((tm, 2 * D_MODEL), lambda i: (i, 0)),
            pl.BlockSpec((len(POOL_WINDOWS), POOL_GROUP, POOL_GROUP), lambda i: (0, 0, 0)),
            pl.BlockSpec((1, D_POOL), const2),
            pl.BlockSpec((D_POOL, D_MODEL), const2),
            pl.BlockSpec((D_SB, D_MODEL), const2),
            pl.BlockSpec((D_MODEL, D_MODEL), const2),
        ],
        out_specs=pl.BlockSpec((tm, D_MODEL), lambda i: (i, 0)),
        out_shape=jax.ShapeDtypeStruct((n, D_MODEL), F32),
        scratch_shapes=[pltpu.VMEM((tm + POOL_HALO, D_POOL), F32)],
        compiler_params=pltpu.CompilerParams(
            dimension_semantics=("parallel",), vmem_limit_bytes=VMEM_LIMIT_BYTES),
        name="merge",
    )(x, u, halo, o, gate, pool_w, pool_scale, w_bp, w_bs, w_out)


def _layer(x, w, *, batch, seq, pos0, k_past, v_past, pool_prefix, tm_ffn, tm_mix, tq, tk):
    x1 = _ffn(x, w["ffn1_norm"], w["ffn1_gate"], w["ffn1_up"], w["ffn1_down"], tm=tm_ffn, tf=512)
    u, qkv, k_hm, v_hm, gate = _proj(x1, w["mix_norm"], w["w_in"], batch=batch, seq=seq, tm=tm_mix)

    if k_past is None:
        o = _attn(qkv, 0, qkv, 1, qkv, 2, batch=batch, q_len=seq, k_len=seq, pos0=0, tq=tq, tk=tk)
    else:
        k_len = -(-(pos0 + seq) // tk) * tk
        def with_past(past, col):
            new = qkv[:, col * D_SB:(col + 1) * D_SB].reshape(batch, seq, D_SB)
            full = jnp.concatenate([past, new], axis=1)
            full = jnp.pad(full, ((0, 0), (0, k_len - full.shape[1]), (0, 0)))
            return full.reshape(batch * k_len, D_SB)
        o = _attn(qkv, 0, with_past(k_past, 1), 0, with_past(v_past, 2), 0,
                  batch=batch, q_len=seq, k_len=k_len, pos0=pos0, tq=tq, tk=tk)

    tiles = seq // tm_mix
    u4 = u.reshape(batch, tiles, tm_mix, D_POOL)
    first = jnp.pad(pool_prefix, ((0, 0), (1, 0), (0, 0)))[:, None]
    halo = jnp.concatenate([first, u4[:, :-1, tm_mix - POOL_HALO:, :]], axis=1)
    halo = halo.reshape(batch * tiles, POOL_HALO, D_POOL)

    x2 = _merge(x1, u, halo, o, gate, w["pool_w"], w["pool_scale"], w["w_branch_pool"],
                w["w_branch_sb"], w["w_out"], tm=tm_mix, pos0=pos0, seq=seq)
    y = _ffn(x2, w["ffn2_norm"], w["ffn2_gate"], w["ffn2_up"], w["ffn2_down"], w["final_norm"],
             tm=tm_ffn, tf=512)
    new_pool = u.reshape(batch, seq, D_POOL)[:, seq - POOL_STATE:, :]
    return y.reshape(batch, seq, D_MODEL), k_hm, v_hm, new_pool


def kernel(x_prompt, x_sample, cache_k, cache_v, state_pool, ffn1_norm, ffn1_gate, ffn1_up, ffn1_down, mix_norm, w_in, pool_w, pool_scale, w_branch_pool, w_branch_sb, w_out, ffn2_norm, ffn2_gate, ffn2_up, ffn2_down, final_norm):
    assert ffn1_norm.shape[0] == 1, "single-layer kernel"
    mats = dict(ffn1_gate=ffn1_gate, ffn1_up=ffn1_up, ffn1_down=ffn1_down, w_in=w_in, pool_w=pool_w,
                w_branch_pool=w_branch_pool, w_branch_sb=w_branch_sb, w_out=w_out,
                ffn2_gate=ffn2_gate, ffn2_up=ffn2_up, ffn2_down=ffn2_down)
    w = {name: m[0].astype(BF16) for name, m in mats.items()}
    w.update(ffn1_norm=ffn1_norm, mix_norm=mix_norm, pool_scale=pool_scale, ffn2_norm=ffn2_norm,
             final_norm=final_norm.reshape(1, D_MODEL))

    b_p, s_p, _ = x_prompt.shape
    b_d, s_d, _ = x_sample.shape
    past = cache_k.shape[3]

    y_p, k_p, v_p, pool_p = _layer(
        x_prompt.reshape(b_p * s_p, D_MODEL), w, batch=b_p, seq=s_p, pos0=0,
        k_past=None, v_past=None, pool_prefix=jnp.zeros((b_p, POOL_STATE, D_POOL), F32),
        tm_ffn=512, tm_mix=512, tq=128, tk=128)

    def token_major(cache):
        return cache[0].transpose(0, 2, 1, 3).reshape(b_d, past, D_SB).astype(BF16)

    y_d, k_d, v_d, pool_d = _layer(
        x_sample.reshape(b_d * s_d, D_MODEL), w, batch=b_d, seq=s_d, pos0=past,
        k_past=token_major(cache_k), v_past=token_major(cache_v), pool_prefix=state_pool[0],
        tm_ffn=b_d * s_d, tm_mix=s_d, tq=s_d, tk=128)

    return (y_p, y_d, k_p[None], v_p[None], pool_p[None], k_d[None], v_d[None], pool_d[None])
```

```python
import functools

import jax
import jax.numpy as jnp
from jax import lax
from jax.experimental import pallas as pl
from jax.experimental.pallas import tpu as pltpu

F32 = jnp.float32
BF16 = jnp.bfloat16

D_MODEL = 1024
D_FF = 4 * D_MODEL
D_POOL = D_MODEL // 2
POOL_WINDOWS = (2, 4, 8, 16)
POOL_GROUP = D_POOL // len(POOL_WINDOWS)
POOL_STATE = max(POOL_WINDOWS) - 1
POOL_HALO = POOL_STATE + 1
SB_HEADS = 8
SB_HEAD_DIM = 64
D_SB = SB_HEADS * SB_HEAD_DIM
D_IN = D_POOL + 3 * D_SB + 2 * D_MODEL
RMS_EPS = 1e-6
LANES = 128
VMEM_LIMIT_BYTES = 56 * 1024 * 1024
FFN_CHUNK = 1024

LOG2E = 1.4426950408889634
SURVIVAL_UNDERFLOW_LOG2 = 150.0


def _rmsnorm(x, g):
    ms = jnp.mean(x * x, axis=-1, keepdims=True)
    return (x * lax.rsqrt(ms + RMS_EPS)) * g


def _ffn_kernel(*refs, n_ff, final):
    if final:
        x_ref, g_ref, wg_ref, wu_ref, wd_ref, fg_ref, o_ref, xn_ref, acc_ref = refs
    else:
        x_ref, g_ref, wg_ref, wu_ref, wd_ref, o_ref, xn_ref, acc_ref = refs
    j = pl.program_id(1)

    @pl.when(j == 0)
    def _():
        xn_ref[...] = _rmsnorm(x_ref[...], g_ref[...]).astype(BF16)
        acc_ref[...] = jnp.zeros_like(acc_ref)

    xn = xn_ref[...]
    gate = jnp.dot(xn, wg_ref[...], preferred_element_type=F32)
    up = jnp.dot(xn, wu_ref[...], preferred_element_type=F32)
    h = (gate * jax.nn.sigmoid(gate) * up).astype(BF16)
    acc_ref[...] += jnp.dot(h, wd_ref[...], preferred_element_type=F32)

    @pl.when(j == n_ff - 1)
    def _():
        y = x_ref[...] + 0.5 * acc_ref[...]
        if final:
            y = _rmsnorm(y, fg_ref[...])
        o_ref[...] = y


def _ffn(x, norm_g, wg, wu, wd, final_g=None, *, tm, tf):
    n = x.shape[0]
    n_ff = D_FF // tf
    final = final_g is not None
    row = pl.BlockSpec((1, D_MODEL), lambda i, j: (0, 0))
    in_specs = [
        pl.BlockSpec((tm, D_MODEL), lambda i, j: (i, 0)),
        row,
        pl.BlockSpec((D_MODEL, tf), lambda i, j: (0, j)),
        pl.BlockSpec((D_MODEL, tf), lambda i, j: (0, j)),
        pl.BlockSpec((tf, D_MODEL), lambda i, j: (j, 0)),
    ]
    args = [x, norm_g, wg, wu, wd]
    if final:
        in_specs.append(row)
        args.append(final_g)
    return pl.pallas_call(
        functools.partial(_ffn_kernel, n_ff=n_ff, final=final),
        grid=(n // tm, n_ff),
        in_specs=in_specs,
        out_specs=pl.BlockSpec((tm, D_MODEL), lambda i, j: (i, 0)),
        out_shape=jax.ShapeDtypeStruct((n, D_MODEL), F32),
        scratch_shapes=[pltpu.VMEM((tm, D_MODEL), BF16), pltpu.VMEM((tm, D_MODEL), F32)],
        compiler_params=pltpu.CompilerParams(
            dimension_semantics=("parallel", "arbitrary"), vmem_limit_bytes=VMEM_LIMIT_BYTES),
        name="ffn_final" if final else "ffn",
    )(*args)


def _proj_kernel(x_ref, g_ref, w_ref, u_ref, qkv_ref, k_ref, v_ref, gate_ref):
    h = _rmsnorm(x_ref[...], g_ref[...]).astype(BF16)

    def mm(c0, c1):
        return jnp.dot(h, w_ref[:, c0:c1], preferred_element_type=F32)

    c = 0
    u_ref[...] = mm(c, c + D_POOL)
    c += D_POOL
    qkv_ref[:, 0:D_SB] = (mm(c, c + D_SB) * (LOG2E * SB_HEAD_DIM ** -0.5)).astype(BF16)
    c += D_SB
    for idx, hm_ref in ((1, k_ref), (2, v_ref)):
        t = mm(c, c + D_SB)
        c += D_SB
        qkv_ref[:, idx * D_SB:(idx + 1) * D_SB] = t.astype(BF16)
        for hd in range(SB_HEADS):
            hm_ref[0, hd, :, :] = t[:, hd * SB_HEAD_DIM:(hd + 1) * SB_HEAD_DIM]
    gate_ref[:, 0:D_MODEL] = mm(c, c + D_MODEL)
    c += D_MODEL
    gate_ref[:, D_MODEL:2 * D_MODEL] = mm(c, c + D_MODEL)


def _proj(x, norm_g, w_in, *, batch, seq, tm):
    n = x.shape[0]
    tiles_per_seq = seq // tm
    hm_spec = pl.BlockSpec((1, SB_HEADS, tm, SB_HEAD_DIM),
                           lambda i: (i // tiles_per_seq, 0, i % tiles_per_seq, 0))
    hm_shape = jax.ShapeDtypeStruct((batch, SB_HEADS, seq, SB_HEAD_DIM), F32)
    return pl.pallas_call(
        _proj_kernel,
        grid=(n // tm,),
        in_specs=[
            pl.BlockSpec((tm, D_MODEL), lambda i: (i, 0)),
            pl.BlockSpec((1, D_MODEL), lambda i: (0, 0)),
            pl.BlockSpec((D_MODEL, D_IN), lambda i: (0, 0)),
        ],
        out_specs=[
            pl.BlockSpec((tm, D_POOL), lambda i: (i, 0)),
            pl.BlockSpec((tm, 3 * D_SB), lambda i: (i, 0)),
            hm_spec,
            hm_spec,
            pl.BlockSpec((tm, 2 * D_MODEL), lambda i: (i, 0)),
        ],
        out_shape=[
            jax.ShapeDtypeStruct((n, D_POOL), F32),
            jax.ShapeDtypeStruct((n, 3 * D_SB), BF16),
            hm_shape,
            hm_shape,
            jax.ShapeDtypeStruct((n, 2 * D_MODEL), F32),
        ],
        compiler_params=pltpu.CompilerParams(
            dimension_semantics=("parallel",), vmem_limit_bytes=VMEM_LIMIT_BYTES),
        name="proj",
    )(x, norm_g, w_in)


def _attn_kernel(q_ref, k_ref, v_ref, o_ref, acc_ref, car_ref, *, tq, tk, pos0):
    qi = pl.program_id(1)
    q_start = pos0 + qi * tq
    t_diag = q_start // tk

    lane = lax.broadcasted_iota(jnp.int32, (1, LANES), 1)
    first_head = lane < SB_HEAD_DIM
    r = lax.broadcasted_iota(jnp.int32, (tk, tk), 0)
    c = lax.broadcasted_iota(jnp.int32, (tk, tk), 1)
    newer = jnp.where(r > c, 1.0, 0.0).astype(BF16)

    acc_ref[...] = jnp.zeros_like(acc_ref)
    car_ref[...] = jnp.zeros_like(car_ref)

    heads = range(SB_HEADS)
    pair_cols = [slice((h // 2) * LANES, (h // 2 + 1) * LANES) for h in heads]
    head_sel = [first_head if h % 2 == 0 else jnp.logical_not(first_head) for h in heads]
    q_heads = [jnp.where(head_sel[h], q_ref[:, pair_cols[h]], jnp.zeros((tq, LANES), BF16))
               for h in heads]

    def process(tiles):
        k_ts, v_ts, valids = [], [], []
        for t, masked in tiles:
            start = pl.multiple_of(t * tk, tk)
            k_ts.append(k_ref[pl.ds(start, tk), :])
            v_ts.append(v_ref[pl.ds(start, tk), :])
            if masked:
                k_pos = start + lax.broadcasted_iota(jnp.int32, (1, tk), 1)
                q_pos = q_start + lax.broadcasted_iota(jnp.int32, (tq, 1), 0)
                valids.append(k_pos < q_pos)
            else:
                valids.append(None)
        idx = [(i, h) for i in range(len(tiles)) for h in heads]
        z, s, sums, w = {}, {}, {}, {}
        for i, h in idx:
            z[i, h] = lax.dot_general(q_heads[h], k_ts[i][:, pair_cols[h]],
                                      (((1,), (1,)), ((), ())), preferred_element_type=F32)
        for i, h in idx:
            s_ih = jnp.maximum(z[i, h], 0.0) + jnp.log2(1.0 + jnp.exp2(-jnp.abs(z[i, h])))
            s[i, h] = s_ih if valids[i] is None else jnp.where(valids[i], s_ih, 0.0)
        for i, h in idx:
            sums[i, h] = jnp.dot(s[i, h].astype(BF16), newer, preferred_element_type=F32)
        cars = [car_ref[h] for h in heads]
        for i, h in idx:
            w_ih = jnp.exp2((z[i, h] - s[i, h]) - sums[i, h] - cars[h])
            if valids[i] is not None:
                w_ih = jnp.where(valids[i], w_ih, 0.0)
            w[i, h] = w_ih.astype(BF16)
            cars[h] = cars[h] + jnp.sum(s[i, h], axis=1, keepdims=True)
        for h in heads:
            car_ref[h] = cars[h]
        for i, h in idx:
            v_h = jnp.where(head_sel[h], v_ts[i][:, pair_cols[h]], jnp.zeros((tk, LANES), BF16))
            acc_ref[:, pair_cols[h]] += jnp.dot(w[i, h], v_h, preferred_element_type=F32)
        car_min = functools.reduce(jnp.minimum, cars)
        return (jnp.min(car_min) >= SURVIVAL_UNDERFLOW_LOG2).astype(jnp.int32)

    def first_three():
        return t_diag - 3, process([(t_diag, True), (t_diag - 1, False), (t_diag - 2, False)])

    def first_one():
        return t_diag - 1, process([(t_diag, True)])

    t_next, done = lax.cond(t_diag >= 2, first_three, first_one)

    def cond(carry):
        t, done = carry
        return jnp.logical_and(t >= 0, done == 0)

    def body(carry):
        t, _ = carry
        return t - 1, process([(t, False)])

    lax.while_loop(cond, body, (t_next, done))
    o_ref[...] = acc_ref[...].astype(BF16)


def _attn(q_arr, q_col, k_arr, k_col, v_arr, v_col, *, batch, q_len, k_len, pos0, tq, tk):
    assert tk % tq == 0 and pos0 % tq == 0 and tk == LANES
    assert q_len % tq == 0 and k_len % tk == 0 and pos0 + q_len <= k_len
    nq = q_len // tq
    return pl.pallas_call(
        functools.partial(_attn_kernel, tq=tq, tk=tk, pos0=pos0),
        grid=(batch, nq),
        in_specs=[
            pl.BlockSpec((tq, D_SB), lambda b, i: (b * nq + i, q_col)),
            pl.BlockSpec((k_len, D_SB), lambda b, i: (b, k_col)),
            pl.BlockSpec((k_len, D_SB), lambda b, i: (b, v_col)),
        ],
        out_specs=pl.BlockSpec((tq, D_SB), lambda b, i: (b * nq + i, 0)),
        out_shape=jax.ShapeDtypeStruct((batch * q_len, D_SB), BF16),
        scratch_shapes=[pltpu.VMEM((tq, D_SB), F32), pltpu.VMEM((SB_HEADS, tq, LANES), F32)],
        compiler_params=pltpu.CompilerParams(
            dimension_semantics=("parallel", "arbitrary"), vmem_limit_bytes=VMEM_LIMIT_BYTES),
        name="attn",
    )(q_arr, k_arr, v_arr)


def _merge_kernel(x_ref, u_ref, halo_ref, o_ref, gate_ref, pw_ref, ps_ref, wbp_ref, wbs_ref,
                  wout_ref, y_ref, ext_ref, *, tm, pos0, tiles_per_seq):
    i = pl.program_id(0)
    ext_ref[0:POOL_HALO, :] = halo_ref[0]
    ext_ref[POOL_HALO:, :] = u_ref[...]
    pos = pos0 + (i % tiles_per_seq) * tm + lax.broadcasted_iota(jnp.int32, (tm, 1), 0)

    mixed = []
    for gi, win in enumerate(POOL_WINDOWS):
        cols = slice(gi * POOL_GROUP, (gi + 1) * POOL_GROUP)
        tot = ext_ref[POOL_HALO:POOL_HALO + tm, cols]
        for back in range(1, win):
            tot = tot + ext_ref[POOL_HALO - back:POOL_HALO - back + tm, cols]
        cnt = jnp.minimum(win, pos + 1).astype(F32)
        d = (tot / cnt - u_ref[:, cols]).astype(BF16)
        mixed.append(jnp.dot(d, pw_ref[gi], preferred_element_type=F32) * ps_ref[:, cols])
    a = jnp.concatenate(mixed, axis=1).astype(BF16)

    br_pool = jnp.dot(a, wbp_ref[...], preferred_element_type=F32)
    br_sb = jnp.dot(o_ref[...], wbs_ref[...], preferred_element_type=F32)
    merged = (jax.nn.sigmoid(gate_ref[:, 0:D_MODEL]) * br_pool
              + jax.nn.sigmoid(gate_ref[:, D_MODEL:2 * D_MODEL]) * br_sb)
    y_ref[...] = x_ref[...] + jnp.dot(merged.astype(BF16), wout_ref[...],
                                      preferred_element_type=F32)


def _merge(x, u, halo, o, gate, pool_w, pool_scale, w_bp, w_bs, w_out, *, tm, pos0, seq):
    n = x.shape[0]
    const2 = lambda i: (0, 0)
    return pl.pallas_call(
        functools.partial(_merge_kernel, tm=tm, pos0=pos0, tiles_per_seq=seq // tm),
        grid=(n // tm,),
        in_specs=[
            pl.BlockSpec((tm, D_MODEL), lambda i: (i, 0)),
            pl.BlockSpec((tm, D_POOL), lambda i: (i, 0)),
            pl.BlockSpec((1, POOL_HALO, D_POOL), lambda i: (i, 0, 0)),
            pl.BlockSpec((tm, D_SB), lambda i: (i, 0)),
            pl.BlockSpec((tm, 2 * D_MODEL), lambda i: (i, 0)),
            pl.BlockSpec((len(POOL_WINDOWS), POOL_GROUP, POOL_GROUP), lambda i: (0, 0, 0)),
            pl.BlockSpec((1, D_POOL), const2),
            pl.BlockSpec((D_POOL, D_MODEL), const2),
            pl.BlockSpec((D_SB, D_MODEL), const2),
            pl.BlockSpec((D_MODEL, D_MODEL), const2),
        ],
        out_specs=pl.BlockSpec((tm, D_MODEL), lambda i: (i, 0)),
        out_shape=jax.ShapeDtypeStruct((n, D_MODEL), F32),
        scratch_shapes=[pltpu.VMEM((tm + POOL_HALO, D_POOL), F32)],
        compiler_params=pltpu.CompilerParams(
            dimension_semantics=("parallel",), vmem_limit_bytes=VMEM_LIMIT_BYTES),
        name="merge",
    )(x, u, halo, o, gate, pool_w, pool_scale, w_bp, w_bs, w_out)


def _layer(x, w, *, batch, seq, pos0, k_past, v_past, pool_prefix, tm_ffn, tm_mix, tq, tk):
    x1 = _ffn(x, w["ffn1_norm"], w["ffn1_gate"], w["ffn1_up"], w["ffn1_down"], tm=tm_ffn,
              tf=FFN_CHUNK)
    u, qkv, k_hm, v_hm, gate = _proj(x1, w["mix_norm"], w["w_in"], batch=batch, seq=seq, tm=tm_mix)

    if k_past is None:
        o = _attn(qkv, 0, qkv, 1, qkv, 2, batch=batch, q_len=seq, k_len=seq, pos0=0, tq=tq, tk=tk)
    else:
        k_len = -(-(pos0 + seq) // tk) * tk
        def with_past(past, col):
            new = qkv[:, col * D_SB:(col + 1) * D_SB].reshape(batch, seq, D_SB)
            full = jnp.concatenate([past, new], axis=1)
            full = jnp.pad(full, ((0, 0), (0, k_len - full.shape[1]), (0, 0)))
            return full.reshape(batch * k_len, D_SB)
        o = _attn(qkv, 0, with_past(k_past, 1), 0, with_past(v_past, 2), 0,
                  batch=batch, q_len=seq, k_len=k_len, pos0=pos0, tq=tq, tk=tk)

    tiles = seq // tm_mix
    u4 = u.reshape(batch, tiles, tm_mix, D_POOL)
    first = jnp.pad(pool_prefix, ((0, 0), (1, 0), (0, 0)))[:, None]
    halo = jnp.concatenate([first, u4[:, :-1, tm_mix - POOL_HALO:, :]], axis=1)
    halo = halo.reshape(batch * tiles, POOL_HALO, D_POOL)

    x2 = _merge(x1, u, halo, o, gate, w["pool_w"], w["pool_scale"], w["w_branch_pool"],
                w["w_branch_sb"], w["w_out"], tm=tm_mix, pos0=pos0, seq=seq)
    y = _ffn(x2, w["ffn2_norm"], w["ffn2_gate"], w["ffn2_up"], w["ffn2_down"], w["final_norm"],
             tm=tm_ffn, tf=FFN_CHUNK)
    new_pool = u.reshape(batch, seq, D_POOL)[:, seq - POOL_STATE:, :]
    return y.reshape(batch, seq, D_MODEL), k_hm, v_hm, new_pool


def kernel(x_prompt, x_sample, cache_k, cache_v, state_pool, ffn1_norm, ffn1_gate, ffn1_up, ffn1_down, mix_norm, w_in, pool_w, pool_scale, w_branch_pool, w_branch_sb, w_out, ffn2_norm, ffn2_gate, ffn2_up, ffn2_down, final_norm):
    assert ffn1_norm.shape[0] == 1, "single-layer kernel"
    mats = dict(ffn1_gate=ffn1_gate, ffn1_up=ffn1_up, ffn1_down=ffn1_down, w_in=w_in, pool_w=pool_w,
                w_branch_pool=w_branch_pool, w_branch_sb=w_branch_sb, w_out=w_out,
                ffn2_gate=ffn2_gate, ffn2_up=ffn2_up, ffn2_down=ffn2_down)
    w = {name: m[0].astype(BF16) for name, m in mats.items()}
    w.update(ffn1_norm=ffn1_norm, mix_norm=mix_norm, pool_scale=pool_scale, ffn2_norm=ffn2_norm,
             final_norm=final_norm.reshape(1, D_MODEL))

    b_p, s_p, _ = x_prompt.shape
    b_d, s_d, _ = x_sample.shape
    past = cache_k.shape[3]

    y_p, k_p, v_p, pool_p = _layer(
        x_prompt.reshape(b_p * s_p, D_MODEL), w, batch=b_p, seq=s_p, pos0=0,
        k_past=None, v_past=None, pool_prefix=jnp.zeros((b_p, POOL_STATE, D_POOL), F32),
        tm_ffn=1024, tm_mix=512, tq=128, tk=128)

    def token_major(cache):
        return cache[0].transpose(0, 2, 1, 3).reshape(b_d, past, D_SB).astype(BF16)

    y_d, k_d, v_d, pool_d = _layer(
        x_sample.reshape(b_d * s_d, D_MODEL), w, batch=b_d, seq=s_d, pos0=past,
        k_past=token_major(cache_k), v_past=token_major(cache_v), pool_prefix=state_pool[0],
        tm_ffn=b_d * s_d, tm_mix=s_d, tq=s_d, tk=128)

    return (y_p, y_d, k_p[None], v_p[None], pool_p[None], k_d[None], v_d[None], pool_d[None])
```

```python
import functools

import jax
import jax.numpy as jnp
from jax import lax
from jax.experimental import pallas as pl
from jax.experimental.pallas import tpu as pltpu

F32 = jnp.float32
BF16 = jnp.bfloat16

D_MODEL = 1024
D_FF = 4 * D_MODEL
D_POOL = D_MODEL // 2
POOL_WINDOWS = (2, 4, 8, 16)
POOL_GROUP = D_POOL // len(POOL_WINDOWS)
POOL_STATE = max(POOL_WINDOWS) - 1
POOL_HALO = POOL_STATE + 1
SB_HEADS = 8
SB_HEAD_DIM = 64
D_SB = SB_HEADS * SB_HEAD_DIM
D_IN = D_POOL + 3 * D_SB + 2 * D_MODEL
RMS_EPS = 1e-6
LANES = 128
VMEM_LIMIT_BYTES = 56 * 1024 * 1024
FFN_CHUNK = 1024
ATTN_WINDOW = 2 * LANES

LOG2E = 1.4426950408889634
SURVIVAL_UNDERFLOW_LOG2 = 150.0
MASKED_LOGIT = -1e30


def _rmsnorm(x, g):
    ms = jnp.mean(x * x, axis=-1, keepdims=True)
    return (x * lax.rsqrt(ms + RMS_EPS)) * g


def _ffn_kernel(*refs, n_ff, final):
    if final:
        x_ref, g_ref, wg_ref, wu_ref, wd_ref, fg_ref, o_ref, xn_ref, acc_ref = refs
    else:
        x_ref, g_ref, wg_ref, wu_ref, wd_ref, o_ref, xn_ref, acc_ref = refs
    j = pl.program_id(1)

    @pl.when(j == 0)
    def _():
        xn_ref[...] = _rmsnorm(x_ref[...], g_ref[...]).astype(BF16)
        acc_ref[...] = jnp.zeros_like(acc_ref)

    xn = xn_ref[...]
    gate = jnp.dot(xn, wg_ref[...], preferred_element_type=F32)
    up = jnp.dot(xn, wu_ref[...], preferred_element_type=F32)
    h = (gate * jax.nn.sigmoid(gate) * up).astype(BF16)
    acc_ref[...] += jnp.dot(h, wd_ref[...], preferred_element_type=F32)

    @pl.when(j == n_ff - 1)
    def _():
        y = x_ref[...] + 0.5 * acc_ref[...]
        if final:
            y = _rmsnorm(y, fg_ref[...])
        o_ref[...] = y


def _ffn(x, norm_g, wg, wu, wd, final_g=None, *, tm, tf):
    n = x.shape[0]
    n_ff = D_FF // tf
    final = final_g is not None
    row = pl.BlockSpec((1, D_MODEL), lambda i, j: (0, 0))
    in_specs = [
        pl.BlockSpec((tm, D_MODEL), lambda i, j: (i, 0)),
        row,
        pl.BlockSpec((D_MODEL, tf), lambda i, j: (0, j)),
        pl.BlockSpec((D_MODEL, tf), lambda i, j: (0, j)),
        pl.BlockSpec((tf, D_MODEL), lambda i, j: (j, 0)),
    ]
    args = [x, norm_g, wg, wu, wd]
    if final:
        in_specs.append(row)
        args.append(final_g)
    return pl.pallas_call(
        functools.partial(_ffn_kernel, n_ff=n_ff, final=final),
        grid=(n // tm, n_ff),
        in_specs=in_specs,
        out_specs=pl.BlockSpec((tm, D_MODEL), lambda i, j: (i, 0)),
        out_shape=jax.ShapeDtypeStruct((n, D_MODEL), F32),
        scratch_shapes=[pltpu.VMEM((tm, D_MODEL), BF16), pltpu.VMEM((tm, D_MODEL), F32)],
        compiler_params=pltpu.CompilerParams(
            dimension_semantics=("parallel", "arbitrary"), vmem_limit_bytes=VMEM_LIMIT_BYTES),
        name="ffn_final" if final else "ffn",
    )(*args)


def _proj_kernel(x_ref, g_ref, w_ref, u_ref, qkv_ref, k_ref, v_ref, gate_ref):
    h = _rmsnorm(x_ref[...], g_ref[...]).astype(BF16)

    def mm(c0, c1):
        return jnp.dot(h, w_ref[:, c0:c1], preferred_element_type=F32)

    c = 0
    u_ref[...] = mm(c, c + D_POOL)
    c += D_POOL
    qkv_ref[:, 0:D_SB] = (mm(c, c + D_SB) * (LOG2E * SB_HEAD_DIM ** -0.5)).astype(BF16)
    c += D_SB
    for idx, hm_ref in ((1, k_ref), (2, v_ref)):
        t = mm(c, c + D_SB)
        c += D_SB
        qkv_ref[:, idx * D_SB:(idx + 1) * D_SB] = t.astype(BF16)
        for hd in range(SB_HEADS):
            hm_ref[0, hd, :, :] = t[:, hd * SB_HEAD_DIM:(hd + 1) * SB_HEAD_DIM]
    gate_ref[:, 0:D_MODEL] = mm(c, c + D_MODEL).astype(BF16)
    c += D_MODEL
    gate_ref[:, D_MODEL:2 * D_MODEL] = mm(c, c + D_MODEL).astype(BF16)


def _proj(x, norm_g, w_in, *, batch, seq, tm):
    n = x.shape[0]
    tiles_per_seq = seq // tm
    hm_spec = pl.BlockSpec((1, SB_HEADS, tm, SB_HEAD_DIM),
                           lambda i: (i // tiles_per_seq, 0, i % tiles_per_seq, 0))
    hm_shape = jax.ShapeDtypeStruct((batch, SB_HEADS, seq, SB_HEAD_DIM), F32)
    return pl.pallas_call(
        _proj_kernel,
        grid=(n // tm,),
        in_specs=[
            pl.BlockSpec((tm, D_MODEL), lambda i: (i, 0)),
            pl.BlockSpec((1, D_MODEL), lambda i: (0, 0)),
            pl.BlockSpec((D_MODEL, D_IN), lambda i: (0, 0)),
        ],
        out_specs=[
            pl.BlockSpec((tm, D_POOL), lambda i: (i, 0)),
            pl.BlockSpec((tm, 3 * D_SB), lambda i: (i, 0)),
            hm_spec,
            hm_spec,
            pl.BlockSpec((tm, 2 * D_MODEL), lambda i: (i, 0)),
        ],
        out_shape=[
            jax.ShapeDtypeStruct((n, D_POOL), F32),
            jax.ShapeDtypeStruct((n, 3 * D_SB), BF16),
            hm_shape,
            hm_shape,
            jax.ShapeDtypeStruct((n, 2 * D_MODEL), BF16),
        ],
        compiler_params=pltpu.CompilerParams(
            dimension_semantics=("parallel",), vmem_limit_bytes=VMEM_LIMIT_BYTES),
        name="proj",
    )(x, norm_g, w_in)


def _attn_kernel(q_ref, k_ref, v_ref, o_ref, acc_ref, car_ref, *, tq, hq, pos0):
    win = ATTN_WINDOW
    q_start = pos0 + pl.program_id(1) * tq
    subs = range(tq // hq)
    pairs = range(SB_HEADS // 2)
    pair_cols = [slice(p * LANES, (p + 1) * LANES) for p in pairs]

    lane = lax.broadcasted_iota(jnp.int32, (1, LANES), 1)
    first_head = lane < SB_HEAD_DIM
    r = lax.broadcasted_iota(jnp.int32, (win, win), 0)
    c = lax.broadcasted_iota(jnp.int32, (win, win), 1)
    newer = jnp.where(r > c, 1.0, 0.0).astype(BF16)

    def softplus2(z):
        return jnp.maximum(z, 0.0) + jnp.log2(1.0 + jnp.exp2(-jnp.abs(z)))

    def visit(m, mode):
        k_ws, v_ws, valids = [], [], []
        for j in subs:
            row0 = q_start + j * hq
            win_end = row0 + hq - m * win
            q_pos = row0 + lax.broadcasted_iota(jnp.int32, (hq, 1), 0)
            if mode == "inner":
                start = pl.multiple_of(win_end - win, hq)
                k_pos = (win_end - LANES) + lax.broadcasted_iota(jnp.int32, (1, LANES), 1)
                valids.append(k_pos < q_pos)
            else:
                start = pl.multiple_of(jnp.maximum(win_end - win, 0), hq)
                k_pos = start + lax.broadcasted_iota(jnp.int32, (1, win), 1)
                valids.append(k_pos < (q_pos if mode == "edge" else win_end))
            k_ws.append(k_ref[pl.ds(start, win), :])
            v_ws.append(v_ref[pl.ds(start, win), :])

        def mask(x, j):
            if mode == "inner":
                return jnp.concatenate(
                    [x[:, :win - LANES],
                     jnp.where(valids[j], x[:, win - LANES:], MASKED_LOGIT)], axis=1)
            return jnp.where(valids[j], x, MASKED_LOGIT)

        z_parts = []
        for j in subs:
            rows = slice(j * hq, (j + 1) * hq)
            for p in pairs:
                q_p = q_ref[rows, pair_cols[p]]
                zero = jnp.zeros_like(q_p)
                q_2 = jnp.concatenate([jnp.where(first_head, q_p, zero),
                                       jnp.where(first_head, zero, q_p)], axis=0)
                z_2 = lax.dot_general(q_2, k_ws[j][:, pair_cols[p]], (((1,), (1,)), ((), ())),
                                      preferred_element_type=F32)
                z_parts += [mask(z_2[:hq], j), mask(z_2[hq:], j)]
        z = jnp.concatenate(z_parts, axis=0)
        s = softplus2(z)
        sums = jnp.dot(s.astype(BF16), newer, preferred_element_type=F32)
        total = jnp.broadcast_to(jnp.sum(s, axis=1, keepdims=True), (s.shape[0], LANES))
        arg = (z - s) - sums
        if mode == "older":
            car = car_ref[...]
            arg = arg - jnp.concatenate([car] * (win // LANES), axis=1)
            total = car + total
        car_ref[...] = total
        w = jnp.exp2(arg).astype(BF16)

        i = 0
        for j in subs:
            rows = slice(j * hq, (j + 1) * hq)
            for p in pairs:
                o_2 = jnp.dot(w[i * hq:(i + 2) * hq], v_ws[j][:, pair_cols[p]],
                              preferred_element_type=F32)
                i += 2
                o_p = jnp.where(first_head, o_2[:hq], o_2[hq:])
                if mode == "older":
                    acc_ref[rows, pair_cols[p]] += o_p
                else:
                    acc_ref[rows, pair_cols[p]] = o_p
        return (jnp.min(total) >= SURVIVAL_UNDERFLOW_LOG2).astype(jnp.int32)

    done = lax.cond(q_start + hq >= win, lambda: visit(0, "inner"), lambda: visit(0, "edge"))

    def cond(carry):
        m, done = carry
        return jnp.logical_and(q_start + tq - m * win > 0, done == 0)

    def body(carry):
        m, _ = carry
        return m + 1, visit(m, "older")

    lax.while_loop(cond, body, (jnp.int32(1), done))
    o_ref[...] = acc_ref[...].astype(BF16)


def _attn(q_arr, q_col, k_arr, k_col, v_arr, v_col, *, batch, q_len, k_len, pos0, tq, hq):
    assert tq % hq == 0 and q_len % tq == 0 and pos0 % tq == 0 and ATTN_WINDOW % hq == 0
    assert pos0 + q_len <= k_len and k_len >= ATTN_WINDOW and hq % 16 == 0
    nq = q_len // tq
    stacked_rows = (tq // hq) * SB_HEADS * hq
    return pl.pallas_call(
        functools.partial(_attn_kernel, tq=tq, hq=hq, pos0=pos0),
        grid=(batch, nq),
        in_specs=[
            pl.BlockSpec((tq, D_SB), lambda b, i: (b * nq + i, q_col)),
            pl.BlockSpec((k_len, D_SB), lambda b, i: (b, k_col)),
            pl.BlockSpec((k_len, D_SB), lambda b, i: (b, v_col)),
        ],
        out_specs=pl.BlockSpec((tq, D_SB), lambda b, i: (b * nq + i, 0)),
        out_shape=jax.ShapeDtypeStruct((batch * q_len, D_SB), BF16),
        scratch_shapes=[pltpu.VMEM((tq, D_SB), F32), pltpu.VMEM((stacked_rows, LANES), F32)],
        compiler_params=pltpu.CompilerParams(
            dimension_semantics=("parallel", "arbitrary"), vmem_limit_bytes=VMEM_LIMIT_BYTES),
        name="attn",
    )(q_arr, k_arr, v_arr)


def _merge_kernel(x_ref, u_ref, halo_ref, o_ref, gate_ref, pw_ref, ps_ref, wbp_ref, wbs_ref,
                  wout_ref, y_ref, ext_ref, *, tm, pos0, tiles_per_seq):
    i = pl.program_id(0)
    ext_ref[0:POOL_HALO, :] = halo_ref[0]
    ext_ref[POOL_HALO:, :] = u_ref[...]
    pos = pos0 + (i % tiles_per_seq) * tm + lax.broadcasted_iota(jnp.int32, (tm, 1), 0)

    mixed = []
    for gi, win in enumerate(POOL_WINDOWS):
        cols = slice(gi * POOL_GROUP, (gi + 1) * POOL_GROUP)
        tot = ext_ref[POOL_HALO:POOL_HALO + tm, cols]
        for back in range(1, win):
            tot = tot + ext_ref[POOL_HALO - back:POOL_HALO - back + tm, cols]
        cnt = jnp.minimum(win, pos + 1).astype(F32)
        d = (tot / cnt - u_ref[:, cols]).astype(BF16)
        mixed.append(jnp.dot(d, pw_ref[gi], preferred_element_type=F32) * ps_ref[:, cols])
    a = jnp.concatenate(mixed, axis=1).astype(BF16)

    br_pool = jnp.dot(a, wbp_ref[...], preferred_element_type=F32)
    br_sb = jnp.dot(o_ref[...], wbs_ref[...], preferred_element_type=F32)
    merged = (jax.nn.sigmoid(gate_ref[:, 0:D_MODEL].astype(F32)) * br_pool
              + jax.nn.sigmoid(gate_ref[:, D_MODEL:2 * D_MODEL].astype(F32)) * br_sb)
    y_ref[...] = x_ref[...] + jnp.dot(merged.astype(BF16), wout_ref[...],
                                      preferred_element_type=F32)


def _merge(x, u, halo, o, gate, pool_w, pool_scale, w_bp, w_bs, w_out, *, tm, pos0, seq):
    n = x.shape[0]
    const2 = lambda i: (0, 0)
    return pl.pallas_call(
        functools.partial(_merge_kernel, tm=tm, pos0=pos0, tiles_per_seq=seq // tm),
        grid=(n // tm,),
        in_specs=[
            pl.BlockSpec((tm, D_MODEL), lambda i: (i, 0)),
            pl.BlockSpec((tm, D_POOL), lambda i: (i, 0)),
            pl.BlockSpec((1, POOL_HALO, D_POOL), lambda i: (i, 0, 0)),
            pl.BlockSpec((tm, D_SB), lambda i: (i, 0)),
            pl.BlockSpec((tm, 2 * D_MODEL), lambda i: (i, 0)),
            pl.BlockSpec((len(POOL_WINDOWS), POOL_GROUP, POOL_GROUP), lambda i: (0, 0, 0)),
            pl.BlockSpec((1, D_POOL), const2),
            pl.BlockSpec((D_POOL, D_MODEL), const2),
            pl.BlockSpec((D_SB, D_MODEL), const2),
            pl.BlockSpec((D_MODEL, D_MODEL), const2),
        ],
        out_specs=pl.BlockSpec((tm, D_MODEL), lambda i: (i, 0)),
        out_shape=jax.ShapeDtypeStruct((n, D_MODEL), F32),
        scratch_shapes=[pltpu.VMEM((tm + POOL_HALO, D_POOL), F32)],
        compiler_params=pltpu.CompilerParams(
            dimension_semantics=("parallel",), vmem_limit_bytes=VMEM_LIMIT_BYTES),
        name="merge",
    )(x, u, halo, o, gate, pool_w, pool_scale, w_bp, w_bs, w_out)


def _layer(x, w, *, batch, seq, pos0, k_past, v_past, pool_prefix, tm_ffn, tm_mix, tq, hq):
    assert seq >= POOL_STATE
    x1 = _ffn(x, w["ffn1_norm"], w["ffn1_gate"], w["ffn1_up"], w["ffn1_down"], tm=tm_ffn,
              tf=FFN_CHUNK)
    u, qkv, k_hm, v_hm, gate = _proj(x1, w["mix_norm"], w["w_in"], batch=batch, seq=seq, tm=tm_mix)

    if k_past is None:
        o = _attn(qkv, 0, qkv, 1, qkv, 2, batch=batch, q_len=seq, k_len=seq, pos0=0, tq=tq, hq=hq)
    else:
        def with_past(past, col):
            new = qkv[:, col * D_SB:(col + 1) * D_SB].reshape(batch, seq, D_SB)
            return jnp.concatenate([past, new], axis=1).reshape(batch * (pos0 + seq), D_SB)
        o = _attn(qkv, 0, with_past(k_past, 1), 0, with_past(v_past, 2), 0,
                  batch=batch, q_len=seq, k_len=pos0 + seq, pos0=pos0, tq=tq, hq=hq)

    tiles = seq // tm_mix
    u4 = u.reshape(batch, tiles, tm_mix, D_POOL)
    first = jnp.pad(pool_prefix, ((0, 0), (1, 0), (0, 0)))[:, None]
    halo = jnp.concatenate([first, u4[:, :-1, tm_mix - POOL_HALO:, :]], axis=1)
    halo = halo.reshape(batch * tiles, POOL_HALO, D_POOL)

    x2 = _merge(x1, u, halo, o, gate, w["pool_w"], w["pool_scale"], w["w_branch_pool"],
                w["w_branch_sb"], w["w_out"], tm=tm_mix, pos0=pos0, seq=seq)
    y = _ffn(x2, w["ffn2_norm"], w["ffn2_gate"], w["ffn2_up"], w["ffn2_down"], w["final_norm"],
             tm=tm_ffn, tf=FFN_CHUNK)
    new_pool = u.reshape(batch, seq, D_POOL)[:, seq - POOL_STATE:, :]
    return y.reshape(batch, seq, D_MODEL), k_hm, v_hm, new_pool


def kernel(x_prompt, x_sample, cache_k, cache_v, state_pool, ffn1_norm, ffn1_gate, ffn1_up, ffn1_down, mix_norm, w_in, pool_w, pool_scale, w_branch_pool, w_branch_sb, w_out, ffn2_norm, ffn2_gate, ffn2_up, ffn2_down, final_norm):
    assert ffn1_norm.shape[0] == 1, "single-layer kernel"
    mats = dict(ffn1_gate=ffn1_gate, ffn1_up=ffn1_up, ffn1_down=ffn1_down, w_in=w_in, pool_w=pool_w,
                w_branch_pool=w_branch_pool, w_branch_sb=w_branch_sb, w_out=w_out,
                ffn2_gate=ffn2_gate, ffn2_up=ffn2_up, ffn2_down=ffn2_down)
    w = {name: m[0].astype(BF16) for name, m in mats.items()}
    w.update(ffn1_norm=ffn1_norm, mix_norm=mix_norm, pool_scale=pool_scale, ffn2_norm=ffn2_norm,
             final_norm=final_norm.reshape(1, D_MODEL))

    b_p, s_p, _ = x_prompt.shape
    b_d, s_d, _ = x_sample.shape
    past = cache_k.shape[3]

    y_p, k_p, v_p, pool_p = _layer(
        x_prompt.reshape(b_p * s_p, D_MODEL), w, batch=b_p, seq=s_p, pos0=0,
        k_past=None, v_past=None, pool_prefix=jnp.zeros((b_p, POOL_STATE, D_POOL), F32),
        tm_ffn=1024, tm_mix=512, tq=512, hq=64)

    def token_major(cache):
        return cache[0].transpose(0, 2, 1, 3).reshape(b_d, past, D_SB).astype(BF16)

    y_d, k_d, v_d, pool_d = _layer(
        x_sample.reshape(b_d * s_d, D_MODEL), w, batch=b_d, seq=s_d, pos0=past,
        k_past=token_major(cache_k), v_past=token_major(cache_v), pool_prefix=state_pool[0],
        tm_ffn=b_d * s_d, tm_mix=s_d, tq=s_d, hq=s_d)

    return (y_p, y_d, k_p[None], v_p[None], pool_p[None], k_d[None], v_d[None], pool_d[None])
```

```python
import functools

import jax
import jax.numpy as jnp
from jax import lax
from jax.experimental import pallas as pl
from jax.experimental.pallas import tpu as pltpu

F32 = jnp.float32
BF16 = jnp.bfloat16

D_MODEL = 1024
D_FF = 4 * D_MODEL
D_POOL = D_MODEL // 2
POOL_WINDOWS = (2, 4, 8, 16)
POOL_GROUP = D_POOL // len(POOL_WINDOWS)
POOL_STATE = max(POOL_WINDOWS) - 1
POOL_HALO = POOL_STATE + 1
SB_HEADS = 8
SB_HEAD_DIM = 64
D_SB = SB_HEADS * SB_HEAD_DIM
D_IN = D_POOL + 3 * D_SB + 2 * D_MODEL
RMS_EPS = 1e-6
LANES = 128
VMEM_LIMIT_BYTES = 56 * 1024 * 1024
FFN_CHUNK = 1024
ATTN_WINDOW = 2 * LANES

LOG2E = 1.4426950408889634
SURVIVAL_UNDERFLOW_LOG2 = 150.0
MASKED_LOGIT = -1e30


def _rmsnorm(x, g):
    ms = jnp.mean(x * x, axis=-1, keepdims=True)
    return (x * lax.rsqrt(ms + RMS_EPS)) * g


def _ffn_kernel(*refs, n_ff, final, cast):
    refs = list(refs)
    x_ref, xnext_ref, g_ref, wg_ref, wu_ref, wd_ref = refs[:6]
    del refs[:6]
    fg_ref = refs.pop(0) if final else None
    o_ref = refs.pop(0)
    if cast:
        wg_out, wu_out, wd_out = refs[:3]
        del refs[:3]
    xn_ref, acc_ref = refs
    i = pl.program_id(0)
    j = pl.program_id(1)
    slot = i % 2

    @pl.when(jnp.logical_and(i == 0, j == 0))
    def _():
        xn_ref[0] = _rmsnorm(x_ref[...], g_ref[...]).astype(BF16)

    def chunk():
        wg, wu, wd = wg_ref[...], wu_ref[...], wd_ref[...]
        if cast:
            wg, wu, wd = wg.astype(BF16), wu.astype(BF16), wd.astype(BF16)
            wg_out[...] = wg
            wu_out[...] = wu
            wd_out[...] = wd
        xn = xn_ref[slot]
        gate = jnp.dot(xn, wg, preferred_element_type=F32)
        up = jnp.dot(xn, wu, preferred_element_type=F32)
        h = (gate * jax.nn.sigmoid(gate) * up).astype(BF16)
        return jnp.dot(h, wd, preferred_element_type=F32)

    def first():
        acc_ref[...] = chunk()

    def middle():
        acc_ref[...] += chunk()

    def last():
        y = x_ref[...] + 0.5 * (acc_ref[...] + chunk())
        if final:
            y = _rmsnorm(y, fg_ref[...])
        o_ref[...] = y
        xn_ref[1 - slot] = _rmsnorm(xnext_ref[...], g_ref[...]).astype(BF16)

    case = jnp.where(j == 0, 0, jnp.where(j == n_ff - 1, 2, 1))
    lax.switch(case, [first, middle, last])


def _ffn(x, norm_g, wg, wu, wd, final_g=None, *, tm, tf):
    n = x.shape[0]
    n_rows = n // tm
    n_ff = D_FF // tf
    assert n_ff >= 2
    final = final_g is not None
    cast = wg.dtype == F32
    assert not cast or n_rows == 1
    row = pl.BlockSpec((1, D_MODEL), lambda i, j: (0, 0))
    w_specs = [
        pl.BlockSpec((D_MODEL, tf), lambda i, j: (0, j)),
        pl.BlockSpec((D_MODEL, tf), lambda i, j: (0, j)),
        pl.BlockSpec((tf, D_MODEL), lambda i, j: (j, 0)),
    ]
    in_specs = [
        pl.BlockSpec((tm, D_MODEL), lambda i, j: (i, 0)),
        pl.BlockSpec((tm, D_MODEL), lambda i, j: (jnp.minimum(i + 1, n_rows - 1), 0)),
        row,
    ] + w_specs
    args = [x, x, norm_g, wg, wu, wd]
    if final:
        in_specs.append(row)
        args.append(final_g)
    out_specs = [pl.BlockSpec((tm, D_MODEL), lambda i, j: (i, 0))]
    out_shape = [jax.ShapeDtypeStruct((n, D_MODEL), F32)]
    if cast:
        out_specs += w_specs
        out_shape += [jax.ShapeDtypeStruct(m.shape, BF16) for m in (wg, wu, wd)]
    outs = pl.pallas_call(
        functools.partial(_ffn_kernel, n_ff=n_ff, final=final, cast=cast),
        grid=(n_rows, n_ff),
        in_specs=in_specs,
        out_specs=out_specs,
        out_shape=out_shape,
        scratch_shapes=[pltpu.VMEM((2, tm, D_MODEL), BF16), pltpu.VMEM((tm, D_MODEL), F32)],
        compiler_params=pltpu.CompilerParams(
            dimension_semantics=("arbitrary", "arbitrary"), vmem_limit_bytes=VMEM_LIMIT_BYTES),
        name=("ffn_final" if final else "ffn") + ("_cast" if cast else ""),
    )(*args)
    return outs if cast else outs[0]


def _proj_kernel(x_ref, g_ref, w_ref, u_ref, qkv_ref, k_ref, v_ref, gate_ref):
    h = _rmsnorm(x_ref[...], g_ref[...]).astype(BF16)

    def mm(c0, c1):
        return jnp.dot(h, w_ref[:, c0:c1], preferred_element_type=F32)

    c = 0
    u_ref[...] = mm(c, c + D_POOL)
    c += D_POOL
    qkv_ref[:, 0:D_SB] = (mm(c, c + D_SB) * (LOG2E * SB_HEAD_DIM ** -0.5)).astype(BF16)
    c += D_SB
    for idx, hm_ref in ((1, k_ref), (2, v_ref)):
        t = mm(c, c + D_SB)
        c += D_SB
        qkv_ref[:, idx * D_SB:(idx + 1) * D_SB] = t.astype(BF16)
        for hd in range(SB_HEADS):
            hm_ref[0, hd, :, :] = t[:, hd * SB_HEAD_DIM:(hd + 1) * SB_HEAD_DIM]
    gate_ref[:, 0:D_MODEL] = mm(c, c + D_MODEL).astype(BF16)
    c += D_MODEL
    gate_ref[:, D_MODEL:2 * D_MODEL] = mm(c, c + D_MODEL).astype(BF16)


def _proj(x, norm_g, w_in, *, batch, seq, tm):
    n = x.shape[0]
    tiles_per_seq = seq // tm
    hm_spec = pl.BlockSpec((1, SB_HEADS, tm, SB_HEAD_DIM),
                           lambda i: (i // tiles_per_seq, 0, i % tiles_per_seq, 0))
    hm_shape = jax.ShapeDtypeStruct((batch, SB_HEADS, seq, SB_HEAD_DIM), F32)
    return pl.pallas_call(
        _proj_kernel,
        grid=(n // tm,),
        in_specs=[
            pl.BlockSpec((tm, D_MODEL), lambda i: (i, 0)),
            pl.BlockSpec((1, D_MODEL), lambda i: (0, 0)),
            pl.BlockSpec((D_MODEL, D_IN), lambda i: (0, 0)),
        ],
        out_specs=[
            pl.BlockSpec((tm, D_POOL), lambda i: (i, 0)),
            pl.BlockSpec((tm, 3 * D_SB), lambda i: (i, 0)),
            hm_spec,
            hm_spec,
            pl.BlockSpec((tm, 2 * D_MODEL), lambda i: (i, 0)),
        ],
        out_shape=[
            jax.ShapeDtypeStruct((n, D_POOL), F32),
            jax.ShapeDtypeStruct((n, 3 * D_SB), BF16),
            hm_shape,
            hm_shape,
            jax.ShapeDtypeStruct((n, 2 * D_MODEL), BF16),
        ],
        compiler_params=pltpu.CompilerParams(
            dimension_semantics=("parallel",), vmem_limit_bytes=VMEM_LIMIT_BYTES),
        name="proj",
    )(x, norm_g, w_in)


def _attn_kernel(q_ref, k_ref, v_ref, o_ref, acc_ref, car_ref, *, tq, hq, pos0):
    win = ATTN_WINDOW
    q_start = pos0 + pl.program_id(1) * tq
    subs = range(tq // hq)
    pairs = range(SB_HEADS // 2)
    pair_cols = [slice(p * LANES, (p + 1) * LANES) for p in pairs]

    lane = lax.broadcasted_iota(jnp.int32, (1, LANES), 1)
    first_head = lane < SB_HEAD_DIM
    r = lax.broadcasted_iota(jnp.int32, (win, win), 0)
    c = lax.broadcasted_iota(jnp.int32, (win, win), 1)
    newer = jnp.where(r > c, 1.0, 0.0).astype(BF16)

    def softplus2(z):
        return jnp.maximum(z, 0.0) + jnp.log2(1.0 + jnp.exp2(-jnp.abs(z)))

    def visit(m, mode):
        k_ws, v_ws, valids = [], [], []
        for j in subs:
            row0 = q_start + j * hq
            win_end = row0 + hq - m * win
            q_pos = row0 + lax.broadcasted_iota(jnp.int32, (hq, 1), 0)
            if mode == "inner":
                start = pl.multiple_of(win_end - win, hq)
                k_pos = (win_end - LANES) + lax.broadcasted_iota(jnp.int32, (1, LANES), 1)
                valids.append(k_pos < q_pos)
            else:
                start = pl.multiple_of(jnp.maximum(win_end - win, 0), hq)
                k_pos = start + lax.broadcasted_iota(jnp.int32, (1, win), 1)
                valids.append(k_pos < (q_pos if mode == "edge" else win_end))
            k_ws.append(k_ref[pl.ds(start, win), :])
            v_ws.append(v_ref[pl.ds(start, win), :])

        def mask(x, j):
            if mode == "inner":
                return jnp.concatenate(
                    [x[:, :win - LANES],
                     jnp.where(valids[j], x[:, win - LANES:], MASKED_LOGIT)], axis=1)
            return jnp.where(valids[j], x, MASKED_LOGIT)

        z_parts = []
        for j in subs:
            rows = slice(j * hq, (j + 1) * hq)
            for p in pairs:
                q_p = q_ref[rows, pair_cols[p]]
                zero = jnp.zeros_like(q_p)
                q_2 = jnp.concatenate([jnp.where(first_head, q_p, zero),
                                       jnp.where(first_head, zero, q_p)], axis=0)
                z_2 = lax.dot_general(q_2, k_ws[j][:, pair_cols[p]], (((1,), (1,)), ((), ())),
                                      preferred_element_type=F32)
                z_parts += [mask(z_2[:hq], j), mask(z_2[hq:], j)]
        z = jnp.concatenate(z_parts, axis=0)
        s = softplus2(z)
        sums = jnp.dot(s.astype(BF16), newer, preferred_element_type=F32)
        total = jnp.broadcast_to(jnp.sum(s, axis=1, keepdims=True), (s.shape[0], LANES))
        arg = (z - s) - sums
        if mode == "older":
            car = car_ref[...]
            arg = arg - jnp.concatenate([car] * (win // LANES), axis=1)
            total = car + total
        car_ref[...] = total
        w = jnp.exp2(arg).astype(BF16)

        i = 0
        for j in subs:
            rows = slice(j * hq, (j + 1) * hq)
            for p in pairs:
                o_2 = jnp.dot(w[i * hq:(i + 2) * hq], v_ws[j][:, pair_cols[p]],
                              preferred_element_type=F32)
                i += 2
                o_p = jnp.where(first_head, o_2[:hq], o_2[hq:])
                if mode == "older":
                    acc_ref[rows, pair_cols[p]] += o_p
                else:
                    acc_ref[rows, pair_cols[p]] = o_p
        return (jnp.min(total) >= SURVIVAL_UNDERFLOW_LOG2).astype(jnp.int32)

    done = lax.cond(q_start + hq >= win, lambda: visit(0, "inner"), lambda: visit(0, "edge"))

    def cond(carry):
        m, done = carry
        return jnp.logical_and(q_start + tq - m * win > 0, done == 0)

    def body(carry):
        m, _ = carry
        return m + 1, visit(m, "older")

    lax.while_loop(cond, body, (jnp.int32(1), done))
    o_ref[...] = acc_ref[...].astype(BF16)


def _attn(q_arr, q_col, k_arr, k_col, v_arr, v_col, *, batch, q_len, k_len, pos0, tq, hq):
    assert tq % hq == 0 and q_len % tq == 0 and pos0 % tq == 0 and ATTN_WINDOW % hq == 0
    assert pos0 + q_len <= k_len and k_len >= ATTN_WINDOW and hq % 16 == 0
    nq = q_len // tq
    stacked_rows = (tq // hq) * SB_HEADS * hq
    return pl.pallas_call(
        functools.partial(_attn_kernel, tq=tq, hq=hq, pos0=pos0),
        grid=(batch, nq),
        in_specs=[
            pl.BlockSpec((tq, D_SB), lambda b, i: (b * nq + i, q_col)),
            pl.BlockSpec((k_len, D_SB), lambda b, i: (b, k_col)),
            pl.BlockSpec((k_len, D_SB), lambda b, i: (b, v_col)),
        ],
        out_specs=pl.BlockSpec((tq, D_SB), lambda b, i: (b * nq + i, 0)),
        out_shape=jax.ShapeDtypeStruct((batch * q_len, D_SB), BF16),
        scratch_shapes=[pltpu.VMEM((tq, D_SB), F32), pltpu.VMEM((stacked_rows, LANES), F32)],
        compiler_params=pltpu.CompilerParams(
            dimension_semantics=("parallel", "arbitrary"), vmem_limit_bytes=VMEM_LIMIT_BYTES),
        name="attn",
    )(q_arr, k_arr, v_arr)


def _merge_kernel(x_ref, u_ref, halo_ref, o_ref, gate_ref, pw_ref, ps_ref, wbp_ref, wbs_ref,
                  wout_ref, y_ref, *, tm, pos0, tiles_per_seq):
    i = pl.program_id(0)
    pos = pos0 + (i % tiles_per_seq) * tm + lax.broadcasted_iota(jnp.int32, (tm, 1), 0)

    run = jnp.concatenate([halo_ref[0], u_ref[...]], axis=0)
    width, totals = 1, {}
    for gi, win in enumerate(POOL_WINDOWS):
        while width < win:
            run = run + pltpu.roll(run, width, 0)
            width *= 2
        assert width == win, "pool windows must be ascending powers of two"
        totals[gi] = run[POOL_HALO:, 0:POOL_GROUP]
        if gi + 1 < len(POOL_WINDOWS):
            run = run[:, POOL_GROUP:]

    mixed = []
    for gi, win in enumerate(POOL_WINDOWS):
        cols = slice(gi * POOL_GROUP, (gi + 1) * POOL_GROUP)
        tot = totals[gi]
        cnt = jnp.minimum(win, pos + 1).astype(F32)
        d = (tot / cnt - u_ref[:, cols]).astype(BF16)
        mixed.append(jnp.dot(d, pw_ref[gi], preferred_element_type=F32) * ps_ref[:, cols])
    a = jnp.concatenate(mixed, axis=1).astype(BF16)

    br_pool = jnp.dot(a, wbp_ref[...], preferred_element_type=F32)
    br_sb = jnp.dot(o_ref[...], wbs_ref[...], preferred_element_type=F32)
    merged = (jax.nn.sigmoid(gate_ref[:, 0:D_MODEL].astype(F32)) * br_pool
              + jax.nn.sigmoid(gate_ref[:, D_MODEL:2 * D_MODEL].astype(F32)) * br_sb)
    y_ref[...] = x_ref[...] + jnp.dot(merged.astype(BF16), wout_ref[...],
                                      preferred_element_type=F32)


def _merge(x, u, halo, o, gate, pool_w, pool_scale, w_bp, w_bs, w_out, *, tm, pos0, seq):
    n = x.shape[0]
    const2 = lambda i: (0, 0)
    return pl.pallas_call(
        functools.partial(_merge_kernel, tm=tm, pos0=pos0, tiles_per_seq=seq // tm),
        grid=(n // tm,),
        in_specs=[
            pl.BlockSpec((tm, D_MODEL), lambda i: (i, 0)),
            pl.BlockSpec((tm, D_POOL), lambda i: (i, 0)),
            pl.BlockSpec((1, POOL_HALO, D_POOL), lambda i: (i, 0, 0)),
            pl.BlockSpec((tm, D_SB), lambda i: (i, 0)),
            pl.BlockSpec((tm, 2 * D_MODEL), lambda i: (i, 0)),
            pl.BlockSpec((len(POOL_WINDOWS), POOL_GROUP, POOL_GROUP), lambda i: (0, 0, 0)),
            pl.BlockSpec((1, D_POOL), const2),
            pl.BlockSpec((D_POOL, D_MODEL), const2),
            pl.BlockSpec((D_SB, D_MODEL), const2),
            pl.BlockSpec((D_MODEL, D_MODEL), const2),
        ],
        out_specs=pl.BlockSpec((tm, D_MODEL), lambda i: (i, 0)),
        out_shape=jax.ShapeDtypeStruct((n, D_MODEL), F32),
        compiler_params=pltpu.CompilerParams(
            dimension_semantics=("parallel",), vmem_limit_bytes=VMEM_LIMIT_BYTES),
        name="merge",
    )(x, u, halo, o, gate, pool_w, pool_scale, w_bp, w_bs, w_out)


def _layer(x, w, *, batch, seq, pos0, k_past, v_past, pool_prefix, tm_ffn, tm_mix, tq, hq):
    assert seq >= POOL_STATE
    w_bf16 = {}

    def ffn(x_in, prefix, final_g=None):
        names = [prefix + "_gate", prefix + "_up", prefix + "_down"]
        res = _ffn(x_in, w[prefix + "_norm"], *[w[nm] for nm in names], final_g,
                   tm=tm_ffn, tf=FFN_CHUNK)
        if isinstance(res, (list, tuple)):
            w_bf16.update(zip(names, res[1:]))
            return res[0]
        return res

    x1 = ffn(x, "ffn1")
    u, qkv, k_hm, v_hm, gate = _proj(x1, w["mix_norm"], w["w_in"], batch=batch, seq=seq, tm=tm_mix)

    if k_past is None:
        o = _attn(qkv, 0, qkv, 1, qkv, 2, batch=batch, q_len=seq, k_len=seq, pos0=0, tq=tq, hq=hq)
    else:
        def with_past(past, col):
            new = qkv[:, col * D_SB:(col + 1) * D_SB].reshape(batch, seq, D_SB)
            return jnp.concatenate([past, new], axis=1).reshape(batch * (pos0 + seq), D_SB)
        o = _attn(qkv, 0, with_past(k_past, 1), 0, with_past(v_past, 2), 0,
                  batch=batch, q_len=seq, k_len=pos0 + seq, pos0=pos0, tq=tq, hq=hq)

    tiles = seq // tm_mix
    u4 = u.reshape(batch, tiles, tm_mix, D_POOL)
    first = jnp.pad(pool_prefix, ((0, 0), (1, 0), (0, 0)))[:, None]
    halo = jnp.concatenate([first, u4[:, :-1, tm_mix - POOL_HALO:, :]], axis=1)
    halo = halo.reshape(batch * tiles, POOL_HALO, D_POOL)

    x2 = _merge(x1, u, halo, o, gate, w["pool_w"], w["pool_scale"], w["w_branch_pool"],
                w["w_branch_sb"], w["w_out"], tm=tm_mix, pos0=pos0, seq=seq)
    y = ffn(x2, "ffn2", w["final_norm"])
    new_pool = u.reshape(batch, seq, D_POOL)[:, seq - POOL_STATE:, :]
    return (y.reshape(batch, seq, D_MODEL), k_hm, v_hm, new_pool), w_bf16


def kernel(x_prompt, x_sample, cache_k, cache_v, state_pool, ffn1_norm, ffn1_gate, ffn1_up, ffn1_down, mix_norm, w_in, pool_w, pool_scale, w_branch_pool, w_branch_sb, w_out, ffn2_norm, ffn2_gate, ffn2_up, ffn2_down, final_norm):
    assert ffn1_norm.shape[0] == 1, "single-layer kernel"
    mixer_mats = dict(w_in=w_in, pool_w=pool_w, w_branch_pool=w_branch_pool,
                      w_branch_sb=w_branch_sb, w_out=w_out)
    w = {name: m[0].astype(BF16) for name, m in mixer_mats.items()}
    w.update(ffn1_gate=ffn1_gate[0], ffn1_up=ffn1_up[0], ffn1_down=ffn1_down[0],
             ffn2_gate=ffn2_gate[0], ffn2_up=ffn2_up[0], ffn2_down=ffn2_down[0])
    w.update(ffn1_norm=ffn1_norm, mix_norm=mix_norm, pool_scale=pool_scale, ffn2_norm=ffn2_norm,
             final_norm=final_norm.reshape(1, D_MODEL))

    b_p, s_p, _ = x_prompt.shape
    b_d, s_d, _ = x_sample.shape
    past = cache_k.shape[3]

    def token_major(cache):
        return cache[0].transpose(0, 2, 1, 3).reshape(b_d, past, D_SB).astype(BF16)

    (y_d, k_d, v_d, pool_d), ffn_bf16 = _layer(
        x_sample.reshape(b_d * s_d, D_MODEL), w, batch=b_d, seq=s_d, pos0=past,
        k_past=token_major(cache_k), v_past=token_major(cache_v), pool_prefix=state_pool[0],
        tm_ffn=b_d * s_d, tm_mix=s_d, tq=s_d, hq=s_d)

    (y_p, k_p, v_p, pool_p), _ = _layer(
        x_prompt.reshape(b_p * s_p, D_MODEL), {**w, **ffn_bf16}, batch=b_p, seq=s_p, pos0=0,
        k_past=None, v_past=None, pool_prefix=jnp.zeros((b_p, POOL_STATE, D_POOL), F32),
        tm_ffn=1024, tm_mix=512, tq=512, hq=64)

    return (y_p, y_d, k_p[None], v_p[None], pool_p[None], k_d[None], v_d[None], pool_d[None])
```

```python
import functools

import jax
import jax.numpy as jnp
from jax import lax
from jax.experimental import pallas as pl
from jax.experimental.pallas import tpu as pltpu

F32 = jnp.float32
BF16 = jnp.bfloat16

D_MODEL = 1024
D_FF = 4 * D_MODEL
D_POOL = D_MODEL // 2
POOL_WINDOWS = (2, 4, 8, 16)
POOL_GROUP = D_POOL // len(POOL_WINDOWS)
POOL_STATE = max(POOL_WINDOWS) - 1
POOL_HALO = POOL_STATE + 1
SB_HEADS = 8
SB_HEAD_DIM = 64
D_SB = SB_HEADS * SB_HEAD_DIM
D_IN = D_POOL + 3 * D_SB + 2 * D_MODEL
RMS_EPS = 1e-6
LANES = 128
VMEM_LIMIT_BYTES = 56 * 1024 * 1024
FFN_CHUNK = 1024
ATTN_WINDOW = 2 * LANES

LOG2E = 1.4426950408889634
SURVIVAL_UNDERFLOW_LOG2 = 150.0
MASKED_LOGIT = -1e30


def _rmsnorm(x, g):
    ms = jnp.mean(x * x, axis=-1, keepdims=True)
    return (x * lax.rsqrt(ms + RMS_EPS)) * g


def _ffn_kernel(*refs, n_ff, post, cast):
    refs = list(refs)
    x_ref, xnext_ref, g_ref, wg_ref, wu_ref, wd_ref = refs[:6]
    del refs[:6]
    pg_ref = refs.pop(0) if post else None
    o_ref = refs.pop(0)
    yn_ref = refs.pop(0) if post == "extra" else None
    if cast:
        wg_out, wu_out, wd_out = refs[:3]
        del refs[:3]
    xn_ref, acc_ref = refs
    i = pl.program_id(0)
    j = pl.program_id(1)
    slot = i % 2

    @pl.when(jnp.logical_and(i == 0, j == 0))
    def _():
        xn_ref[0] = _rmsnorm(x_ref[...], g_ref[...]).astype(BF16)

    def chunk():
        wg, wu, wd = wg_ref[...], wu_ref[...], wd_ref[...]
        if cast:
            wg, wu, wd = wg.astype(BF16), wu.astype(BF16), wd.astype(BF16)
            wg_out[...] = wg
            wu_out[...] = wu
            wd_out[...] = wd
        xn = xn_ref[slot]
        gate = jnp.dot(xn, wg, preferred_element_type=F32)
        up = jnp.dot(xn, wu, preferred_element_type=F32)
        h = (gate * jax.nn.sigmoid(gate) * up).astype(BF16)
        return jnp.dot(h, wd, preferred_element_type=F32)

    def first():
        acc_ref[...] = chunk()

    def middle():
        acc_ref[...] += chunk()

    def last():
        y = x_ref[...] + 0.5 * (acc_ref[...] + chunk())
        if post == "replace":
            y = _rmsnorm(y, pg_ref[...])
        o_ref[...] = y
        if post == "extra":
            yn_ref[...] = _rmsnorm(y, pg_ref[...]).astype(BF16)
        xn_ref[1 - slot] = _rmsnorm(xnext_ref[...], g_ref[...]).astype(BF16)

    case = jnp.where(j == 0, 0, jnp.where(j == n_ff - 1, 2, 1))
    lax.switch(case, [first, middle, last])


def _ffn(x, norm_g, wg, wu, wd, post_g=None, post=None, *, tm, tf):
    n = x.shape[0]
    n_rows = n // tm
    n_ff = D_FF // tf
    assert n_ff >= 2 and post in (None, "replace", "extra") and (post is None) == (post_g is None)
    final = post is not None
    cast = wg.dtype == F32
    assert not cast or n_rows == 1
    row = pl.BlockSpec((1, D_MODEL), lambda i, j: (0, 0))
    w_specs = [
        pl.BlockSpec((D_MODEL, tf), lambda i, j: (0, j)),
        pl.BlockSpec((D_MODEL, tf), lambda i, j: (0, j)),
        pl.BlockSpec((tf, D_MODEL), lambda i, j: (j, 0)),
    ]
    in_specs = [
        pl.BlockSpec((tm, D_MODEL), lambda i, j: (i, 0)),
        pl.BlockSpec((tm, D_MODEL), lambda i, j: (jnp.minimum(i + 1, n_rows - 1), 0)),
        row,
    ] + w_specs
    args = [x, x, norm_g, wg, wu, wd]
    if final:
        in_specs.append(row)
        args.append(post_g)
    tile = pl.BlockSpec((tm, D_MODEL), lambda i, j: (i, 0))
    out_specs = [tile]
    out_shape = [jax.ShapeDtypeStruct((n, D_MODEL), F32)]
    if post == "extra":
        out_specs.append(tile)
        out_shape.append(jax.ShapeDtypeStruct((n, D_MODEL), BF16))
    if cast:
        out_specs += w_specs
        out_shape += [jax.ShapeDtypeStruct(m.shape, BF16) for m in (wg, wu, wd)]
    return pl.pallas_call(
        functools.partial(_ffn_kernel, n_ff=n_ff, post=post, cast=cast),
        grid=(n_rows, n_ff),
        in_specs=in_specs,
        out_specs=out_specs,
        out_shape=out_shape,
        scratch_shapes=[pltpu.VMEM((2, tm, D_MODEL), BF16), pltpu.VMEM((tm, D_MODEL), F32)],
        compiler_params=pltpu.CompilerParams(
            dimension_semantics=("arbitrary", "arbitrary"), vmem_limit_bytes=VMEM_LIMIT_BYTES),
        name={None: "ffn", "replace": "ffn_final", "extra": "ffn_proj"}[post]
        + ("_cast" if cast else ""),
    )(*args)


def _proj_kernel(h_ref, w_ref, u_ref, qkv_ref, k_ref, v_ref, gate_ref):
    h = h_ref[...]

    def mm(c0, c1):
        return jnp.dot(h, w_ref[:, c0:c1], preferred_element_type=F32)

    c = 0
    u_ref[...] = mm(c, c + D_POOL)
    c += D_POOL
    qkv_ref[:, 0:D_SB] = (mm(c, c + D_SB) * (LOG2E * SB_HEAD_DIM ** -0.5)).astype(BF16)
    c += D_SB
    for idx, hm_ref in ((1, k_ref), (2, v_ref)):
        t = mm(c, c + D_SB)
        c += D_SB
        qkv_ref[:, idx * D_SB:(idx + 1) * D_SB] = t.astype(BF16)
        for hd in range(SB_HEADS):
            hm_ref[0, hd, :, :] = t[:, hd * SB_HEAD_DIM:(hd + 1) * SB_HEAD_DIM]
    gate_ref[:, 0:D_MODEL] = mm(c, c + D_MODEL).astype(BF16)
    c += D_MODEL
    gate_ref[:, D_MODEL:2 * D_MODEL] = mm(c, c + D_MODEL).astype(BF16)


def _proj(h, w_in, *, batch, seq, tm):
    n = h.shape[0]
    tiles_per_seq = seq // tm
    hm_spec = pl.BlockSpec((1, SB_HEADS, tm, SB_HEAD_DIM),
                           lambda i: (i // tiles_per_seq, 0, i % tiles_per_seq, 0))
    hm_shape = jax.ShapeDtypeStruct((batch, SB_HEADS, seq, SB_HEAD_DIM), F32)
    return pl.pallas_call(
        _proj_kernel,
        grid=(n // tm,),
        in_specs=[
            pl.BlockSpec((tm, D_MODEL), lambda i: (i, 0)),
            pl.BlockSpec((D_MODEL, D_IN), lambda i: (0, 0)),
        ],
        out_specs=[
            pl.BlockSpec((tm, D_POOL), lambda i: (i, 0)),
            pl.BlockSpec((tm, 3 * D_SB), lambda i: (i, 0)),
            hm_spec,
            hm_spec,
            pl.BlockSpec((tm, 2 * D_MODEL), lambda i: (i, 0)),
        ],
        out_shape=[
            jax.ShapeDtypeStruct((n, D_POOL), F32),
            jax.ShapeDtypeStruct((n, 3 * D_SB), BF16),
            hm_shape,
            hm_shape,
            jax.ShapeDtypeStruct((n, 2 * D_MODEL), BF16),
        ],
        compiler_params=pltpu.CompilerParams(
            dimension_semantics=("parallel",), vmem_limit_bytes=VMEM_LIMIT_BYTES),
        name="proj",
    )(h, w_in)


def _attn_kernel(q_ref, k_ref, v_ref, o_ref, acc_ref, car_ref, *, tq, hq, pos0):
    win = ATTN_WINDOW
    q_start = pos0 + pl.program_id(1) * tq
    subs = range(tq // hq)
    pairs = range(SB_HEADS // 2)
    pair_cols = [slice(p * LANES, (p + 1) * LANES) for p in pairs]

    lane = lax.broadcasted_iota(jnp.int32, (1, LANES), 1)
    first_head = lane < SB_HEAD_DIM
    r = lax.broadcasted_iota(jnp.int32, (win, win), 0)
    c = lax.broadcasted_iota(jnp.int32, (win, win), 1)
    newer = jnp.where(r > c, 1.0, 0.0).astype(BF16)

    def softplus2(z):
        return jnp.maximum(z, 0.0) + jnp.log2(1.0 + jnp.exp2(-jnp.abs(z)))

    def visit(m, mode):
        k_ws, v_ws, valids = [], [], []
        for j in subs:
            row0 = q_start + j * hq
            win_end = row0 + hq - m * win
            q_pos = row0 + lax.broadcasted_iota(jnp.int32, (hq, 1), 0)
            if mode == "inner":
                start = pl.multiple_of(win_end - win, hq)
                k_pos = (win_end - LANES) + lax.broadcasted_iota(jnp.int32, (1, LANES), 1)
                valids.append(k_pos < q_pos)
            else:
                start = pl.multiple_of(jnp.maximum(win_end - win, 0), hq)
                k_pos = start + lax.broadcasted_iota(jnp.int32, (1, win), 1)
                valids.append(k_pos < (q_pos if mode == "edge" else win_end))
            k_ws.append(k_ref[pl.ds(start, win), :])
            v_ws.append(v_ref[pl.ds(start, win), :])

        def mask(x, j):
            if mode == "inner":
                return jnp.concatenate(
                    [x[:, :win - LANES],
                     jnp.where(valids[j], x[:, win - LANES:], MASKED_LOGIT)], axis=1)
            return jnp.where(valids[j], x, MASKED_LOGIT)

        z_parts = []
        for j in subs:
            rows = slice(j * hq, (j + 1) * hq)
            for p in pairs:
                q_p = q_ref[rows, pair_cols[p]]
                zero = jnp.zeros_like(q_p)
                q_2 = jnp.concatenate([jnp.where(first_head, q_p, zero),
                                       jnp.where(first_head, zero, q_p)], axis=0)
                z_2 = lax.dot_general(q_2, k_ws[j][:, pair_cols[p]], (((1,), (1,)), ((), ())),
                                      preferred_element_type=F32)
                z_parts += [mask(z_2[:hq], j), mask(z_2[hq:], j)]
        z = jnp.concatenate(z_parts, axis=0)
        s = softplus2(z)
        sums = jnp.dot(s.astype(BF16), newer, preferred_element_type=F32)
        total = jnp.broadcast_to(jnp.sum(s, axis=1, keepdims=True), (s.shape[0], LANES))
        arg = (z - s) - sums
        if mode == "older":
            car = car_ref[...]
            arg = arg - jnp.concatenate([car] * (win // LANES), axis=1)
            total = car + total
        car_ref[...] = total
        w = jnp.exp2(arg).astype(BF16)

        i = 0
        for j in subs:
            rows = slice(j * hq, (j + 1) * hq)
            for p in pairs:
                o_2 = jnp.dot(w[i * hq:(i + 2) * hq], v_ws[j][:, pair_cols[p]],
                              preferred_element_type=F32)
                i += 2
                o_p = jnp.where(first_head, o_2[:hq], o_2[hq:])
                if mode == "older":
                    acc_ref[rows, pair_cols[p]] += o_p
                else:
                    acc_ref[rows, pair_cols[p]] = o_p
        return (jnp.min(total) >= SURVIVAL_UNDERFLOW_LOG2).astype(jnp.int32)

    done = lax.cond(q_start + hq >= win, lambda: visit(0, "inner"), lambda: visit(0, "edge"))

    def cond(carry):
        m, done = carry
        return jnp.logical_and(q_start + tq - m * win > 0, done == 0)

    def body(carry):
        m, _ = carry
        return m + 1, visit(m, "older")

    lax.while_loop(cond, body, (jnp.int32(1), done))
    o_ref[...] = acc_ref[...].astype(BF16)


def _attn(q_arr, q_col, k_arr, k_col, v_arr, v_col, *, batch, q_len, k_len, pos0, tq, hq):
    assert tq % hq == 0 and q_len % tq == 0 and pos0 % tq == 0 and ATTN_WINDOW % hq == 0
    assert pos0 + q_len <= k_len and k_len >= ATTN_WINDOW and hq % 16 == 0
    nq = q_len // tq
    stacked_rows = (tq // hq) * SB_HEADS * hq
    return pl.pallas_call(
        functools.partial(_attn_kernel, tq=tq, hq=hq, pos0=pos0),
        grid=(batch, nq),
        in_specs=[
            pl.BlockSpec((tq, D_SB), lambda b, i: (b * nq + i, q_col)),
            pl.BlockSpec((k_len, D_SB), lambda b, i: (b, k_col)),
            pl.BlockSpec((k_len, D_SB), lambda b, i: (b, v_col)),
        ],
        out_specs=pl.BlockSpec((tq, D_SB), lambda b, i: (b * nq + i, 0)),
        out_shape=jax.ShapeDtypeStruct((batch * q_len, D_SB), BF16),
        scratch_shapes=[pltpu.VMEM((tq, D_SB), F32), pltpu.VMEM((stacked_rows, LANES), F32)],
        compiler_params=pltpu.CompilerParams(
            dimension_semantics=("parallel", "arbitrary"), vmem_limit_bytes=VMEM_LIMIT_BYTES),
        name="attn",
    )(q_arr, k_arr, v_arr)


def _merge_kernel(x_ref, u_ref, halo_ref, o_ref, gate_ref, pw_ref, ps_ref, wbp_ref, wbs_ref,
                  wout_ref, y_ref, *, tm, pos0, tiles_per_seq):
    i = pl.program_id(0)
    pos = pos0 + (i % tiles_per_seq) * tm + lax.broadcasted_iota(jnp.int32, (tm, 1), 0)

    n_groups = len(POOL_WINDOWS)
    n_chunks = n_groups if tm % (16 * n_groups) == 0 else 1
    quarter = tm // n_chunks
    run = jnp.concatenate([halo_ref[0], u_ref[...]], axis=0)
    width, mixed, gated_sb = 1, [], []
    for gi, win in enumerate(POOL_WINDOWS):
        while width < win:
            run = run + pltpu.roll(run, width, 0)
            width *= 2
        assert width == win, "pool windows must be ascending powers of two"
        cols = slice(gi * POOL_GROUP, (gi + 1) * POOL_GROUP)
        cnt = jnp.minimum(win, pos + 1).astype(F32)
        d = (run[POOL_HALO:, 0:POOL_GROUP] / cnt - u_ref[:, cols]).astype(BF16)
        mixed.append(jnp.dot(d, pw_ref[gi], preferred_element_type=F32) * ps_ref[:, cols])
        if gi + 1 < n_groups:
            run = run[:, POOL_GROUP:]
        if gi < n_chunks:
            rows = slice(gi * quarter, (gi + 1) * quarter)
            br_sb = jnp.dot(o_ref[rows, :], wbs_ref[...], preferred_element_type=F32)
            gated_sb.append(
                jax.nn.sigmoid(gate_ref[rows, D_MODEL:2 * D_MODEL].astype(F32)) * br_sb)
    a = jnp.concatenate(mixed, axis=1).astype(BF16)

    br_pool = jnp.dot(a, wbp_ref[...], preferred_element_type=F32)
    merged = (jax.nn.sigmoid(gate_ref[:, 0:D_MODEL].astype(F32)) * br_pool
              + jnp.concatenate(gated_sb, axis=0))
    y_ref[...] = x_ref[...] + jnp.dot(merged.astype(BF16), wout_ref[...],
                                      preferred_element_type=F32)


def _merge(x, u, halo, o, gate, pool_w, pool_scale, w_bp, w_bs, w_out, *, tm, pos0, seq):
    n = x.shape[0]
    const2 = lambda i: (0, 0)
    return pl.pallas_call(
        functools.partial(_merge_kernel, tm=tm, pos0=pos0, tiles_per_seq=seq // tm),
        grid=(n // tm,),
        in_specs=[
            pl.BlockSpec((tm, D_MODEL), lambda i: (i, 0)),
            pl.BlockSpec((tm, D_POOL), lambda i: (i, 0)),
            pl.BlockSpec((1, POOL_HALO, D_POOL), lambda i: (i, 0, 0)),
            pl.BlockSpec((tm, D_SB), lambda i: (i, 0)),
            pl.BlockSpec((tm, 2 * D_MODEL), lambda i: (i, 0)),
            pl.BlockSpec((len(POOL_WINDOWS), POOL_GROUP, POOL_GROUP), lambda i: (0, 0, 0)),
            pl.BlockSpec((1, D_POOL), const2),
            pl.BlockSpec((D_POOL, D_MODEL), const2),
            pl.BlockSpec((D_SB, D_MODEL), const2),
            pl.BlockSpec((D_MODEL, D_MODEL), const2),
        ],
        out_specs=pl.BlockSpec((tm, D_MODEL), lambda i: (i, 0)),
        out_shape=jax.ShapeDtypeStruct((n, D_MODEL), F32),
        compiler_params=pltpu.CompilerParams(
            dimension_semantics=("parallel",), vmem_limit_bytes=VMEM_LIMIT_BYTES),
        name="merge",
    )(x, u, halo, o, gate, pool_w, pool_scale, w_bp, w_bs, w_out)


def _layer(x, w, *, batch, seq, pos0, k_past, v_past, pool_prefix, tm_ffn, tm_mix, tq, hq):
    assert seq >= POOL_STATE
    w_bf16 = {}

    def ffn(x_in, prefix, post_g, post):
        names = [prefix + "_gate", prefix + "_up", prefix + "_down"]
        res = _ffn(x_in, w[prefix + "_norm"], *[w[nm] for nm in names], post_g, post,
                   tm=tm_ffn, tf=FFN_CHUNK)
        n_main = 2 if post == "extra" else 1
        w_bf16.update(zip(names, res[n_main:]))
        return res[:n_main]

    x1, h = ffn(x, "ffn1", w["mix_norm"], "extra")
    u, qkv, k_hm, v_hm, gate = _proj(h, w["w_in"], batch=batch, seq=seq, tm=tm_mix)

    if k_past is None:
        o = _attn(qkv, 0, qkv, 1, qkv, 2, batch=batch, q_len=seq, k_len=seq, pos0=0, tq=tq, hq=hq)
    else:
        def with_past(past, col):
            new = qkv[:, col * D_SB:(col + 1) * D_SB].reshape(batch, seq, D_SB)
            return jnp.concatenate([past, new], axis=1).reshape(batch * (pos0 + seq), D_SB)
        o = _attn(qkv, 0, with_past(k_past, 1), 0, with_past(v_past, 2), 0,
                  batch=batch, q_len=seq, k_len=pos0 + seq, pos0=pos0, tq=tq, hq=hq)

    tiles = seq // tm_mix
    u4 = u.reshape(batch, tiles, tm_mix, D_POOL)
    first = jnp.pad(pool_prefix, ((0, 0), (1, 0), (0, 0)))[:, None]
    halo = jnp.concatenate([first, u4[:, :-1, tm_mix - POOL_HALO:, :]], axis=1)
    halo = halo.reshape(batch * tiles, POOL_HALO, D_POOL)

    x2 = _merge(x1, u, halo, o, gate, w["pool_w"], w["pool_scale"], w["w_branch_pool"],
                w["w_branch_sb"], w["w_out"], tm=tm_mix, pos0=pos0, seq=seq)
    (y,) = ffn(x2, "ffn2", w["final_norm"], "replace")
    new_pool = u.reshape(batch, seq, D_POOL)[:, seq - POOL_STATE:, :]
    return (y.reshape(batch, seq, D_MODEL), k_hm, v_hm, new_pool), w_bf16


def kernel(x_prompt, x_sample, cache_k, cache_v, state_pool, ffn1_norm, ffn1_gate, ffn1_up, ffn1_down, mix_norm, w_in, pool_w, pool_scale, w_branch_pool, w_branch_sb, w_out, ffn2_norm, ffn2_gate, ffn2_up, ffn2_down, final_norm):
    assert ffn1_norm.shape[0] == 1, "single-layer kernel"
    mixer_mats = dict(w_in=w_in, pool_w=pool_w, w_branch_pool=w_branch_pool,
                      w_branch_sb=w_branch_sb, w_out=w_out)
    w = {name: m[0].astype(BF16) for name, m in mixer_mats.items()}
    w.update(ffn1_gate=ffn1_gate[0], ffn1_up=ffn1_up[0], ffn1_down=ffn1_down[0],
             ffn2_gate=ffn2_gate[0], ffn2_up=ffn2_up[0], ffn2_down=ffn2_down[0])
    w.update(ffn1_norm=ffn1_norm, mix_norm=mix_norm, pool_scale=pool_scale, ffn2_norm=ffn2_norm,
             final_norm=final_norm.reshape(1, D_MODEL))

    b_p, s_p, _ = x_prompt.shape
    b_d, s_d, _ = x_sample.shape
    past = cache_k.shape[3]

    def token_major(cache):
        return cache[0].transpose(0, 2, 1, 3).reshape(b_d, past, D_SB).astype(BF16)

    (y_d, k_d, v_d, pool_d), ffn_bf16 = _layer(
        x_sample.reshape(b_d * s_d, D_MODEL), w, batch=b_d, seq=s_d, pos0=past,
        k_past=token_major(cache_k), v_past=token_major(cache_v), pool_prefix=state_pool[0],
        tm_ffn=b_d * s_d, tm_mix=s_d, tq=s_d, hq=s_d)

    (y_p, k_p, v_p, pool_p), _ = _layer(
        x_prompt.reshape(b_p * s_p, D_MODEL), {**w, **ffn_bf16}, batch=b_p, seq=s_p, pos0=0,
        k_past=None, v_past=None, pool_prefix=jnp.zeros((b_p, POOL_STATE, D_POOL), F32),
        tm_ffn=1024, tm_mix=512, tq=256, hq=64)

    return (y_p, y_d, k_p[None], v_p[None], pool_p[None], k_d[None], v_d[None], pool_d[None])
```

```python
import functools

import jax
import jax.numpy as jnp
from jax import lax
from jax.experimental import pallas as pl
from jax.experimental.pallas import tpu as pltpu

F32 = jnp.float32
BF16 = jnp.bfloat16

D_MODEL = 1024
D_FF = 4 * D_MODEL
D_POOL = D_MODEL // 2
POOL_WINDOWS = (2, 4, 8, 16)
POOL_GROUP = D_POOL // len(POOL_WINDOWS)
POOL_STATE = max(POOL_WINDOWS) - 1
POOL_HALO = POOL_STATE + 1
SB_HEADS = 8
SB_HEAD_DIM = 64
D_SB = SB_HEADS * SB_HEAD_DIM
D_IN = D_POOL + 3 * D_SB + 2 * D_MODEL
RMS_EPS = 1e-6
LANES = 128
VMEM_LIMIT_BYTES = 56 * 1024 * 1024
FFN_CHUNK = 1024
FFN_EPILOGUE_PIECES = 2
ATTN_WINDOW = 2 * LANES

LOG2E = 1.4426950408889634
SURVIVAL_UNDERFLOW_LOG2 = 150.0
MASKED_LOGIT = -1e30


def _rmsnorm(x, g):
    ms = jnp.mean(x * x, axis=-1, keepdims=True)
    return (x * lax.rsqrt(ms + RMS_EPS)) * g


def _ffn_kernel(*refs, n_ff, post, cast):
    refs = list(refs)
    x_ref, xnext_ref, g_ref, wg_ref, wu_ref, wd_ref = refs[:6]
    del refs[:6]
    pg_ref = refs.pop(0) if post else None
    o_ref = refs.pop(0)
    yn_ref = refs.pop(0) if post == "extra" else None
    if cast:
        wg_out, wu_out, wd_out = refs[:3]
        del refs[:3]
    xn_ref, acc_ref = refs
    i = pl.program_id(0)
    j = pl.program_id(1)
    slot = i % 2

    @pl.when(jnp.logical_and(i == 0, j == 0))
    def _():
        xn_ref[0] = _rmsnorm(x_ref[...], g_ref[...]).astype(BF16)

    def hidden():
        wg, wu, wd = wg_ref[...], wu_ref[...], wd_ref[...]
        if cast:
            wg, wu, wd = wg.astype(BF16), wu.astype(BF16), wd.astype(BF16)
            wg_out[...] = wg
            wu_out[...] = wu
            wd_out[...] = wd
        xn = xn_ref[slot]
        gate = jnp.dot(xn, wg, preferred_element_type=F32)
        up = jnp.dot(xn, wu, preferred_element_type=F32)
        return (gate * jax.nn.sigmoid(gate) * up).astype(BF16), wd

    def first():
        h, wd = hidden()
        acc_ref[...] = jnp.dot(h, wd, preferred_element_type=F32)

    def middle():
        h, wd = hidden()
        acc_ref[...] += jnp.dot(h, wd, preferred_element_type=F32)

    def last():
        h, wd = hidden()
        piece = x_ref.shape[0] // FFN_EPILOGUE_PIECES
        for r in range(FFN_EPILOGUE_PIECES):
            rows = slice(r * piece, (r + 1) * piece)
            part = jnp.dot(h[rows], wd, preferred_element_type=F32)
            y = x_ref[rows, :] + 0.5 * (acc_ref[rows, :] + part)
            if post == "replace":
                y = _rmsnorm(y, pg_ref[...])
            o_ref[rows, :] = y
            if post == "extra":
                yn_ref[rows, :] = _rmsnorm(y, pg_ref[...]).astype(BF16)
        xn_ref[1 - slot] = _rmsnorm(xnext_ref[...], g_ref[...]).astype(BF16)

    case = jnp.where(j == 0, 0, jnp.where(j == n_ff - 1, 2, 1))
    lax.switch(case, [first, middle, last])


def _ffn(x, norm_g, wg, wu, wd, post_g=None, post=None, *, tm, tf):
    n = x.shape[0]
    n_rows = n // tm
    n_ff = D_FF // tf
    assert n_ff >= 2 and post in (None, "replace", "extra") and (post is None) == (post_g is None)
    final = post is not None
    cast = wg.dtype == F32
    assert not cast or n_rows == 1
    row = pl.BlockSpec((1, D_MODEL), lambda i, j: (0, 0))
    w_specs = [
        pl.BlockSpec((D_MODEL, tf), lambda i, j: (0, j)),
        pl.BlockSpec((D_MODEL, tf), lambda i, j: (0, j)),
        pl.BlockSpec((tf, D_MODEL), lambda i, j: (j, 0)),
    ]
    in_specs = [
        pl.BlockSpec((tm, D_MODEL), lambda i, j: (i, 0)),
        pl.BlockSpec((tm, D_MODEL), lambda i, j: (jnp.minimum(i + 1, n_rows - 1), 0)),
        row,
    ] + w_specs
    args = [x, x, norm_g, wg, wu, wd]
    if final:
        in_specs.append(row)
        args.append(post_g)
    tile = pl.BlockSpec((tm, D_MODEL), lambda i, j: (i, 0))
    out_specs = [tile]
    out_shape = [jax.ShapeDtypeStruct((n, D_MODEL), F32)]
    if post == "extra":
        out_specs.append(tile)
        out_shape.append(jax.ShapeDtypeStruct((n, D_MODEL), BF16))
    if cast:
        out_specs += w_specs
        out_shape += [jax.ShapeDtypeStruct(m.shape, BF16) for m in (wg, wu, wd)]
    return pl.pallas_call(
        functools.partial(_ffn_kernel, n_ff=n_ff, post=post, cast=cast),
        grid=(n_rows, n_ff),
        in_specs=in_specs,
        out_specs=out_specs,
        out_shape=out_shape,
        scratch_shapes=[pltpu.VMEM((2, tm, D_MODEL), BF16), pltpu.VMEM((tm, D_MODEL), F32)],
        compiler_params=pltpu.CompilerParams(
            dimension_semantics=("arbitrary", "arbitrary"), vmem_limit_bytes=VMEM_LIMIT_BYTES),
        name={None: "ffn", "replace": "ffn_final", "extra": "ffn_proj"}[post]
        + ("_cast" if cast else ""),
    )(*args)


def _proj_kernel(h_ref, w_ref, u_ref, qkv_ref, k_ref, v_ref, gate_ref):
    h = h_ref[...]

    def mm(c0, c1):
        return jnp.dot(h, w_ref[:, c0:c1], preferred_element_type=F32)

    c = 0
    u_ref[...] = mm(c, c + D_POOL)
    c += D_POOL
    qkv_ref[:, 0:D_SB] = (mm(c, c + D_SB) * (LOG2E * SB_HEAD_DIM ** -0.5)).astype(BF16)
    c += D_SB
    for idx, hm_ref in ((1, k_ref), (2, v_ref)):
        t = mm(c, c + D_SB)
        c += D_SB
        qkv_ref[:, idx * D_SB:(idx + 1) * D_SB] = t.astype(BF16)
        for hd in range(SB_HEADS):
            hm_ref[0, hd, :, :] = t[:, hd * SB_HEAD_DIM:(hd + 1) * SB_HEAD_DIM]
    gate_ref[:, 0:D_MODEL] = mm(c, c + D_MODEL).astype(BF16)
    c += D_MODEL
    gate_ref[:, D_MODEL:2 * D_MODEL] = mm(c, c + D_MODEL).astype(BF16)


def _proj(h, w_in, *, batch, seq, tm):
    n = h.shape[0]
    tiles_per_seq = seq // tm
    hm_spec = pl.BlockSpec((1, SB_HEADS, tm, SB_HEAD_DIM),
                           lambda i: (i // tiles_per_seq, 0, i % tiles_per_seq, 0))
    hm_shape = jax.ShapeDtypeStruct((batch, SB_HEADS, seq, SB_HEAD_DIM), F32)
    return pl.pallas_call(
        _proj_kernel,
        grid=(n // tm,),
        in_specs=[
            pl.BlockSpec((tm, D_MODEL), lambda i: (i, 0)),
            pl.BlockSpec((D_MODEL, D_IN), lambda i: (0, 0)),
        ],
        out_specs=[
            pl.BlockSpec((tm, D_POOL), lambda i: (i, 0)),
            pl.BlockSpec((tm, 3 * D_SB), lambda i: (i, 0)),
            hm_spec,
            hm_spec,
            pl.BlockSpec((tm, 2 * D_MODEL), lambda i: (i, 0)),
        ],
        out_shape=[
            jax.ShapeDtypeStruct((n, D_POOL), F32),
            jax.ShapeDtypeStruct((n, 3 * D_SB), BF16),
            hm_shape,
            hm_shape,
            jax.ShapeDtypeStruct((n, 2 * D_MODEL), BF16),
        ],
        compiler_params=pltpu.CompilerParams(
            dimension_semantics=("parallel",), vmem_limit_bytes=VMEM_LIMIT_BYTES),
        name="proj",
    )(h, w_in)


def _attn_kernel(q_ref, k_ref, v_ref, o_ref, acc_ref, car_ref, *, tq, hq, pos0):
    win = ATTN_WINDOW
    q_start = pos0 + pl.program_id(1) * tq
    subs = range(tq // hq)
    pairs = range(SB_HEADS // 2)
    pair_cols = [slice(p * LANES, (p + 1) * LANES) for p in pairs]

    lane = lax.broadcasted_iota(jnp.int32, (1, LANES), 1)
    first_head = lane < SB_HEAD_DIM
    r = lax.broadcasted_iota(jnp.int32, (win, win), 0)
    c = lax.broadcasted_iota(jnp.int32, (win, win), 1)
    newer = jnp.where(r > c, 1.0, 0.0).astype(BF16)

    def softplus2(z):
        return jnp.maximum(z, 0.0) + jnp.log2(1.0 + jnp.exp2(-jnp.abs(z)))

    def visit(m, mode):
        k_ws, v_ws, valids = [], [], []
        for j in subs:
            row0 = q_start + j * hq
            win_end = row0 + hq - m * win
            q_pos = row0 + lax.broadcasted_iota(jnp.int32, (hq, 1), 0)
            if mode == "inner":
                start = pl.multiple_of(win_end - win, hq)
                k_pos = (win_end - LANES) + lax.broadcasted_iota(jnp.int32, (1, LANES), 1)
                valids.append(k_pos < q_pos)
            else:
                start = pl.multiple_of(jnp.maximum(win_end - win, 0), hq)
                k_pos = start + lax.broadcasted_iota(jnp.int32, (1, win), 1)
                valids.append(k_pos < (q_pos if mode == "edge" else win_end))
            k_ws.append(k_ref[pl.ds(start, win), :])
            v_ws.append(v_ref[pl.ds(start, win), :])

        def mask(x, j):
            if mode == "inner":
                return jnp.concatenate(
                    [x[:, :win - LANES],
                     jnp.where(valids[j], x[:, win - LANES:], MASKED_LOGIT)], axis=1)
            return jnp.where(valids[j], x, MASKED_LOGIT)

        z_parts = []
        for j in subs:
            rows = slice(j * hq, (j + 1) * hq)
            for p in pairs:
                q_p = q_ref[rows, pair_cols[p]]
                zero = jnp.zeros_like(q_p)
                q_2 = jnp.concatenate([jnp.where(first_head, q_p, zero),
                                       jnp.where(first_head, zero, q_p)], axis=0)
                z_2 = lax.dot_general(q_2, k_ws[j][:, pair_cols[p]], (((1,), (1,)), ((), ())),
                                      preferred_element_type=F32)
                z_parts += [mask(z_2[:hq], j), mask(z_2[hq:], j)]
        z = jnp.concatenate(z_parts, axis=0)
        s = softplus2(z)
        log2_beta = z - s
        sums = jnp.dot(s.astype(BF16), newer, preferred_element_type=F32)
        total = jnp.broadcast_to(jnp.sum(s, axis=1, keepdims=True), (s.shape[0], LANES))
        arg = log2_beta - sums
        if mode == "older":
            car = car_ref[...]
            arg = arg - jnp.concatenate([car] * (win // LANES), axis=1)
            total = car + total
        car_ref[...] = total
        w = jnp.exp2(arg).astype(BF16)

        i = 0
        for j in subs:
            rows = slice(j * hq, (j + 1) * hq)
            for p in pairs:
                o_2 = jnp.dot(w[i * hq:(i + 2) * hq], v_ws[j][:, pair_cols[p]],
                              preferred_element_type=F32)
                i += 2
                o_p = jnp.where(first_head, o_2[:hq], o_2[hq:])
                if mode == "older":
                    acc_ref[rows, pair_cols[p]] += o_p
                else:
                    acc_ref[rows, pair_cols[p]] = o_p
        return (jnp.min(total) >= SURVIVAL_UNDERFLOW_LOG2).astype(jnp.int32)

    done = lax.cond(q_start + hq >= win, lambda: visit(0, "inner"), lambda: visit(0, "edge"))

    def cond(carry):
        m, done = carry
        return jnp.logical_and(q_start + tq - m * win > 0, done == 0)

    def body(carry):
        m, _ = carry
        return m + 1, visit(m, "older")

    lax.while_loop(cond, body, (jnp.int32(1), done))
    o_ref[...] = acc_ref[...].astype(BF16)


def _attn(q_arr, q_col, k_arr, k_col, v_arr, v_col, *, batch, q_len, k_len, pos0, tq, hq):
    assert tq % hq == 0 and q_len % tq == 0 and pos0 % tq == 0 and ATTN_WINDOW % hq == 0
    assert pos0 + q_len <= k_len and k_len >= ATTN_WINDOW and hq % 16 == 0
    nq = q_len // tq
    stacked_rows = (tq // hq) * SB_HEADS * hq
    return pl.pallas_call(
        functools.partial(_attn_kernel, tq=tq, hq=hq, pos0=pos0),
        grid=(batch, nq),
        in_specs=[
            pl.BlockSpec((tq, D_SB), lambda b, i: (b * nq + i, q_col)),
            pl.BlockSpec((k_len, D_SB), lambda b, i: (b, k_col)),
            pl.BlockSpec((k_len, D_SB), lambda b, i: (b, v_col)),
        ],
        out_specs=pl.BlockSpec((tq, D_SB), lambda b, i: (b * nq + i, 0)),
        out_shape=jax.ShapeDtypeStruct((batch * q_len, D_SB), BF16),
        scratch_shapes=[pltpu.VMEM((tq, D_SB), F32), pltpu.VMEM((stacked_rows, LANES), F32)],
        compiler_params=pltpu.CompilerParams(
            dimension_semantics=("parallel", "arbitrary"), vmem_limit_bytes=VMEM_LIMIT_BYTES),
        name="attn",
    )(q_arr, k_arr, v_arr)


def _merge_kernel(x_ref, u_ref, halo_ref, o_ref, gate_ref, pw_ref, ps_ref, wbp_ref, wbs_ref,
                  wout_ref, y_ref, *, tm, pos0, tiles_per_seq):
    i = pl.program_id(0)
    pos = pos0 + (i % tiles_per_seq) * tm + lax.broadcasted_iota(jnp.int32, (tm, 1), 0)

    n_groups = len(POOL_WINDOWS)
    n_chunks = n_groups if tm % (16 * n_groups) == 0 else 1
    quarter = tm // n_chunks
    run = jnp.concatenate([halo_ref[0], u_ref[...]], axis=0)
    width, mixed, gated_sb = 1, [], []
    for gi, win in enumerate(POOL_WINDOWS):
        while width < win:
            run = run + pltpu.roll(run, width, 0)
            width *= 2
        assert width == win, "pool windows must be ascending powers of two"
        cols = slice(gi * POOL_GROUP, (gi + 1) * POOL_GROUP)
        cnt = jnp.minimum(win, pos + 1).astype(F32)
        d = (run[POOL_HALO:, 0:POOL_GROUP] / cnt - u_ref[:, cols]).astype(BF16)
        mixed.append(jnp.dot(d, pw_ref[gi], preferred_element_type=F32) * ps_ref[:, cols])
        if gi + 1 < n_groups:
            run = run[:, POOL_GROUP:]
        if gi < n_chunks:
            rows = slice(gi * quarter, (gi + 1) * quarter)
            br_sb = jnp.dot(o_ref[rows, :], wbs_ref[...], preferred_element_type=F32)
            gated_sb.append(
                jax.nn.sigmoid(gate_ref[rows, D_MODEL:2 * D_MODEL].astype(F32)) * br_sb)
    a = jnp.concatenate(mixed, axis=1).astype(BF16)

    br_pool = jnp.dot(a, wbp_ref[...], preferred_element_type=F32)
    merged = (jax.nn.sigmoid(gate_ref[:, 0:D_MODEL].astype(F32)) * br_pool
              + jnp.concatenate(gated_sb, axis=0))
    y_ref[...] = x_ref[...] + jnp.dot(merged.astype(BF16), wout_ref[...],
                                      preferred_element_type=F32)


def _merge(x, u, halo, o, gate, pool_w, pool_scale, w_bp, w_bs, w_out, *, tm, pos0, seq):
    n = x.shape[0]
    const2 = lambda i: (0, 0)
    return pl.pallas_call(
        functools.partial(_merge_kernel, tm=tm, pos0=pos0, tiles_per_seq=seq // tm),
        grid=(n // tm,),
        in_specs=[
            pl.BlockSpec((tm, D_MODEL), lambda i: (i, 0)),
            pl.BlockSpec((tm, D_POOL), lambda i: (i, 0)),
            pl.BlockSpec((1, POOL_HALO, D_POOL), lambda i: (i, 0, 0)),
            pl.BlockSpec((tm, D_SB), lambda i: (i, 0)),
            pl.BlockSpec((tm, 2 * D_MODEL), lambda i: (i, 0)),
            pl.BlockSpec((len(POOL_WINDOWS), POOL_GROUP, POOL_GROUP), lambda i: (0, 0, 0)),
            pl.BlockSpec((1, D_POOL), const2),
            pl.BlockSpec((D_POOL, D_MODEL), const2),
            pl.BlockSpec((D_SB, D_MODEL), const2),
            pl.BlockSpec((D_MODEL, D_MODEL), const2),
        ],
        out_specs=pl.BlockSpec((tm, D_MODEL), lambda i: (i, 0)),
        out_shape=jax.ShapeDtypeStruct((n, D_MODEL), F32),
        compiler_params=pltpu.CompilerParams(
            dimension_semantics=("parallel",), vmem_limit_bytes=VMEM_LIMIT_BYTES),
        name="merge",
    )(x, u, halo, o, gate, pool_w, pool_scale, w_bp, w_bs, w_out)


def _layer(x, w, *, batch, seq, pos0, k_past, v_past, pool_prefix, tm_ffn, tm_mix, tq, hq):
    assert seq >= POOL_STATE
    w_bf16 = {}

    def ffn(x_in, prefix, post_g, post):
        names = [prefix + "_gate", prefix + "_up", prefix + "_down"]
        res = _ffn(x_in, w[prefix + "_norm"], *[w[nm] for nm in names], post_g, post,
                   tm=tm_ffn, tf=FFN_CHUNK)
        n_main = 2 if post == "extra" else 1
        w_bf16.update(zip(names, res[n_main:]))
        return res[:n_main]

    x1, h = ffn(x, "ffn1", w["mix_norm"], "extra")
    u, qkv, k_hm, v_hm, gate = _proj(h, w["w_in"], batch=batch, seq=seq, tm=tm_mix)

    if k_past is None:
        o = _attn(qkv, 0, qkv, 1, qkv, 2, batch=batch, q_len=seq, k_len=seq, pos0=0, tq=tq, hq=hq)
    else:
        def with_past(past, new):
            full = jnp.concatenate([past, new], axis=2).transpose(0, 2, 1, 3)
            return full.reshape(batch * (pos0 + seq), D_SB).astype(BF16)
        o = _attn(qkv, 0, with_past(k_past, k_hm), 0, with_past(v_past, v_hm), 0,
                  batch=batch, q_len=seq, k_len=pos0 + seq, pos0=pos0, tq=tq, hq=hq)

    tiles = seq // tm_mix
    u4 = u.reshape(batch, tiles, tm_mix, D_POOL)
    first = jnp.pad(pool_prefix, ((0, 0), (1, 0), (0, 0)))[:, None]
    halo = jnp.concatenate([first, u4[:, :-1, tm_mix - POOL_HALO:, :]], axis=1)
    halo = halo.reshape(batch * tiles, POOL_HALO, D_POOL)

    x2 = _merge(x1, u, halo, o, gate, w["pool_w"], w["pool_scale"], w["w_branch_pool"],
                w["w_branch_sb"], w["w_out"], tm=tm_mix, pos0=pos0, seq=seq)
    (y,) = ffn(x2, "ffn2", w["final_norm"], "replace")
    new_pool = u.reshape(batch, seq, D_POOL)[:, seq - POOL_STATE:, :]
    return (y.reshape(batch, seq, D_MODEL), k_hm, v_hm, new_pool), w_bf16


def kernel(x_prompt, x_sample, cache_k, cache_v, state_pool, ffn1_norm, ffn1_gate, ffn1_up, ffn1_down, mix_norm, w_in, pool_w, pool_scale, w_branch_pool, w_branch_sb, w_out, ffn2_norm, ffn2_gate, ffn2_up, ffn2_down, final_norm):
    assert ffn1_norm.shape[0] == 1, "single-layer kernel"
    mixer_mats = dict(w_in=w_in, pool_w=pool_w, w_branch_pool=w_branch_pool,
                      w_branch_sb=w_branch_sb, w_out=w_out)
    w = {name: m[0].astype(BF16) for name, m in mixer_mats.items()}
    w.update(ffn1_gate=ffn1_gate[0], ffn1_up=ffn1_up[0], ffn1_down=ffn1_down[0],
             ffn2_gate=ffn2_gate[0], ffn2_up=ffn2_up[0], ffn2_down=ffn2_down[0])
    w.update(ffn1_norm=ffn1_norm, mix_norm=mix_norm, pool_scale=pool_scale, ffn2_norm=ffn2_norm,
             final_norm=final_norm.reshape(1, D_MODEL))

    b_p, s_p, _ = x_prompt.shape
    b_d, s_d, _ = x_sample.shape
    past = cache_k.shape[3]

    (y_d, k_d, v_d, pool_d), ffn_bf16 = _layer(
        x_sample.reshape(b_d * s_d, D_MODEL), w, batch=b_d, seq=s_d, pos0=past,
        k_past=cache_k[0], v_past=cache_v[0], pool_prefix=state_pool[0],
        tm_ffn=b_d * s_d, tm_mix=s_d, tq=s_d, hq=s_d)

    (y_p, k_p, v_p, pool_p), _ = _layer(
        x_prompt.reshape(b_p * s_p, D_MODEL), {**w, **ffn_bf16}, batch=b_p, seq=s_p, pos0=0,
        k_past=None, v_past=None, pool_prefix=jnp.zeros((b_p, POOL_STATE, D_POOL), F32),
        tm_ffn=1024, tm_mix=512, tq=512, hq=64)

    return (y_p, y_d, k_p[None], v_p[None], pool_p[None], k_d[None], v_d[None], pool_d[None])
```

```python
import functools

import jax
import jax.numpy as jnp
from jax import lax
from jax.experimental import pallas as pl
from jax.experimental.pallas import tpu as pltpu

F32 = jnp.float32
BF16 = jnp.bfloat16

D_MODEL = 1024
D_FF = 4 * D_MODEL
D_POOL = D_MODEL // 2
POOL_WINDOWS = (2, 4, 8, 16)
POOL_GROUP = D_POOL // len(POOL_WINDOWS)
POOL_STATE = max(POOL_WINDOWS) - 1
POOL_HALO = POOL_STATE + 1
SB_HEADS = 8
SB_HEAD_DIM = 64
D_SB = SB_HEADS * SB_HEAD_DIM
D_IN = D_POOL + 3 * D_SB + 2 * D_MODEL
RMS_EPS = 1e-6
LANES = 128
VMEM_LIMIT_BYTES = 56 * 1024 * 1024
FFN_CHUNK = 1024
ATTN_WINDOW = 2 * LANES

LOG2E = 1.4426950408889634
SURVIVAL_UNDERFLOW_LOG2 = 150.0
MASKED_LOGIT = -1e30


def _rmsnorm(x, g):
    ms = jnp.mean(x * x, axis=-1, keepdims=True)
    return (x * lax.rsqrt(ms + RMS_EPS)) * g


def _ffn_kernel(*refs, n_ff, final, cast):
    refs = list(refs)
    x_ref, xnext_ref, g_ref, wg_ref, wu_ref, wd_ref = refs[:6]
    del refs[:6]
    fg_ref = refs.pop(0) if final else None
    o_ref = refs.pop(0)
    if cast:
        wg_out, wu_out, wd_out = refs[:3]
        del refs[:3]
    xn_ref, acc_ref = refs
    i = pl.program_id(0)
    j = pl.program_id(1)
    slot = i % 2

    @pl.when(jnp.logical_and(i == 0, j == 0))
    def _():
        xn_ref[0] = _rmsnorm(x_ref[...], g_ref[...]).astype(BF16)

    def chunk():
        wg, wu, wd = wg_ref[...], wu_ref[...], wd_ref[...]
        if cast:
            wg, wu, wd = wg.astype(BF16), wu.astype(BF16), wd.astype(BF16)
            wg_out[...] = wg
            wu_out[...] = wu
            wd_out[...] = wd
        xn = xn_ref[slot]
        gate = jnp.dot(xn, wg, preferred_element_type=F32)
        up = jnp.dot(xn, wu, preferred_element_type=F32)
        h = (gate * jax.nn.sigmoid(gate) * up).astype(BF16)
        return jnp.dot(h, wd, preferred_element_type=F32)

    def prepare_next(rows):
        xn_ref[1 - slot, rows, :] = _rmsnorm(xnext_ref[rows, :], g_ref[...]).astype(BF16)

    piece = x_ref.shape[0] // n_ff

    def first():
        acc_ref[...] = chunk()
        prepare_next(pl.ds(0, piece))

    def middle():
        acc_ref[...] += chunk()
        prepare_next(pl.ds(pl.multiple_of(j * piece, piece), piece))

    def last():
        y = x_ref[...] + 0.5 * (acc_ref[...] + chunk())
        if final:
            y = _rmsnorm(y, fg_ref[...])
        o_ref[...] = y
        prepare_next(pl.ds((n_ff - 1) * piece, piece))

    case = jnp.where(j == 0, 0, jnp.where(j == n_ff - 1, 2, 1))
    lax.switch(case, [first, middle, last])


def _ffn(x, norm_g, wg, wu, wd, final_g=None, *, tm, tf):
    n = x.shape[0]
    n_rows = n // tm
    n_ff = D_FF // tf
    assert n_ff >= 2 and tm % (16 * n_ff) == 0
    final = final_g is not None
    cast = wg.dtype == F32
    assert not cast or n_rows == 1
    row = pl.BlockSpec((1, D_MODEL), lambda i, j: (0, 0))
    w_specs = [
        pl.BlockSpec((D_MODEL, tf), lambda i, j: (0, j)),
        pl.BlockSpec((D_MODEL, tf), lambda i, j: (0, j)),
        pl.BlockSpec((tf, D_MODEL), lambda i, j: (j, 0)),
    ]
    in_specs = [
        pl.BlockSpec((tm, D_MODEL), lambda i, j: (i, 0)),
        pl.BlockSpec((tm, D_MODEL), lambda i, j: (jnp.minimum(i + 1, n_rows - 1), 0)),
        row,
    ] + w_specs
    args = [x, x, norm_g, wg, wu, wd]
    if final:
        in_specs.append(row)
        args.append(final_g)
    out_specs = [pl.BlockSpec((tm, D_MODEL), lambda i, j: (i, 0))]
    out_shape = [jax.ShapeDtypeStruct((n, D_MODEL), F32)]
    if cast:
        out_specs += w_specs
        out_shape += [jax.ShapeDtypeStruct(m.shape, BF16) for m in (wg, wu, wd)]
    outs = pl.pallas_call(
        functools.partial(_ffn_kernel, n_ff=n_ff, final=final, cast=cast),
        grid=(n_rows, n_ff),
        in_specs=in_specs,
        out_specs=out_specs,
        out_shape=out_shape,
        scratch_shapes=[pltpu.VMEM((2, tm, D_MODEL), BF16), pltpu.VMEM((tm, D_MODEL), F32)],
        compiler_params=pltpu.CompilerParams(
            dimension_semantics=("arbitrary", "arbitrary"), vmem_limit_bytes=VMEM_LIMIT_BYTES),
        name=("ffn_final" if final else "ffn") + ("_cast" if cast else ""),
    )(*args)
    return outs if cast else outs[0]


def _proj_kernel(x_ref, g_ref, w_ref, u_ref, qkv_ref, k_ref, v_ref, gate_ref):
    h = _rmsnorm(x_ref[...], g_ref[...]).astype(BF16)

    def mm(c0, c1):
        return jnp.dot(h, w_ref[:, c0:c1], preferred_element_type=F32)

    c = 0
    u_ref[...] = mm(c, c + D_POOL)
    c += D_POOL
    qkv_ref[:, 0:D_SB] = (mm(c, c + D_SB) * (LOG2E * SB_HEAD_DIM ** -0.5)).astype(BF16)
    c += D_SB
    for idx, hm_ref in ((1, k_ref), (2, v_ref)):
        t = mm(c, c + D_SB)
        c += D_SB
        qkv_ref[:, idx * D_SB:(idx + 1) * D_SB] = t.astype(BF16)
        for hd in range(SB_HEADS):
            hm_ref[0, hd, :, :] = t[:, hd * SB_HEAD_DIM:(hd + 1) * SB_HEAD_DIM]
    gate_ref[:, 0:D_MODEL] = mm(c, c + D_MODEL).astype(BF16)
    c += D_MODEL
    gate_ref[:, D_MODEL:2 * D_MODEL] = mm(c, c + D_MODEL).astype(BF16)


def _proj(x, norm_g, w_in, *, batch, seq, tm):
    n = x.shape[0]
    tiles_per_seq = seq // tm
    hm_spec = pl.BlockSpec((1, SB_HEADS, tm, SB_HEAD_DIM),
                           lambda i: (i // tiles_per_seq, 0, i % tiles_per_seq, 0))
    hm_shape = jax.ShapeDtypeStruct((batch, SB_HEADS, seq, SB_HEAD_DIM), F32)
    return pl.pallas_call(
        _proj_kernel,
        grid=(n // tm,),
        in_specs=[
            pl.BlockSpec((tm, D_MODEL), lambda i: (i, 0)),
            pl.BlockSpec((1, D_MODEL), lambda i: (0, 0)),
            pl.BlockSpec((D_MODEL, D_IN), lambda i: (0, 0)),
        ],
        out_specs=[
            pl.BlockSpec((tm, D_POOL), lambda i: (i, 0)),
            pl.BlockSpec((tm, 3 * D_SB), lambda i: (i, 0)),
            hm_spec,
            hm_spec,
            pl.BlockSpec((tm, 2 * D_MODEL), lambda i: (i, 0)),
        ],
        out_shape=[
            jax.ShapeDtypeStruct((n, D_POOL), F32),
            jax.ShapeDtypeStruct((n, 3 * D_SB), BF16),
            hm_shape,
            hm_shape,
            jax.ShapeDtypeStruct((n, 2 * D_MODEL), BF16),
        ],
        compiler_params=pltpu.CompilerParams(
            dimension_semantics=("parallel",), vmem_limit_bytes=VMEM_LIMIT_BYTES),
        name="proj",
    )(x, norm_g, w_in)


def _attn_kernel(q_ref, k_ref, v_ref, o_ref, acc_ref, car_ref, *, tq, hq, pos0):
    win = ATTN_WINDOW
    q_start = pos0 + pl.program_id(1) * tq
    subs = range(tq // hq)
    pairs = range(SB_HEADS // 2)
    pair_cols = [slice(p * LANES, (p + 1) * LANES) for p in pairs]

    lane = lax.broadcasted_iota(jnp.int32, (1, LANES), 1)
    first_head = lane < SB_HEAD_DIM
    r = lax.broadcasted_iota(jnp.int32, (win, win), 0)
    c = lax.broadcasted_iota(jnp.int32, (win, win), 1)
    newer = jnp.where(r > c, 1.0, 0.0).astype(BF16)

    def softplus2(z):
        return jnp.maximum(z, 0.0) + jnp.log2(1.0 + jnp.exp2(-jnp.abs(z)))

    def visit(m, mode):
        k_ws, v_ws, valids = [], [], []
        for j in subs:
            row0 = q_start + j * hq
            win_end = row0 + hq - m * win
            q_pos = row0 + lax.broadcasted_iota(jnp.int32, (hq, 1), 0)
            if mode == "inner":
                start = pl.multiple_of(win_end - win, hq)
                k_pos = (win_end - LANES) + lax.broadcasted_iota(jnp.int32, (1, LANES), 1)
                valids.append(k_pos < q_pos)
            else:
                start = pl.multiple_of(jnp.maximum(win_end - win, 0), hq)
                k_pos = start + lax.broadcasted_iota(jnp.int32, (1, win), 1)
                valids.append(k_pos < (q_pos if mode == "edge" else win_end))
            k_ws.append(k_ref[pl.ds(start, win), :])
            v_ws.append(v_ref[pl.ds(start, win), :])

        def mask(x, j):
            if mode == "inner":
                return jnp.concatenate(
                    [x[:, :win - LANES],
                     jnp.where(valids[j], x[:, win - LANES:], MASKED_LOGIT)], axis=1)
            return jnp.where(valids[j], x, MASKED_LOGIT)

        z_parts = []
        for j in subs:
            rows = slice(j * hq, (j + 1) * hq)
            for p in pairs:
                q_p = q_ref[rows, pair_cols[p]]
                zero = jnp.zeros_like(q_p)
                q_2 = jnp.concatenate([jnp.where(first_head, q_p, zero),
                                       jnp.where(first_head, zero, q_p)], axis=0)
                z_2 = lax.dot_general(q_2, k_ws[j][:, pair_cols[p]], (((1,), (1,)), ((), ())),
                                      preferred_element_type=F32)
                z_parts += [mask(z_2[:hq], j), mask(z_2[hq:], j)]
        z = jnp.concatenate(z_parts, axis=0)
        s = softplus2(z)
        sums = jnp.dot(s.astype(BF16), newer, preferred_element_type=F32)
        total = jnp.broadcast_to(jnp.sum(s, axis=1, keepdims=True), (s.shape[0], LANES))
        arg = (z - s) - sums
        if mode == "older":
            car = car_ref[...]
            arg = arg - jnp.concatenate([car] * (win // LANES), axis=1)
            total = car + total
        car_ref[...] = total
        w = jnp.exp2(arg).astype(BF16)

        i = 0
        for j in subs:
            rows = slice(j * hq, (j + 1) * hq)
            for p in pairs:
                o_2 = jnp.dot(w[i * hq:(i + 2) * hq], v_ws[j][:, pair_cols[p]],
                              preferred_element_type=F32)
                i += 2
                o_p = jnp.where(first_head, o_2[:hq], o_2[hq:])
                if mode == "older":
                    acc_ref[rows, pair_cols[p]] += o_p
                else:
                    acc_ref[rows, pair_cols[p]] = o_p
        return (jnp.min(total) >= SURVIVAL_UNDERFLOW_LOG2).astype(jnp.int32)

    done = lax.cond(q_start + hq >= win, lambda: visit(0, "inner"), lambda: visit(0, "edge"))

    def cond(carry):
        m, done = carry
        return jnp.logical_and(q_start + tq - m * win > 0, done == 0)

    def body(carry):
        m, _ = carry
        return m + 1, visit(m, "older")

    lax.while_loop(cond, body, (jnp.int32(1), done))
    o_ref[...] = acc_ref[...].astype(BF16)


def _attn(q_arr, q_col, k_arr, k_col, v_arr, v_col, *, batch, q_len, k_len, pos0, tq, hq):
    assert tq % hq == 0 and q_len % tq == 0 and pos0 % tq == 0 and ATTN_WINDOW % hq == 0
    assert pos0 + q_len <= k_len and k_len >= ATTN_WINDOW and hq % 16 == 0
    nq = q_len // tq
    stacked_rows = (tq // hq) * SB_HEADS * hq
    return pl.pallas_call(
        functools.partial(_attn_kernel, tq=tq, hq=hq, pos0=pos0),
        grid=(batch, nq),
        in_specs=[
            pl.BlockSpec((tq, D_SB), lambda b, i: (b * nq + i, q_col)),
            pl.BlockSpec((k_len, D_SB), lambda b, i: (b, k_col)),
            pl.BlockSpec((k_len, D_SB), lambda b, i: (b, v_col)),
        ],
        out_specs=pl.BlockSpec((tq, D_SB), lambda b, i: (b * nq + i, 0)),
        out_shape=jax.ShapeDtypeStruct((batch * q_len, D_SB), BF16),
        scratch_shapes=[pltpu.VMEM((tq, D_SB), F32), pltpu.VMEM((stacked_rows, LANES), F32)],
        compiler_params=pltpu.CompilerParams(
            dimension_semantics=("parallel", "arbitrary"), vmem_limit_bytes=VMEM_LIMIT_BYTES),
        name="attn",
    )(q_arr, k_arr, v_arr)


def _merge_kernel(x_ref, u_ref, halo_ref, o_ref, gate_ref, pw_ref, ps_ref, wbp_ref, wbs_ref,
                  wout_ref, y_ref, *, tm, pos0, tiles_per_seq):
    i = pl.program_id(0)
    pos = pos0 + (i % tiles_per_seq) * tm + lax.broadcasted_iota(jnp.int32, (tm, 1), 0)

    n_groups = len(POOL_WINDOWS)
    n_chunks = n_groups if tm % (16 * n_groups) == 0 else 1
    quarter = tm // n_chunks
    run = jnp.concatenate([halo_ref[0], u_ref[...]], axis=0)
    width, mixed, gated_sb = 1, [], []
    for gi, win in enumerate(POOL_WINDOWS):
        while width < win:
            run = run + pltpu.roll(run, width, 0)
            width *= 2
        assert width == win, "pool windows must be ascending powers of two"
        cols = slice(gi * POOL_GROUP, (gi + 1) * POOL_GROUP)
        cnt = jnp.minimum(win, pos + 1).astype(F32)
        d = (run[POOL_HALO:, 0:POOL_GROUP] / cnt - u_ref[:, cols]).astype(BF16)
        mixed.append(jnp.dot(d, pw_ref[gi], preferred_element_type=F32) * ps_ref[:, cols])
        if gi + 1 < n_groups:
            run = run[:, POOL_GROUP:]
        if gi < n_chunks:
            rows = slice(gi * quarter, (gi + 1) * quarter)
            br_sb = jnp.dot(o_ref[rows, :], wbs_ref[...], preferred_element_type=F32)
            gated_sb.append(
                jax.nn.sigmoid(gate_ref[rows, D_MODEL:2 * D_MODEL].astype(F32)) * br_sb)
    a = jnp.concatenate(mixed, axis=1).astype(BF16)

    br_pool = jnp.dot(a, wbp_ref[...], preferred_element_type=F32)
    merged = (jax.nn.sigmoid(gate_ref[:, 0:D_MODEL].astype(F32)) * br_pool
              + jnp.concatenate(gated_sb, axis=0))
    y_ref[...] = x_ref[...] + jnp.dot(merged.astype(BF16), wout_ref[...],
                                      preferred_element_type=F32)


def _merge(x, u, halo, o, gate, pool_w, pool_scale, w_bp, w_bs, w_out, *, tm, pos0, seq):
    n = x.shape[0]
    const2 = lambda i: (0, 0)
    return pl.pallas_call(
        functools.partial(_merge_kernel, tm=tm, pos0=pos0, tiles_per_seq=seq // tm),
        grid=(n // tm,),
        in_specs=[
            pl.BlockSpec((tm, D_MODEL), lambda i: (i, 0)),
            pl.BlockSpec((tm, D_POOL), lambda i: (i, 0)),
            pl.BlockSpec((1, POOL_HALO, D_POOL), lambda i: (i, 0, 0)),
            pl.BlockSpec((tm, D_SB), lambda i: (i, 0)),
            pl.BlockSpec((tm, 2 * D_MODEL), lambda i: (i, 0)),
            pl.BlockSpec((len(POOL_WINDOWS), POOL_GROUP, POOL_GROUP), lambda i: (0, 0, 0)),
            pl.BlockSpec((1, D_POOL), const2),
            pl.BlockSpec((D_POOL, D_MODEL), const2),
            pl.BlockSpec((D_SB, D_MODEL), const2),
            pl.BlockSpec((D_MODEL, D_MODEL), const2),
        ],
        out_specs=pl.BlockSpec((tm, D_MODEL), lambda i: (i, 0)),
        out_shape=jax.ShapeDtypeStruct((n, D_MODEL), F32),
        compiler_params=pltpu.CompilerParams(
            dimension_semantics=("parallel",), vmem_limit_bytes=VMEM_LIMIT_BYTES),
        name="merge",
    )(x, u, halo, o, gate, pool_w, pool_scale, w_bp, w_bs, w_out)


def _layer(x, w, *, batch, seq, pos0, k_past, v_past, pool_prefix, tm_ffn, tm_mix, tq, hq):
    assert seq >= POOL_STATE
    w_bf16 = {}

    def ffn(x_in, prefix, final_g=None):
        names = [prefix + "_gate", prefix + "_up", prefix + "_down"]
        res = _ffn(x_in, w[prefix + "_norm"], *[w[nm] for nm in names], final_g,
                   tm=tm_ffn, tf=FFN_CHUNK)
        if isinstance(res, (list, tuple)):
            w_bf16.update(zip(names, res[1:]))
            return res[0]
        return res

    x1 = ffn(x, "ffn1")
    u, qkv, k_hm, v_hm, gate = _proj(x1, w["mix_norm"], w["w_in"], batch=batch, seq=seq, tm=tm_mix)

    if k_past is None:
        o = _attn(qkv, 0, qkv, 1, qkv, 2, batch=batch, q_len=seq, k_len=seq, pos0=0, tq=tq, hq=hq)
    else:
        def with_past(past, col):
            new = qkv[:, col * D_SB:(col + 1) * D_SB].reshape(batch, seq, D_SB)
            return jnp.concatenate([past, new], axis=1).reshape(batch * (pos0 + seq), D_SB)
        o = _attn(qkv, 0, with_past(k_past, 1), 0, with_past(v_past, 2), 0,
                  batch=batch, q_len=seq, k_len=pos0 + seq, pos0=pos0, tq=tq, hq=hq)

    tiles = seq // tm_mix
    u4 = u.reshape(batch, tiles, tm_mix, D_POOL)
    first = jnp.pad(pool_prefix, ((0, 0), (1, 0), (0, 0)))[:, None]
    halo = jnp.concatenate([first, u4[:, :-1, tm_mix - POOL_HALO:, :]], axis=1)
    halo = halo.reshape(batch * tiles, POOL_HALO, D_POOL)

    x2 = _merge(x1, u, halo, o, gate, w["pool_w"], w["pool_scale"], w["w_branch_pool"],
                w["w_branch_sb"], w["w_out"], tm=tm_mix, pos0=pos0, seq=seq)
    y = ffn(x2, "ffn2", w["final_norm"])
    new_pool = u.reshape(batch, seq, D_POOL)[:, seq - POOL_STATE:, :]
    return (y.reshape(batch, seq, D_MODEL), k_hm, v_hm, new_pool), w_bf16


def kernel(x_prompt, x_sample, cache_k, cache_v, state_pool, ffn1_norm, ffn1_gate, ffn1_up, ffn1_down, mix_norm, w_in, pool_w, pool_scale, w_branch_pool, w_branch_sb, w_out, ffn2_norm, ffn2_gate, ffn2_up, ffn2_down, final_norm):
    assert ffn1_norm.shape[0] == 1, "single-layer kernel"
    mixer_mats = dict(w_in=w_in, pool_w=pool_w, w_branch_pool=w_branch_pool,
                      w_branch_sb=w_branch_sb, w_out=w_out)
    w = {name: m[0].astype(BF16) for name, m in mixer_mats.items()}
    w.update(ffn1_gate=ffn1_gate[0], ffn1_up=ffn1_up[0], ffn1_down=ffn1_down[0],
             ffn2_gate=ffn2_gate[0], ffn2_up=ffn2_up[0], ffn2_down=ffn2_down[0])
    w.update(ffn1_norm=ffn1_norm, mix_norm=mix_norm, pool_scale=pool_scale, ffn2_norm=ffn2_norm,
             final_norm=final_norm.reshape(1, D_MODEL))

    b_p, s_p, _ = x_prompt.shape
    b_d, s_d, _ = x_sample.shape
    past = cache_k.shape[3]

    def token_major(cache):
        return cache[0].transpose(0, 2, 1, 3).reshape(b_d, past, D_SB).astype(BF16)

    (y_d, k_d, v_d, pool_d), ffn_bf16 = _layer(
        x_sample.reshape(b_d * s_d, D_MODEL), w, batch=b_d, seq=s_d, pos0=past,
        k_past=token_major(cache_k), v_past=token_major(cache_v), pool_prefix=state_pool[0],
        tm_ffn=b_d * s_d, tm_mix=s_d, tq=s_d, hq=s_d)

    (y_p, k_p, v_p, pool_p), _ = _layer(
        x_prompt.reshape(b_p * s_p, D_MODEL), {**w, **ffn_bf16}, batch=b_p, seq=s_p, pos0=0,
        k_past=None, v_past=None, pool_prefix=jnp.zeros((b_p, POOL_STATE, D_POOL), F32),
        tm_ffn=1024, tm_mix=512, tq=512, hq=64)

    return (y_p, y_d, k_p[None], v_p[None], pool_p[None], k_d[None], v_d[None], pool_d[None])
```

```python
import functools

import jax
import jax.numpy as jnp
from jax import lax
from jax.experimental import pallas as pl
from jax.experimental.pallas import tpu as pltpu

F32 = jnp.float32
BF16 = jnp.bfloat16

D_MODEL = 1024
D_FF = 4 * D_MODEL
D_POOL = D_MODEL // 2
POOL_WINDOWS = (2, 4, 8, 16)
POOL_GROUP = D_POOL // len(POOL_WINDOWS)
POOL_STATE = max(POOL_WINDOWS) - 1
POOL_HALO = POOL_STATE + 1
SB_HEADS = 8
SB_HEAD_DIM = 64
D_SB = SB_HEADS * SB_HEAD_DIM
D_IN = D_POOL + 3 * D_SB + 2 * D_MODEL
RMS_EPS = 1e-6
LANES = 128
VMEM_LIMIT_BYTES = 56 * 1024 * 1024
FFN_CHUNK = 1024
ATTN_WINDOW = 2 * LANES

LOG2E = 1.4426950408889634
SURVIVAL_UNDERFLOW_LOG2 = 150.0
MASKED_LOGIT = -1e30


def _rmsnorm(x, g):
    ms = jnp.mean(x * x, axis=-1, keepdims=True)
    return (x * lax.rsqrt(ms + RMS_EPS)) * g


def _ffn_kernel(*refs, n_ff, final, cast):
    refs = list(refs)
    x_ref, xnext_ref, g_ref, wg_ref, wu_ref, wd_ref = refs[:6]
    del refs[:6]
    fg_ref = refs.pop(0) if final else None
    o_ref = refs.pop(0)
    if cast:
        wg_out, wu_out, wd_out = refs[:3]
        del refs[:3]
    xn_ref, acc_ref = refs
    i = pl.program_id(0)
    j = pl.program_id(1)
    slot = i % 2

    @pl.when(jnp.logical_and(i == 0, j == 0))
    def _():
        xn_ref[0] = _rmsnorm(x_ref[...], g_ref[...]).astype(BF16)

    def chunk():
        wg, wu, wd = wg_ref[...], wu_ref[...], wd_ref[...]
        if cast:
            wg, wu, wd = wg.astype(BF16), wu.astype(BF16), wd.astype(BF16)
            wg_out[...] = wg
            wu_out[...] = wu
            wd_out[...] = wd
        xn = xn_ref[slot]
        gate = jnp.dot(xn, wg, preferred_element_type=F32)
        up = jnp.dot(xn, wu, preferred_element_type=F32)
        h = (gate * jax.nn.sigmoid(gate) * up).astype(BF16)
        return jnp.dot(h, wd, preferred_element_type=F32)

    def prepare_next(rows):
        xn_ref[1 - slot, rows, :] = _rmsnorm(xnext_ref[rows, :], g_ref[...]).astype(BF16)

    piece = x_ref.shape[0] // n_ff

    def first():
        acc_ref[...] = chunk()
        prepare_next(pl.ds(0, piece))

    def middle():
        acc_ref[...] += chunk()
        prepare_next(pl.ds(pl.multiple_of(j * piece, piece), piece))

    def last():
        y = x_ref[...] + 0.5 * (acc_ref[...] + chunk())
        if final:
            y = _rmsnorm(y, fg_ref[...])
        o_ref[...] = y
        prepare_next(pl.ds((n_ff - 1) * piece, piece))

    case = jnp.where(j == 0, 0, jnp.where(j == n_ff - 1, 2, 1))
    lax.switch(case, [first, middle, last])


def _ffn(x, norm_g, wg, wu, wd, final_g=None, *, tm, tf):
    n = x.shape[0]
    n_rows = n // tm
    n_ff = D_FF // tf
    assert n_ff >= 2 and tm % (16 * n_ff) == 0
    final = final_g is not None
    cast = wg.dtype == F32
    assert not cast or n_rows == 1
    row = pl.BlockSpec((1, D_MODEL), lambda i, j: (0, 0))
    w_specs = [
        pl.BlockSpec((D_MODEL, tf), lambda i, j: (0, j)),
        pl.BlockSpec((D_MODEL, tf), lambda i, j: (0, j)),
        pl.BlockSpec((tf, D_MODEL), lambda i, j: (j, 0)),
    ]
    in_specs = [
        pl.BlockSpec((tm, D_MODEL), lambda i, j: (i, 0)),
        pl.BlockSpec((tm, D_MODEL), lambda i, j: (jnp.minimum(i + 1, n_rows - 1), 0)),
        row,
    ] + w_specs
    args = [x, x, norm_g, wg, wu, wd]
    if final:
        in_specs.append(row)
        args.append(final_g)
    out_specs = [pl.BlockSpec((tm, D_MODEL), lambda i, j: (i, 0))]
    out_shape = [jax.ShapeDtypeStruct((n, D_MODEL), F32)]
    if cast:
        out_specs += w_specs
        out_shape += [jax.ShapeDtypeStruct(m.shape, BF16) for m in (wg, wu, wd)]
    outs = pl.pallas_call(
        functools.partial(_ffn_kernel, n_ff=n_ff, final=final, cast=cast),
        grid=(n_rows, n_ff),
        in_specs=in_specs,
        out_specs=out_specs,
        out_shape=out_shape,
        scratch_shapes=[pltpu.VMEM((2, tm, D_MODEL), BF16), pltpu.VMEM((tm, D_MODEL), F32)],
        compiler_params=pltpu.CompilerParams(
            dimension_semantics=("arbitrary", "arbitrary"), vmem_limit_bytes=VMEM_LIMIT_BYTES),
        name=("ffn_final" if final else "ffn") + ("_cast" if cast else ""),
    )(*args)
    return outs if cast else outs[0]


def _proj_kernel(x_ref, g_ref, w_ref, u_ref, qkv_ref, k_ref, v_ref, gate_ref):
    h = _rmsnorm(x_ref[...], g_ref[...]).astype(BF16)

    def mm(c0, c1):
        return jnp.dot(h, w_ref[:, c0:c1], preferred_element_type=F32)

    c = 0
    u_ref[...] = mm(c, c + D_POOL)
    c += D_POOL
    qkv_ref[:, 0:D_SB] = (mm(c, c + D_SB) * (LOG2E * SB_HEAD_DIM ** -0.5)).astype(BF16)
    c += D_SB
    for idx, hm_ref in ((1, k_ref), (2, v_ref)):
        t = mm(c, c + D_SB)
        c += D_SB
        qkv_ref[:, idx * D_SB:(idx + 1) * D_SB] = t.astype(BF16)
        for hd in range(SB_HEADS):
            hm_ref[0, hd, :, :] = t[:, hd * SB_HEAD_DIM:(hd + 1) * SB_HEAD_DIM]
    gate_ref[:, 0:D_MODEL] = mm(c, c + D_MODEL).astype(BF16)
    c += D_MODEL
    gate_ref[:, D_MODEL:2 * D_MODEL] = mm(c, c + D_MODEL).astype(BF16)


def _proj(x, norm_g, w_in, *, batch, seq, tm):
    n = x.shape[0]
    tiles_per_seq = seq // tm
    hm_spec = pl.BlockSpec((1, SB_HEADS, tm, SB_HEAD_DIM),
                           lambda i: (i // tiles_per_seq, 0, i % tiles_per_seq, 0))
    hm_shape = jax.ShapeDtypeStruct((batch, SB_HEADS, seq, SB_HEAD_DIM), F32)
    return pl.pallas_call(
        _proj_kernel,
        grid=(n // tm,),
        in_specs=[
            pl.BlockSpec((tm, D_MODEL), lambda i: (i, 0)),
            pl.BlockSpec((1, D_MODEL), lambda i: (0, 0)),
            pl.BlockSpec((D_MODEL, D_IN), lambda i: (0, 0)),
        ],
        out_specs=[
            pl.BlockSpec((tm, D_POOL), lambda i: (i, 0)),
            pl.BlockSpec((tm, 3 * D_SB), lambda i: (i, 0)),
            hm_spec,
            hm_spec,
            pl.BlockSpec((tm, 2 * D_MODEL), lambda i: (i, 0)),
        ],
        out_shape=[
            jax.ShapeDtypeStruct((n, D_POOL), F32),
            jax.ShapeDtypeStruct((n, 3 * D_SB), BF16),
            hm_shape,
            hm_shape,
            jax.ShapeDtypeStruct((n, 2 * D_MODEL), BF16),
        ],
        compiler_params=pltpu.CompilerParams(
            dimension_semantics=("parallel",), vmem_limit_bytes=VMEM_LIMIT_BYTES),
        name="proj",
    )(x, norm_g, w_in)


def _attn_kernel(q_ref, k_ref, v_ref, o_ref, acc_ref, car_ref, *, tq, hq, pos0):
    win = ATTN_WINDOW
    q_start = pos0 + pl.program_id(1) * tq
    subs = range(tq // hq)
    pairs = range(SB_HEADS // 2)
    pair_cols = [slice(p * LANES, (p + 1) * LANES) for p in pairs]

    lane = lax.broadcasted_iota(jnp.int32, (1, LANES), 1)
    first_head = lane < SB_HEAD_DIM
    r = lax.broadcasted_iota(jnp.int32, (win, win), 0)
    c = lax.broadcasted_iota(jnp.int32, (win, win), 1)
    newer = jnp.where(r > c, 1.0, 0.0).astype(BF16)

    def softplus2(z):
        return jnp.maximum(z, 0.0) + jnp.log2(1.0 + jnp.exp2(-jnp.abs(z)))

    def visit(m, mode):
        k_ws, v_ws, valids = [], [], []
        for j in subs:
            row0 = q_start + j * hq
            win_end = row0 + hq - m * win
            q_pos = row0 + lax.broadcasted_iota(jnp.int32, (hq, 1), 0)
            if mode == "inner":
                start = pl.multiple_of(win_end - win, hq)
                k_pos = (win_end - LANES) + lax.broadcasted_iota(jnp.int32, (1, LANES), 1)
                valids.append(k_pos < q_pos)
            else:
                start = pl.multiple_of(jnp.maximum(win_end - win, 0), hq)
                k_pos = start + lax.broadcasted_iota(jnp.int32, (1, win), 1)
                valids.append(k_pos < (q_pos if mode == "edge" else win_end))
            k_ws.append(k_ref[pl.ds(start, win), :])
            v_ws.append(v_ref[pl.ds(start, win), :])

        def mask(x, j):
            if mode == "inner":
                return jnp.concatenate(
                    [x[:, :win - LANES],
                     jnp.where(valids[j], x[:, win - LANES:], MASKED_LOGIT)], axis=1)
            return jnp.where(valids[j], x, MASKED_LOGIT)

        z_parts = []
        for j in subs:
            rows = slice(j * hq, (j + 1) * hq)
            for p in pairs:
                q_p = q_ref[rows, pair_cols[p]]
                zero = jnp.zeros_like(q_p)
                q_2 = jnp.concatenate([jnp.where(first_head, q_p, zero),
                                       jnp.where(first_head, zero, q_p)], axis=0)
                z_2 = lax.dot_general(q_2, k_ws[j][:, pair_cols[p]], (((1,), (1,)), ((), ())),
                                      preferred_element_type=F32)
                z_parts += [mask(z_2[:hq], j), mask(z_2[hq:], j)]
        z = jnp.concatenate(z_parts, axis=0)
        s = softplus2(z)
        sums = jnp.dot(s.astype(BF16), newer, preferred_element_type=F32)
        total = jnp.broadcast_to(jnp.sum(s, axis=1, keepdims=True), (s.shape[0], LANES))
        arg = (z - s) - sums
        if mode == "older":
            car = car_ref[...]
            arg = arg - jnp.concatenate([car] * (win // LANES), axis=1)
            total = car + total
        car_ref[...] = total
        w = jnp.exp2(arg).astype(BF16)

        i = 0
        for j in subs:
            rows = slice(j * hq, (j + 1) * hq)
            for p in pairs:
                o_2 = jnp.dot(w[i * hq:(i + 2) * hq], v_ws[j][:, pair_cols[p]],
                              preferred_element_type=F32)
                i += 2
                o_p = jnp.where(first_head, o_2[:hq], o_2[hq:])
                if mode == "older":
                    acc_ref[rows, pair_cols[p]] += o_p
                else:
                    acc_ref[rows, pair_cols[p]] = o_p
        return (jnp.min(total) >= SURVIVAL_UNDERFLOW_LOG2).astype(jnp.int32)

    done = lax.cond(q_start + hq >= win, lambda: visit(0, "inner"), lambda: visit(0, "edge"))

    def cond(carry):
        m, done = carry
        return jnp.logical_and(q_start + tq - m * win > 0, done == 0)

    def body(carry):
        m, _ = carry
        return m + 1, visit(m, "older")

    lax.while_loop(cond, body, (jnp.int32(1), done))
    o_ref[...] = acc_ref[...].astype(BF16)


def _attn(q_arr, q_col, k_arr, k_col, v_arr, v_col, *, batch, q_len, k_len, pos0, tq, hq):
    assert tq % hq == 0 and q_len % tq == 0 and pos0 % tq == 0 and ATTN_WINDOW % hq == 0
    assert pos0 + q_len <= k_len and k_len >= ATTN_WINDOW and hq % 16 == 0
    nq = q_len // tq
    stacked_rows = (tq // hq) * SB_HEADS * hq
    return pl.pallas_call(
        functools.partial(_attn_kernel, tq=tq, hq=hq, pos0=pos0),
        grid=(batch, nq),
        in_specs=[
            pl.BlockSpec((tq, D_SB), lambda b, i: (b * nq + i, q_col)),
            pl.BlockSpec((k_len, D_SB), lambda b, i: (b, k_col)),
            pl.BlockSpec((k_len, D_SB), lambda b, i: (b, v_col)),
        ],
        out_specs=pl.BlockSpec((tq, D_SB), lambda b, i: (b * nq + i, 0)),
        out_shape=jax.ShapeDtypeStruct((batch * q_len, D_SB), BF16),
        scratch_shapes=[pltpu.VMEM((tq, D_SB), F32), pltpu.VMEM((stacked_rows, LANES), F32)],
        compiler_params=pltpu.CompilerParams(
            dimension_semantics=("parallel", "arbitrary"), vmem_limit_bytes=VMEM_LIMIT_BYTES),
        name="attn",
    )(q_arr, k_arr, v_arr)


def _merge_kernel(x_ref, u_ref, halo_ref, o_ref, gate_ref, pw_ref, ps_ref, wbp_ref, wbs_ref,
                  wout_ref, y_ref, *, tm, pos0, tiles_per_seq):
    i = pl.program_id(0)
    pos = pos0 + (i % tiles_per_seq) * tm + lax.broadcasted_iota(jnp.int32, (tm, 1), 0)

    n_groups = len(POOL_WINDOWS)
    n_chunks = n_groups if tm % (16 * n_groups) == 0 else 1
    quarter = tm // n_chunks
    run = jnp.concatenate([halo_ref[0], u_ref[...]], axis=0)
    width, mixed, gated_sb = 1, [], []
    for gi, win in enumerate(POOL_WINDOWS):
        while width < win:
            run = run + pltpu.roll(run, width, 0)
            width *= 2
        assert width == win, "pool windows must be ascending powers of two"
        cols = slice(gi * POOL_GROUP, (gi + 1) * POOL_GROUP)
        cnt = jnp.minimum(win, pos + 1).astype(F32)
        d = (run[POOL_HALO:, 0:POOL_GROUP] / cnt - u_ref[:, cols]).astype(BF16)
        mixed.append(jnp.dot(d, pw_ref[gi], preferred_element_type=F32) * ps_ref[:, cols])
        if gi + 1 < n_groups:
            run = run[:, POOL_GROUP:]
        if gi < n_chunks:
            rows = slice(gi * quarter, (gi + 1) * quarter)
            br_sb = jnp.dot(o_ref[rows, :], wbs_ref[...], preferred_element_type=F32)
            gated_sb.append(
                jax.nn.sigmoid(gate_ref[rows, D_MODEL:2 * D_MODEL].astype(F32)) * br_sb)
    a = jnp.concatenate(mixed, axis=1).astype(BF16)

    br_pool = jnp.dot(a, wbp_ref[...], preferred_element_type=F32)
    merged = (jax.nn.sigmoid(gate_ref[:, 0:D_MODEL].astype(F32)) * br_pool
              + jnp.concatenate(gated_sb, axis=0))
    y_ref[...] = x_ref[...] + jnp.dot(merged.astype(BF16), wout_ref[...],
                                      preferred_element_type=F32)


def _merge(x, u, halo, o, gate, pool_w, pool_scale, w_bp, w_bs, w_out, *, tm, pos0, seq):
    n = x.shape[0]
    const2 = lambda i: (0, 0)
    resident = pl.Buffered(1)
    return pl.pallas_call(
        functools.partial(_merge_kernel, tm=tm, pos0=pos0, tiles_per_seq=seq // tm),
        grid=(n // tm,),
        in_specs=[
            pl.BlockSpec((tm, D_MODEL), lambda i: (i, 0)),
            pl.BlockSpec((tm, D_POOL), lambda i: (i, 0)),
            pl.BlockSpec((1, POOL_HALO, D_POOL), lambda i: (i, 0, 0)),
            pl.BlockSpec((tm, D_SB), lambda i: (i, 0)),
            pl.BlockSpec((tm, 2 * D_MODEL), lambda i: (i, 0)),
            pl.BlockSpec((len(POOL_WINDOWS), POOL_GROUP, POOL_GROUP), lambda i: (0, 0, 0)),
            pl.BlockSpec((1, D_POOL), const2),
            pl.BlockSpec((D_POOL, D_MODEL), const2, pipeline_mode=resident),
            pl.BlockSpec((D_SB, D_MODEL), const2, pipeline_mode=resident),
            pl.BlockSpec((D_MODEL, D_MODEL), const2, pipeline_mode=resident),
        ],
        out_specs=pl.BlockSpec((tm, D_MODEL), lambda i: (i, 0)),
        out_shape=jax.ShapeDtypeStruct((n, D_MODEL), F32),
        compiler_params=pltpu.CompilerParams(
            dimension_semantics=("parallel",), vmem_limit_bytes=VMEM_LIMIT_BYTES),
        name="merge",
    )(x, u, halo, o, gate, pool_w, pool_scale, w_bp, w_bs, w_out)


def _layer(x, w, *, batch, seq, pos0, k_past, v_past, pool_prefix, tm_ffn, tm_mix, tm_merge, tq,
           hq):
    assert seq >= POOL_STATE
    w_bf16 = {}

    def ffn(x_in, prefix, final_g=None):
        names = [prefix + "_gate", prefix + "_up", prefix + "_down"]
        res = _ffn(x_in, w[prefix + "_norm"], *[w[nm] for nm in names], final_g,
                   tm=tm_ffn, tf=FFN_CHUNK)
        if isinstance(res, (list, tuple)):
            w_bf16.update(zip(names, res[1:]))
            return res[0]
        return res

    x1 = ffn(x, "ffn1")
    u, qkv, k_hm, v_hm, gate = _proj(x1, w["mix_norm"], w["w_in"], batch=batch, seq=seq, tm=tm_mix)

    if k_past is None:
        o = _attn(qkv, 0, qkv, 1, qkv, 2, batch=batch, q_len=seq, k_len=seq, pos0=0, tq=tq, hq=hq)
    else:
        def with_past(past, col):
            new = qkv[:, col * D_SB:(col + 1) * D_SB].reshape(batch, seq, D_SB)
            return jnp.concatenate([past, new], axis=1).reshape(batch * (pos0 + seq), D_SB)
        o = _attn(qkv, 0, with_past(k_past, 1), 0, with_past(v_past, 2), 0,
                  batch=batch, q_len=seq, k_len=pos0 + seq, pos0=pos0, tq=tq, hq=hq)

    tiles = seq // tm_merge
    u4 = u.reshape(batch, tiles, tm_merge, D_POOL)
    first = jnp.pad(pool_prefix, ((0, 0), (1, 0), (0, 0)))[:, None]
    halo = jnp.concatenate([first, u4[:, :-1, tm_merge - POOL_HALO:, :]], axis=1)
    halo = halo.reshape(batch * tiles, POOL_HALO, D_POOL)

    x2 = _merge(x1, u, halo, o, gate, w["pool_w"], w["pool_scale"], w["w_branch_pool"],
                w["w_branch_sb"], w["w_out"], tm=tm_merge, pos0=pos0, seq=seq)
    y = ffn(x2, "ffn2", w["final_norm"])
    new_pool = u.reshape(batch, seq, D_POOL)[:, seq - POOL_STATE:, :]
    return (y.reshape(batch, seq, D_MODEL), k_hm, v_hm, new_pool), w_bf16


def kernel(x_prompt, x_sample, cache_k, cache_v, state_pool, ffn1_norm, ffn1_gate, ffn1_up, ffn1_down, mix_norm, w_in, pool_w, pool_scale, w_branch_pool, w_branch_sb, w_out, ffn2_norm, ffn2_gate, ffn2_up, ffn2_down, final_norm):
    assert ffn1_norm.shape[0] == 1, "single-layer kernel"
    mixer_mats = dict(w_in=w_in, pool_w=pool_w, w_branch_pool=w_branch_pool,
                      w_branch_sb=w_branch_sb, w_out=w_out)
    w = {name: m[0].astype(BF16) for name, m in mixer_mats.items()}
    w.update(ffn1_gate=ffn1_gate[0], ffn1_up=ffn1_up[0], ffn1_down=ffn1_down[0],
             ffn2_gate=ffn2_gate[0], ffn2_up=ffn2_up[0], ffn2_down=ffn2_down[0])
    w.update(ffn1_norm=ffn1_norm, mix_norm=mix_norm, pool_scale=pool_scale, ffn2_norm=ffn2_norm,
             final_norm=final_norm.reshape(1, D_MODEL))

    b_p, s_p, _ = x_prompt.shape
    b_d, s_d, _ = x_sample.shape
    past = cache_k.shape[3]

    def token_major(cache):
        return cache[0].transpose(0, 2, 1, 3).reshape(b_d, past, D_SB).astype(BF16)

    (y_d, k_d, v_d, pool_d), ffn_bf16 = _layer(
        x_sample.reshape(b_d * s_d, D_MODEL), w, batch=b_d, seq=s_d, pos0=past,
        k_past=token_major(cache_k), v_past=token_major(cache_v), pool_prefix=state_pool[0],
        tm_ffn=b_d * s_d, tm_mix=s_d, tm_merge=s_d, tq=s_d, hq=s_d)

    (y_p, k_p, v_p, pool_p), _ = _layer(
        x_prompt.reshape(b_p * s_p, D_MODEL), {**w, **ffn_bf16}, batch=b_p, seq=s_p, pos0=0,
        k_past=None, v_past=None, pool_prefix=jnp.zeros((b_p, POOL_STATE, D_POOL), F32),
        tm_ffn=1024, tm_mix=512, tm_merge=1024, tq=512, hq=64)

    return (y_p, y_d, k_p[None], v_p[None], pool_p[None], k_d[None], v_d[None], pool_d[None])
```

```python
import functools

import jax
import jax.numpy as jnp
from jax import lax
from jax.experimental import pallas as pl
from jax.experimental.pallas import tpu as pltpu

F32 = jnp.float32
BF16 = jnp.bfloat16

D_MODEL = 1024
D_FF = 4 * D_MODEL
D_POOL = D_MODEL // 2
POOL_WINDOWS = (2, 4, 8, 16)
POOL_GROUP = D_POOL // len(POOL_WINDOWS)
POOL_STATE = max(POOL_WINDOWS) - 1
POOL_HALO = POOL_STATE + 1
SB_HEADS = 8
SB_HEAD_DIM = 64
D_SB = SB_HEADS * SB_HEAD_DIM
D_IN = D_POOL + 3 * D_SB + 2 * D_MODEL
RMS_EPS = 1e-6
LANES = 128
BF16_SUBLANES = 16
VMEM_LIMIT_BYTES = 56 * 1024 * 1024
FFN_CHUNK = 1024
ATTN_WINDOW = 2 * LANES

LOG2E = 1.4426950408889634
SURVIVAL_UNDERFLOW_LOG2 = 150.0
MASKED_LOGIT = -1e30


def _rmsnorm(x, g):
    ms = jnp.mean(x * x, axis=-1, keepdims=True)
    return (x * lax.rsqrt(ms + RMS_EPS)) * g


def _ffn_kernel(*refs, n_ff, final, cast):
    refs = list(refs)
    x_ref, xnext_ref, g_ref, wg_ref, wu_ref, wd_ref = refs[:6]
    del refs[:6]
    fg_ref = refs.pop(0) if final else None
    o_ref = refs.pop(0)
    if cast:
        wg_out, wu_out, wd_out = refs[:3]
        del refs[:3]
    xn_ref, acc_ref = refs
    i = pl.program_id(0)
    j = pl.program_id(1)
    slot = i % 2

    @pl.when(jnp.logical_and(i == 0, j == 0))
    def _():
        xn_ref[0] = _rmsnorm(x_ref[...], g_ref[...]).astype(BF16)

    def chunk():
        wg, wu, wd = wg_ref[...], wu_ref[...], wd_ref[...]
        if cast:
            wg, wu, wd = wg.astype(BF16), wu.astype(BF16), wd.astype(BF16)
            wg_out[...] = wg
            wu_out[...] = wu
            wd_out[...] = wd
        xn = xn_ref[slot]
        gate = jnp.dot(xn, wg, preferred_element_type=F32)
        up = jnp.dot(xn, wu, preferred_element_type=F32)
        h = (gate * jax.nn.sigmoid(gate) * up).astype(BF16)
        return jnp.dot(h, wd, preferred_element_type=F32)

    def prepare_next(rows):
        xn_ref[1 - slot, rows, :] = _rmsnorm(xnext_ref[rows, :], g_ref[...]).astype(BF16)

    piece = x_ref.shape[0] // n_ff

    def first():
        acc_ref[...] = chunk()
        prepare_next(pl.ds(0, piece))

    def middle():
        acc_ref[...] += chunk()
        prepare_next(pl.ds(pl.multiple_of(j * piece, piece), piece))

    def last():
        y = x_ref[...] + 0.5 * (acc_ref[...] + chunk())
        if final:
            y = _rmsnorm(y, fg_ref[...])
        o_ref[...] = y
        prepare_next(pl.ds((n_ff - 1) * piece, piece))

    case = jnp.where(j == 0, 0, jnp.where(j == n_ff - 1, 2, 1))
    lax.switch(case, [first, middle, last])


def _ffn(x, norm_g, wg, wu, wd, final_g=None, *, tm, tf):
    n = x.shape[0]
    n_rows = n // tm
    n_ff = D_FF // tf
    assert n_ff >= 2 and tm % (BF16_SUBLANES * n_ff) == 0
    final = final_g is not None
    cast = wg.dtype == F32
    assert not cast or n_rows == 1
    row = pl.BlockSpec((1, D_MODEL), lambda i, j: (0, 0))
    w_specs = [
        pl.BlockSpec((D_MODEL, tf), lambda i, j: (0, j)),
        pl.BlockSpec((D_MODEL, tf), lambda i, j: (0, j)),
        pl.BlockSpec((tf, D_MODEL), lambda i, j: (j, 0)),
    ]
    in_specs = [
        pl.BlockSpec((tm, D_MODEL), lambda i, j: (i, 0)),
        pl.BlockSpec((tm, D_MODEL), lambda i, j: (jnp.minimum(i + 1, n_rows - 1), 0)),
        row,
    ] + w_specs
    args = [x, x, norm_g, wg, wu, wd]
    if final:
        in_specs.append(row)
        args.append(final_g)
    out_specs = [pl.BlockSpec((tm, D_MODEL), lambda i, j: (i, 0))]
    out_shape = [jax.ShapeDtypeStruct((n, D_MODEL), F32)]
    if cast:
        out_specs += w_specs
        out_shape += [jax.ShapeDtypeStruct(m.shape, BF16) for m in (wg, wu, wd)]
    outs = pl.pallas_call(
        functools.partial(_ffn_kernel, n_ff=n_ff, final=final, cast=cast),
        grid=(n_rows, n_ff),
        in_specs=in_specs,
        out_specs=out_specs,
        out_shape=out_shape,
        scratch_shapes=[pltpu.VMEM((2, tm, D_MODEL), BF16), pltpu.VMEM((tm, D_MODEL), F32)],
        compiler_params=pltpu.CompilerParams(
            dimension_semantics=("arbitrary", "arbitrary"), vmem_limit_bytes=VMEM_LIMIT_BYTES),
        name=("ffn_final" if final else "ffn") + ("_cast" if cast else ""),
    )(*args)
    return outs if cast else outs[0]


def _proj_kernel(x_ref, g_ref, w_ref, u_ref, qkv_ref, k_ref, v_ref, gate_ref):
    h = _rmsnorm(x_ref[...], g_ref[...]).astype(BF16)

    def mm(c0, c1):
        return jnp.dot(h, w_ref[:, c0:c1], preferred_element_type=F32)

    c = 0
    u_ref[...] = mm(c, c + D_POOL)
    c += D_POOL
    qkv_ref[:, 0:D_SB] = (mm(c, c + D_SB) * (LOG2E * SB_HEAD_DIM ** -0.5)).astype(BF16)
    c += D_SB
    for idx, hm_ref in ((1, k_ref), (2, v_ref)):
        t = mm(c, c + D_SB)
        c += D_SB
        qkv_ref[:, idx * D_SB:(idx + 1) * D_SB] = t.astype(BF16)
        streams, _, rows, _ = hm_ref.shape
        for b in range(streams):
            for hd in range(SB_HEADS):
                hm_ref[b, hd, :, :] = t[b * rows:(b + 1) * rows,
                                        hd * SB_HEAD_DIM:(hd + 1) * SB_HEAD_DIM]
    gate_ref[:, 0:D_MODEL] = mm(c, c + D_MODEL).astype(BF16)
    c += D_MODEL
    gate_ref[:, D_MODEL:2 * D_MODEL] = mm(c, c + D_MODEL).astype(BF16)


def _proj(x, norm_g, w_in, *, batch, seq, tm):
    n = x.shape[0]
    if tm <= seq:
        tiles_per_seq = seq // tm
        hm_spec = pl.BlockSpec((1, SB_HEADS, tm, SB_HEAD_DIM),
                               lambda i: (i // tiles_per_seq, 0, i % tiles_per_seq, 0))
    else:
        assert tm % seq == 0
        hm_spec = pl.BlockSpec((tm // seq, SB_HEADS, seq, SB_HEAD_DIM), lambda i: (i, 0, 0, 0))
    hm_shape = jax.ShapeDtypeStruct((batch, SB_HEADS, seq, SB_HEAD_DIM), F32)
    return pl.pallas_call(
        _proj_kernel,
        grid=(n // tm,),
        in_specs=[
            pl.BlockSpec((tm, D_MODEL), lambda i: (i, 0)),
            pl.BlockSpec((1, D_MODEL), lambda i: (0, 0)),
            pl.BlockSpec((D_MODEL, D_IN), lambda i: (0, 0)),
        ],
        out_specs=[
            pl.BlockSpec((tm, D_POOL), lambda i: (i, 0)),
            pl.BlockSpec((tm, 3 * D_SB), lambda i: (i, 0)),
            hm_spec,
            hm_spec,
            pl.BlockSpec((tm, 2 * D_MODEL), lambda i: (i, 0)),
        ],
        out_shape=[
            jax.ShapeDtypeStruct((n, D_POOL), F32),
            jax.ShapeDtypeStruct((n, 3 * D_SB), BF16),
            hm_shape,
            hm_shape,
            jax.ShapeDtypeStruct((n, 2 * D_MODEL), BF16),
        ],
        compiler_params=pltpu.CompilerParams(
            dimension_semantics=("parallel",), vmem_limit_bytes=VMEM_LIMIT_BYTES),
        name="proj",
    )(x, norm_g, w_in)


def _attn_kernel(q_ref, k_ref, v_ref, o_ref, acc_ref, car_ref, *, tq, hq, pos0):
    win = ATTN_WINDOW
    q_start = pos0 + pl.program_id(1) * tq
    subs = range(tq // hq)
    pairs = range(SB_HEADS // 2)
    pair_cols = [slice(p * LANES, (p + 1) * LANES) for p in pairs]

    lane = lax.broadcasted_iota(jnp.int32, (1, LANES), 1)
    first_head = lane < SB_HEAD_DIM
    r = lax.broadcasted_iota(jnp.int32, (win, win), 0)
    c = lax.broadcasted_iota(jnp.int32, (win, win), 1)
    newer = jnp.where(r > c, 1.0, 0.0).astype(BF16)

    def softplus2(z):
        return jnp.maximum(z, 0.0) + jnp.log2(1.0 + jnp.exp2(-jnp.abs(z)))

    def visit(m, mode):
        k_ws, v_ws, valids = [], [], []
        for j in subs:
            row0 = q_start + j * hq
            win_end = row0 + hq - m * win
            q_pos = row0 + lax.broadcasted_iota(jnp.int32, (hq, 1), 0)
            if mode == "inner":
                start = pl.multiple_of(win_end - win, hq)
                k_pos = (win_end - LANES) + lax.broadcasted_iota(jnp.int32, (1, LANES), 1)
                valids.append(k_pos < q_pos)
            else:
                start = pl.multiple_of(jnp.maximum(win_end - win, 0), hq)
                k_pos = start + lax.broadcasted_iota(jnp.int32, (1, win), 1)
                valids.append(k_pos < (q_pos if mode == "edge" else win_end))
            k_ws.append(k_ref[pl.ds(start, win), :])
            v_ws.append(v_ref[pl.ds(start, win), :])

        def mask(x, j):
            if mode == "inner":
                return jnp.concatenate(
                    [x[:, :win - LANES],
                     jnp.where(valids[j], x[:, win - LANES:], MASKED_LOGIT)], axis=1)
            return jnp.where(valids[j], x, MASKED_LOGIT)

        z_parts = []
        for j in subs:
            rows = slice(j * hq, (j + 1) * hq)
            for p in pairs:
                q_p = q_ref[rows, pair_cols[p]]
                zero = jnp.zeros_like(q_p)
                q_2 = jnp.concatenate([jnp.where(first_head, q_p, zero),
                                       jnp.where(first_head, zero, q_p)], axis=0)
                z_2 = lax.dot_general(q_2, k_ws[j][:, pair_cols[p]], (((1,), (1,)), ((), ())),
                                      preferred_element_type=F32)
                z_parts += [mask(z_2[:hq], j), mask(z_2[hq:], j)]
        z = jnp.concatenate(z_parts, axis=0)
        s = softplus2(z)
        sums = jnp.dot(s.astype(BF16), newer, preferred_element_type=F32)
        total = jnp.broadcast_to(jnp.sum(s, axis=1, keepdims=True), (s.shape[0], LANES))
        arg = (z - s) - sums
        if mode == "older":
            car = car_ref[...]
            arg = arg - jnp.concatenate([car] * (win // LANES), axis=1)
            total = car + total
        car_ref[...] = total
        w = jnp.exp2(arg).astype(BF16)

        i = 0
        for j in subs:
            rows = slice(j * hq, (j + 1) * hq)
            for p in pairs:
                o_2 = jnp.dot(w[i * hq:(i + 2) * hq], v_ws[j][:, pair_cols[p]],
                              preferred_element_type=F32)
                i += 2
                o_p = jnp.where(first_head, o_2[:hq], o_2[hq:])
                if mode == "older":
                    acc_ref[rows, pair_cols[p]] += o_p
                else:
                    acc_ref[rows, pair_cols[p]] = o_p
        return (jnp.min(total) >= SURVIVAL_UNDERFLOW_LOG2).astype(jnp.int32)

    done = lax.cond(q_start + hq >= win, lambda: visit(0, "inner"), lambda: visit(0, "edge"))

    def cond(carry):
        m, done = carry
        return jnp.logical_and(q_start + tq - m * win > 0, done == 0)

    def body(carry):
        m, _ = carry
        return m + 1, visit(m, "older")

    lax.while_loop(cond, body, (jnp.int32(1), done))
    o_ref[...] = acc_ref[...].astype(BF16)


def _attn(q_arr, q_col, k_arr, k_col, v_arr, v_col, *, batch, q_len, k_len, pos0, tq, hq):
    assert tq % hq == 0 and q_len % tq == 0 and pos0 % tq == 0 and ATTN_WINDOW % hq == 0
    assert pos0 + q_len <= k_len and k_len >= ATTN_WINDOW and hq % BF16_SUBLANES == 0
    nq = q_len // tq
    stacked_rows = (tq // hq) * SB_HEADS * hq
    return pl.pallas_call(
        functools.partial(_attn_kernel, tq=tq, hq=hq, pos0=pos0),
        grid=(batch, nq),
        in_specs=[
            pl.BlockSpec((tq, D_SB), lambda b, i: (b * nq + i, q_col)),
            pl.BlockSpec((k_len, D_SB), lambda b, i: (b, k_col)),
            pl.BlockSpec((k_len, D_SB), lambda b, i: (b, v_col)),
        ],
        out_specs=pl.BlockSpec((tq, D_SB), lambda b, i: (b * nq + i, 0)),
        out_shape=jax.ShapeDtypeStruct((batch * q_len, D_SB), BF16),
        scratch_shapes=[pltpu.VMEM((tq, D_SB), F32), pltpu.VMEM((stacked_rows, LANES), F32)],
        compiler_params=pltpu.CompilerParams(
            dimension_semantics=("parallel", "arbitrary"), vmem_limit_bytes=VMEM_LIMIT_BYTES),
        name="attn",
    )(q_arr, k_arr, v_arr)


def _merge_kernel(x_ref, u_ref, halo_ref, o_ref, gate_ref, pw_ref, ps_ref, wbp_ref, wbs_ref,
                  wout_ref, y_ref, *, tm, pos0, tiles_per_seq):
    streams = halo_ref.shape[0]
    rows = tm // streams
    if streams == 1:
        first_pos = pos0 + (pl.program_id(0) % tiles_per_seq) * tm
        pos = first_pos + lax.broadcasted_iota(jnp.int32, (tm, 1), 0)
    else:
        pos = jnp.concatenate([pos0 + lax.broadcasted_iota(jnp.int32, (rows, 1), 0)] * streams,
                              axis=0)

    n_groups = len(POOL_WINDOWS)
    n_chunks = n_groups if tm % (BF16_SUBLANES * n_groups) == 0 else 1
    quarter = tm // n_chunks
    span = POOL_HALO + rows
    run = jnp.concatenate(
        [part for b in range(streams)
         for part in (halo_ref[b], u_ref[b * rows:(b + 1) * rows, :])], axis=0)
    width, mixed, gated_sb = 1, [], []
    for gi, win in enumerate(POOL_WINDOWS):
        while width < win:
            run = run + pltpu.roll(run, width, 0)
            width *= 2
        assert width == win, "pool windows must be ascending powers of two"
        cols = slice(gi * POOL_GROUP, (gi + 1) * POOL_GROUP)
        cnt = jnp.minimum(win, pos + 1).astype(F32)
        tot = jnp.concatenate([run[b * span + POOL_HALO:(b + 1) * span, 0:POOL_GROUP]
                               for b in range(streams)], axis=0)
        d = (tot / cnt - u_ref[:, cols]).astype(BF16)
        mixed.append(jnp.dot(d, pw_ref[gi], preferred_element_type=F32) * ps_ref[:, cols])
        if gi + 1 < n_groups:
            run = run[:, POOL_GROUP:]
        if gi < n_chunks:
            rows = slice(gi * quarter, (gi + 1) * quarter)
            br_sb = jnp.dot(o_ref[rows, :], wbs_ref[...], preferred_element_type=F32)
            gated_sb.append(
                jax.nn.sigmoid(gate_ref[rows, D_MODEL:2 * D_MODEL].astype(F32)) * br_sb)
    a = jnp.concatenate(mixed, axis=1).astype(BF16)

    br_pool = jnp.dot(a, wbp_ref[...], preferred_element_type=F32)
    merged = (jax.nn.sigmoid(gate_ref[:, 0:D_MODEL].astype(F32)) * br_pool
              + jnp.concatenate(gated_sb, axis=0))
    y_ref[...] = x_ref[...] + jnp.dot(merged.astype(BF16), wout_ref[...],
                                      preferred_element_type=F32)


def _merge(x, u, halo, o, gate, pool_w, pool_scale, w_bp, w_bs, w_out, *, tm, pos0, seq):
    n = x.shape[0]
    const2 = lambda i: (0, 0)
    resident = pl.Buffered(1)
    streams = max(tm // seq, 1)
    assert tm % seq == 0 or seq % tm == 0
    return pl.pallas_call(
        functools.partial(_merge_kernel, tm=tm, pos0=pos0, tiles_per_seq=max(seq // tm, 1)),
        grid=(n // tm,),
        in_specs=[
            pl.BlockSpec((tm, D_MODEL), lambda i: (i, 0)),
            pl.BlockSpec((tm, D_POOL), lambda i: (i, 0)),
            pl.BlockSpec((streams, POOL_HALO, D_POOL), lambda i: (i, 0, 0)),
            pl.BlockSpec((tm, D_SB), lambda i: (i, 0)),
            pl.BlockSpec((tm, 2 * D_MODEL), lambda i: (i, 0)),
            pl.BlockSpec((len(POOL_WINDOWS), POOL_GROUP, POOL_GROUP), lambda i: (0, 0, 0)),
            pl.BlockSpec((1, D_POOL), const2),
            pl.BlockSpec((D_POOL, D_MODEL), const2, pipeline_mode=resident),
            pl.BlockSpec((D_SB, D_MODEL), const2, pipeline_mode=resident),
            pl.BlockSpec((D_MODEL, D_MODEL), const2, pipeline_mode=resident),
        ],
        out_specs=pl.BlockSpec((tm, D_MODEL), lambda i: (i, 0)),
        out_shape=jax.ShapeDtypeStruct((n, D_MODEL), F32),
        compiler_params=pltpu.CompilerParams(
            dimension_semantics=("parallel",), vmem_limit_bytes=VMEM_LIMIT_BYTES),
        name="merge",
    )(x, u, halo, o, gate, pool_w, pool_scale, w_bp, w_bs, w_out)


def _layer(x, w, *, batch, seq, pos0, k_past, v_past, pool_prefix, tm_ffn, tm_mix, tm_merge, tq,
           hq):
    assert seq >= POOL_STATE
    w_bf16 = {}

    def ffn(x_in, prefix, final_g=None):
        names = [prefix + "_gate", prefix + "_up", prefix + "_down"]
        res = _ffn(x_in, w[prefix + "_norm"], *[w[nm] for nm in names], final_g,
                   tm=tm_ffn, tf=FFN_CHUNK)
        if isinstance(res, (list, tuple)):
            w_bf16.update(zip(names, res[1:]))
            return res[0]
        return res

    x1 = ffn(x, "ffn1")
    u, qkv, k_hm, v_hm, gate = _proj(x1, w["mix_norm"], w["w_in"], batch=batch, seq=seq, tm=tm_mix)

    if k_past is None:
        o = _attn(qkv, 0, qkv, 1, qkv, 2, batch=batch, q_len=seq, k_len=seq, pos0=0, tq=tq, hq=hq)
    else:
        def with_past(past, col):
            new = qkv[:, col * D_SB:(col + 1) * D_SB].reshape(batch, seq, D_SB)
            return jnp.concatenate([past, new], axis=1).reshape(batch * (pos0 + seq), D_SB)
        o = _attn(qkv, 0, with_past(k_past, 1), 0, with_past(v_past, 2), 0,
                  batch=batch, q_len=seq, k_len=pos0 + seq, pos0=pos0, tq=tq, hq=hq)

    piece = min(tm_merge, seq)
    pieces = seq // piece
    u4 = u.reshape(batch, pieces, piece, D_POOL)
    first = jnp.pad(pool_prefix, ((0, 0), (1, 0), (0, 0)))[:, None]
    halo = jnp.concatenate([first, u4[:, :-1, piece - POOL_HALO:, :]], axis=1)
    halo = halo.reshape(batch * pieces, POOL_HALO, D_POOL)

    x2 = _merge(x1, u, halo, o, gate, w["pool_w"], w["pool_scale"], w["w_branch_pool"],
                w["w_branch_sb"], w["w_out"], tm=tm_merge, pos0=pos0, seq=seq)
    y = ffn(x2, "ffn2", w["final_norm"])
    new_pool = u.reshape(batch, seq, D_POOL)[:, seq - POOL_STATE:, :]
    return (y.reshape(batch, seq, D_MODEL), k_hm, v_hm, new_pool), w_bf16


def kernel(x_prompt, x_sample, cache_k, cache_v, state_pool, ffn1_norm, ffn1_gate, ffn1_up, ffn1_down, mix_norm, w_in, pool_w, pool_scale, w_branch_pool, w_branch_sb, w_out, ffn2_norm, ffn2_gate, ffn2_up, ffn2_down, final_norm):
    assert ffn1_norm.shape[0] == 1, "single-layer kernel"
    mixer_mats = dict(w_in=w_in, pool_w=pool_w, w_branch_pool=w_branch_pool,
                      w_branch_sb=w_branch_sb, w_out=w_out)
    w = {name: m[0].astype(BF16) for name, m in mixer_mats.items()}
    w.update(ffn1_gate=ffn1_gate[0], ffn1_up=ffn1_up[0], ffn1_down=ffn1_down[0],
             ffn2_gate=ffn2_gate[0], ffn2_up=ffn2_up[0], ffn2_down=ffn2_down[0])
    w.update(ffn1_norm=ffn1_norm, mix_norm=mix_norm, pool_scale=pool_scale, ffn2_norm=ffn2_norm,
             final_norm=final_norm.reshape(1, D_MODEL))

    b_p, s_p, _ = x_prompt.shape
    b_d, s_d, _ = x_sample.shape
    past = cache_k.shape[3]

    def token_major(cache):
        return cache[0].transpose(0, 2, 1, 3).reshape(b_d, past, D_SB).astype(BF16)

    (y_d, k_d, v_d, pool_d), ffn_bf16 = _layer(
        x_sample.reshape(b_d * s_d, D_MODEL), w, batch=b_d, seq=s_d, pos0=past,
        k_past=token_major(cache_k), v_past=token_major(cache_v), pool_prefix=state_pool[0],
        tm_ffn=b_d * s_d, tm_mix=b_d * s_d, tm_merge=b_d * s_d, tq=s_d, hq=s_d)

    (y_p, k_p, v_p, pool_p), _ = _layer(
        x_prompt.reshape(b_p * s_p, D_MODEL), {**w, **ffn_bf16}, batch=b_p, seq=s_p, pos0=0,
        k_past=None, v_past=None, pool_prefix=jnp.zeros((b_p, POOL_STATE, D_POOL), F32),
        tm_ffn=1024, tm_mix=512, tm_merge=1024, tq=512, hq=64)

    return (y_p, y_d, k_p[None], v_p[None], pool_p[None], k_d[None], v_d[None], pool_d[None])
```

```python
import functools

import jax
import jax.numpy as jnp
from jax import lax
from jax.experimental import pallas as pl
from jax.experimental.pallas import tpu as pltpu

F32 = jnp.float32
BF16 = jnp.bfloat16

D_MODEL = 1024
D_FF = 4 * D_MODEL
D_POOL = D_MODEL // 2
POOL_WINDOWS = (2, 4, 8, 16)
POOL_GROUP = D_POOL // len(POOL_WINDOWS)
POOL_STATE = max(POOL_WINDOWS) - 1
POOL_HALO = POOL_STATE + 1
SB_HEADS = 8
SB_HEAD_DIM = 64
D_SB = SB_HEADS * SB_HEAD_DIM
D_IN = D_POOL + 3 * D_SB + 2 * D_MODEL
RMS_EPS = 1e-6
LANES = 128
BF16_SUBLANES = 16
VMEM_LIMIT_BYTES = 56 * 1024 * 1024
FFN_CHUNK = 1024
ATTN_WINDOW = 2 * LANES

LOG2E = 1.4426950408889634
SURVIVAL_UNDERFLOW_LOG2 = 150.0
MASKED_LOGIT = -1e30


def _rmsnorm(x, g):
    ms = jnp.mean(x * x, axis=-1, keepdims=True)
    return (x * lax.rsqrt(ms + RMS_EPS)) * g


def _ffn_kernel(*refs, n_ff, final, cast):
    refs = list(refs)
    x_ref, xnext_ref, g_ref, wg_ref, wu_ref, wd_ref = refs[:6]
    del refs[:6]
    fg_ref = refs.pop(0) if final else None
    o_ref = refs.pop(0)
    if cast:
        wg_out, wu_out, wd_out = refs[:3]
        del refs[:3]
    xn_ref, acc_ref = refs
    i = pl.program_id(0)
    j = pl.program_id(1)
    slot = i % 2

    @pl.when(jnp.logical_and(i == 0, j == 0))
    def _():
        xn_ref[0] = _rmsnorm(x_ref[...], g_ref[...]).astype(BF16)

    def chunk():
        wg, wu, wd = wg_ref[...], wu_ref[...], wd_ref[...]
        if cast:
            wg, wu, wd = wg.astype(BF16), wu.astype(BF16), wd.astype(BF16)
            wg_out[...] = wg
            wu_out[...] = wu
            wd_out[...] = wd
        xn = xn_ref[slot]
        gate = jnp.dot(xn, wg, preferred_element_type=F32)
        up = jnp.dot(xn, wu, preferred_element_type=F32)
        h = (gate * jax.nn.sigmoid(gate) * up).astype(BF16)
        return jnp.dot(h, wd, preferred_element_type=F32)

    def prepare_next(rows):
        xn_ref[1 - slot, rows, :] = _rmsnorm(xnext_ref[rows, :], g_ref[...]).astype(BF16)

    piece = x_ref.shape[0] // n_ff

    def first():
        acc_ref[...] = chunk()
        prepare_next(pl.ds(0, piece))

    def middle():
        acc_ref[...] += chunk()
        prepare_next(pl.ds(pl.multiple_of(j * piece, piece), piece))

    def last():
        y = x_ref[...] + 0.5 * (acc_ref[...] + chunk())
        if final:
            y = _rmsnorm(y, fg_ref[...])
        o_ref[...] = y
        prepare_next(pl.ds((n_ff - 1) * piece, piece))

    case = jnp.where(j == 0, 0, jnp.where(j == n_ff - 1, 2, 1))
    lax.switch(case, [first, middle, last])


def _ffn(x, norm_g, wg, wu, wd, final_g=None, *, tm, tf):
    n = x.shape[0]
    n_rows = n // tm
    n_ff = D_FF // tf
    assert n_ff >= 2 and tm % (BF16_SUBLANES * n_ff) == 0
    final = final_g is not None
    cast = wg.dtype == F32
    assert not cast or n_rows == 1
    row = pl.BlockSpec((1, D_MODEL), lambda i, j: (0, 0))
    w_specs = [
        pl.BlockSpec((D_MODEL, tf), lambda i, j: (0, j)),
        pl.BlockSpec((D_MODEL, tf), lambda i, j: (0, j)),
        pl.BlockSpec((tf, D_MODEL), lambda i, j: (j, 0)),
    ]
    in_specs = [
        pl.BlockSpec((tm, D_MODEL), lambda i, j: (i, 0)),
        pl.BlockSpec((tm, D_MODEL), lambda i, j: (jnp.minimum(i + 1, n_rows - 1), 0)),
        row,
    ] + w_specs
    args = [x, x, norm_g, wg, wu, wd]
    if final:
        in_specs.append(row)
        args.append(final_g)
    out_specs = [pl.BlockSpec((tm, D_MODEL), lambda i, j: (i, 0))]
    out_shape = [jax.ShapeDtypeStruct((n, D_MODEL), F32)]
    if cast:
        out_specs += w_specs
        out_shape += [jax.ShapeDtypeStruct(m.shape, BF16) for m in (wg, wu, wd)]
    outs = pl.pallas_call(
        functools.partial(_ffn_kernel, n_ff=n_ff, final=final, cast=cast),
        grid=(n_rows, n_ff),
        in_specs=in_specs,
        out_specs=out_specs,
        out_shape=out_shape,
        scratch_shapes=[pltpu.VMEM((2, tm, D_MODEL), BF16), pltpu.VMEM((tm, D_MODEL), F32)],
        compiler_params=pltpu.CompilerParams(
            dimension_semantics=("arbitrary", "arbitrary"), vmem_limit_bytes=VMEM_LIMIT_BYTES),
        name=("ffn_final" if final else "ffn") + ("_cast" if cast else ""),
    )(*args)
    return outs if cast else outs[0]


def _proj_kernel(x_ref, g_ref, w_ref, u_ref, qkv_ref, k_ref, v_ref, gate_ref):
    h = _rmsnorm(x_ref[...], g_ref[...]).astype(BF16)

    def mm(c0, c1):
        return jnp.dot(h, w_ref[:, c0:c1], preferred_element_type=F32)

    c = 0
    u_ref[...] = mm(c, c + D_POOL)
    c += D_POOL
    qkv_ref[:, 0:D_SB] = (mm(c, c + D_SB) * (LOG2E * SB_HEAD_DIM ** -0.5)).astype(BF16)
    c += D_SB
    for idx, hm_ref in ((1, k_ref), (2, v_ref)):
        t = mm(c, c + D_SB)
        c += D_SB
        qkv_ref[:, idx * D_SB:(idx + 1) * D_SB] = t.astype(BF16)
        streams, _, rows, _ = hm_ref.shape
        for b in range(streams):
            for hd in range(SB_HEADS):
                hm_ref[b, hd, :, :] = t[b * rows:(b + 1) * rows,
                                        hd * SB_HEAD_DIM:(hd + 1) * SB_HEAD_DIM]
    gate_ref[:, 0:D_MODEL] = mm(c, c + D_MODEL).astype(BF16)
    c += D_MODEL
    gate_ref[:, D_MODEL:2 * D_MODEL] = mm(c, c + D_MODEL).astype(BF16)


def _proj(x, norm_g, w_in, *, batch, seq, tm):
    n = x.shape[0]
    if tm <= seq:
        tiles_per_seq = seq // tm
        hm_spec = pl.BlockSpec((1, SB_HEADS, tm, SB_HEAD_DIM),
                               lambda i: (i // tiles_per_seq, 0, i % tiles_per_seq, 0))
    else:
        assert tm % seq == 0
        hm_spec = pl.BlockSpec((tm // seq, SB_HEADS, seq, SB_HEAD_DIM), lambda i: (i, 0, 0, 0))
    hm_shape = jax.ShapeDtypeStruct((batch, SB_HEADS, seq, SB_HEAD_DIM), F32)
    return pl.pallas_call(
        _proj_kernel,
        grid=(n // tm,),
        in_specs=[
            pl.BlockSpec((tm, D_MODEL), lambda i: (i, 0)),
            pl.BlockSpec((1, D_MODEL), lambda i: (0, 0)),
            pl.BlockSpec((D_MODEL, D_IN), lambda i: (0, 0)),
        ],
        out_specs=[
            pl.BlockSpec((tm, D_POOL), lambda i: (i, 0)),
            pl.BlockSpec((tm, 3 * D_SB), lambda i: (i, 0)),
            hm_spec,
            hm_spec,
            pl.BlockSpec((tm, 2 * D_MODEL), lambda i: (i, 0)),
        ],
        out_shape=[
            jax.ShapeDtypeStruct((n, D_POOL), F32),
            jax.ShapeDtypeStruct((n, 3 * D_SB), BF16),
            hm_shape,
            hm_shape,
            jax.ShapeDtypeStruct((n, 2 * D_MODEL), BF16),
        ],
        compiler_params=pltpu.CompilerParams(
            dimension_semantics=("parallel",), vmem_limit_bytes=VMEM_LIMIT_BYTES),
        name="proj",
    )(x, norm_g, w_in)


def _attn_kernel(*refs, tq, hq, pos0, past_heads):
    win = ATTN_WINDOW
    if past_heads:
        q_ref, k_ref, v_ref, k_new_ref, v_new_ref, o_ref, acc_ref, car_ref = refs
    else:
        q_ref, k_ref, v_ref, o_ref, acc_ref, car_ref = refs
        k_new_ref = v_new_ref = None
    q_start = pos0 + pl.program_id(1) * tq
    subs = range(tq // hq)
    pairs = range(SB_HEADS // 2)
    pair_cols = [slice(p * LANES, (p + 1) * LANES) for p in pairs]

    lane = lax.broadcasted_iota(jnp.int32, (1, LANES), 1)
    first_head = lane < SB_HEAD_DIM
    r = lax.broadcasted_iota(jnp.int32, (win, win), 0)
    c = lax.broadcasted_iota(jnp.int32, (win, win), 1)
    newer = jnp.where(r > c, 1.0, 0.0).astype(BF16)

    def softplus2(z):
        return jnp.maximum(z, 0.0) + jnp.log2(1.0 + jnp.exp2(-jnp.abs(z)))

    def window(ref, new_ref, start, mode):
        if not past_heads:
            return ref[pl.ds(start, win), :]

        def past_rows(first, count):
            return jnp.concatenate([ref[0, hd, pl.ds(first, count), :] for hd in range(SB_HEADS)],
                                   axis=1).astype(BF16)

        if mode == "inner":
            return jnp.concatenate([past_rows(pos0 - (win - hq), win - hq), new_ref[...]], axis=0)
        return past_rows(start, win)

    def visit(m, mode):
        k_ws, v_ws, valids = [], [], []
        for j in subs:
            row0 = q_start + j * hq
            win_end = row0 + hq - m * win
            q_pos = row0 + lax.broadcasted_iota(jnp.int32, (hq, 1), 0)
            if mode == "inner":
                start = pl.multiple_of(win_end - win, hq)
                k_pos = (win_end - LANES) + lax.broadcasted_iota(jnp.int32, (1, LANES), 1)
                valids.append(k_pos < q_pos)
            else:
                start = pl.multiple_of(jnp.maximum(win_end - win, 0), hq)
                k_pos = start + lax.broadcasted_iota(jnp.int32, (1, win), 1)
                valids.append(k_pos < (q_pos if mode == "edge" else win_end))
            k_ws.append(window(k_ref, k_new_ref, start, mode))
            v_ws.append(window(v_ref, v_new_ref, start, mode))

        def mask(x, j):
            if mode == "inner":
                return jnp.concatenate(
                    [x[:, :win - LANES],
                     jnp.where(valids[j], x[:, win - LANES:], MASKED_LOGIT)], axis=1)
            return jnp.where(valids[j], x, MASKED_LOGIT)

        z_parts = []
        for j in subs:
            rows = slice(j * hq, (j + 1) * hq)
            for p in pairs:
                q_p = q_ref[rows, pair_cols[p]]
                zero = jnp.zeros_like(q_p)
                q_2 = jnp.concatenate([jnp.where(first_head, q_p, zero),
                                       jnp.where(first_head, zero, q_p)], axis=0)
                z_2 = lax.dot_general(q_2, k_ws[j][:, pair_cols[p]], (((1,), (1,)), ((), ())),
                                      preferred_element_type=F32)
                z_parts += [mask(z_2[:hq], j), mask(z_2[hq:], j)]
        z = jnp.concatenate(z_parts, axis=0)
        s = softplus2(z)
        sums = jnp.dot(s.astype(BF16), newer, preferred_element_type=F32)
        total = jnp.broadcast_to(jnp.sum(s, axis=1, keepdims=True), (s.shape[0], LANES))
        arg = (z - s) - sums
        if mode == "older":
            car = car_ref[...]
            arg = arg - jnp.concatenate([car] * (win // LANES), axis=1)
            total = car + total
        car_ref[...] = total
        w = jnp.exp2(arg).astype(BF16)

        i = 0
        for j in subs:
            rows = slice(j * hq, (j + 1) * hq)
            for p in pairs:
                o_2 = jnp.dot(w[i * hq:(i + 2) * hq], v_ws[j][:, pair_cols[p]],
                              preferred_element_type=F32)
                i += 2
                o_p = jnp.where(first_head, o_2[:hq], o_2[hq:])
                if mode == "older":
                    acc_ref[rows, pair_cols[p]] += o_p
                else:
                    acc_ref[rows, pair_cols[p]] = o_p
        return (jnp.min(total) >= SURVIVAL_UNDERFLOW_LOG2).astype(jnp.int32)

    if past_heads:
        done = visit(0, "inner")
    else:
        done = lax.cond(q_start + hq >= win, lambda: visit(0, "inner"), lambda: visit(0, "edge"))

    def cond(carry):
        m, done = carry
        return jnp.logical_and(q_start + tq - m * win > 0, done == 0)

    def body(carry):
        m, _ = carry
        return m + 1, visit(m, "older")

    lax.while_loop(cond, body, (jnp.int32(1), done))
    o_ref[...] = acc_ref[...].astype(BF16)


def _attn(qkv, k_past=None, v_past=None, *, batch, q_len, pos0, tq, hq):
    assert tq % hq == 0 and q_len % tq == 0 and pos0 % tq == 0 and ATTN_WINDOW % hq == 0
    assert pos0 + q_len >= ATTN_WINDOW and hq % BF16_SUBLANES == 0
    nq = q_len // tq
    stacked_rows = (tq // hq) * SB_HEADS * hq
    q_spec = pl.BlockSpec((tq, D_SB), lambda b, i: (b * nq + i, 0))
    if k_past is None:
        assert pos0 == 0
        in_specs = [q_spec,
                    pl.BlockSpec((q_len, D_SB), lambda b, i: (b, 1)),
                    pl.BlockSpec((q_len, D_SB), lambda b, i: (b, 2))]
        args = (qkv, qkv, qkv)
    else:
        assert tq == hq == q_len and pos0 >= ATTN_WINDOW and pos0 % 8 == 0
        past_spec = pl.BlockSpec((1, SB_HEADS, pos0, SB_HEAD_DIM), lambda b, i: (b, 0, 0, 0))
        in_specs = [q_spec, past_spec, past_spec,
                    pl.BlockSpec((hq, D_SB), lambda b, i: (b, 1)),
                    pl.BlockSpec((hq, D_SB), lambda b, i: (b, 2))]
        args = (qkv, k_past, v_past, qkv, qkv)
    return pl.pallas_call(
        functools.partial(_attn_kernel, tq=tq, hq=hq, pos0=pos0, past_heads=k_past is not None),
        grid=(batch, nq),
        in_specs=in_specs,
        out_specs=pl.BlockSpec((tq, D_SB), lambda b, i: (b * nq + i, 0)),
        out_shape=jax.ShapeDtypeStruct((batch * q_len, D_SB), BF16),
        scratch_shapes=[pltpu.VMEM((tq, D_SB), F32), pltpu.VMEM((stacked_rows, LANES), F32)],
        compiler_params=pltpu.CompilerParams(
            dimension_semantics=("parallel", "arbitrary"), vmem_limit_bytes=VMEM_LIMIT_BYTES),
        name="attn",
    )(*args)


def _merge_kernel(x_ref, u_ref, halo_ref, o_ref, gate_ref, pw_ref, ps_ref, wbp_ref, wbs_ref,
                  wout_ref, y_ref, *, tm, pos0, tiles_per_seq):
    streams = halo_ref.shape[0]
    rows = tm // streams
    if streams == 1:
        first_pos = pos0 + (pl.program_id(0) % tiles_per_seq) * tm
        pos = first_pos + lax.broadcasted_iota(jnp.int32, (tm, 1), 0)
    else:
        pos = jnp.concatenate([pos0 + lax.broadcasted_iota(jnp.int32, (rows, 1), 0)] * streams,
                              axis=0)

    n_groups = len(POOL_WINDOWS)
    n_chunks = n_groups if tm % (BF16_SUBLANES * n_groups) == 0 else 1
    quarter = tm // n_chunks
    span = POOL_HALO + rows
    run = jnp.concatenate(
        [part for b in range(streams)
         for part in (halo_ref[b], u_ref[b * rows:(b + 1) * rows, :])], axis=0)
    width, mixed, gated_sb = 1, [], []
    for gi, win in enumerate(POOL_WINDOWS):
        while width < win:
            run = run + pltpu.roll(run, width, 0)
            width *= 2
        assert width == win, "pool windows must be ascending powers of two"
        cols = slice(gi * POOL_GROUP, (gi + 1) * POOL_GROUP)
        cnt = jnp.minimum(win, pos + 1).astype(F32)
        tot = jnp.concatenate([run[b * span + POOL_HALO:(b + 1) * span, 0:POOL_GROUP]
                               for b in range(streams)], axis=0)
        d = (tot / cnt - u_ref[:, cols]).astype(BF16)
        mixed.append(jnp.dot(d, pw_ref[gi], preferred_element_type=F32) * ps_ref[:, cols])
        if gi + 1 < n_groups:
            run = run[:, POOL_GROUP:]
        if gi < n_chunks:
            rows = slice(gi * quarter, (gi + 1) * quarter)
            br_sb = jnp.dot(o_ref[rows, :], wbs_ref[...], preferred_element_type=F32)
            gated_sb.append(
                jax.nn.sigmoid(gate_ref[rows, D_MODEL:2 * D_MODEL].astype(F32)) * br_sb)
    a = jnp.concatenate(mixed, axis=1).astype(BF16)

    br_pool = jnp.dot(a, wbp_ref[...], preferred_element_type=F32)
    merged = (jax.nn.sigmoid(gate_ref[:, 0:D_MODEL].astype(F32)) * br_pool
              + jnp.concatenate(gated_sb, axis=0))
    y_ref[...] = x_ref[...] + jnp.dot(merged.astype(BF16), wout_ref[...],
                                      preferred_element_type=F32)


def _merge(x, u, halo, o, gate, pool_w, pool_scale, w_bp, w_bs, w_out, *, tm, pos0, seq):
    n = x.shape[0]
    const2 = lambda i: (0, 0)
    resident = pl.Buffered(1)
    streams = max(tm // seq, 1)
    assert tm % seq == 0 or seq % tm == 0
    return pl.pallas_call(
        functools.partial(_merge_kernel, tm=tm, pos0=pos0, tiles_per_seq=max(seq // tm, 1)),
        grid=(n // tm,),
        in_specs=[
            pl.BlockSpec((tm, D_MODEL), lambda i: (i, 0)),
            pl.BlockSpec((tm, D_POOL), lambda i: (i, 0)),
            pl.BlockSpec((streams, POOL_HALO, D_POOL), lambda i: (i, 0, 0)),
            pl.BlockSpec((tm, D_SB), lambda i: (i, 0)),
            pl.BlockSpec((tm, 2 * D_MODEL), lambda i: (i, 0)),
            pl.BlockSpec((len(POOL_WINDOWS), POOL_GROUP, POOL_GROUP), lambda i: (0, 0, 0)),
            pl.BlockSpec((1, D_POOL), const2),
            pl.BlockSpec((D_POOL, D_MODEL), const2, pipeline_mode=resident),
            pl.BlockSpec((D_SB, D_MODEL), const2, pipeline_mode=resident),
            pl.BlockSpec((D_MODEL, D_MODEL), const2, pipeline_mode=resident),
        ],
        out_specs=pl.BlockSpec((tm, D_MODEL), lambda i: (i, 0)),
        out_shape=jax.ShapeDtypeStruct((n, D_MODEL), F32),
        compiler_params=pltpu.CompilerParams(
            dimension_semantics=("parallel",), vmem_limit_bytes=VMEM_LIMIT_BYTES),
        name="merge",
    )(x, u, halo, o, gate, pool_w, pool_scale, w_bp, w_bs, w_out)


def _layer(x, w, *, batch, seq, pos0, k_past, v_past, pool_prefix, tm_ffn, tm_mix, tm_merge, tq,
           hq):
    assert seq >= POOL_STATE
    w_bf16 = {}

    def ffn(x_in, prefix, final_g=None):
        names = [prefix + "_gate", prefix + "_up", prefix + "_down"]
        res = _ffn(x_in, w[prefix + "_norm"], *[w[nm] for nm in names], final_g,
                   tm=tm_ffn, tf=FFN_CHUNK)
        if isinstance(res, (list, tuple)):
            w_bf16.update(zip(names, res[1:]))
            return res[0]
        return res

    x1 = ffn(x, "ffn1")
    u, qkv, k_hm, v_hm, gate = _proj(x1, w["mix_norm"], w["w_in"], batch=batch, seq=seq, tm=tm_mix)

    o = _attn(qkv, k_past, v_past, batch=batch, q_len=seq, pos0=pos0, tq=tq, hq=hq)

    piece = min(tm_merge, seq)
    pieces = seq // piece
    u4 = u.reshape(batch, pieces, piece, D_POOL)
    first = jnp.pad(pool_prefix, ((0, 0), (1, 0), (0, 0)))[:, None]
    halo = jnp.concatenate([first, u4[:, :-1, piece - POOL_HALO:, :]], axis=1)
    halo = halo.reshape(batch * pieces, POOL_HALO, D_POOL)

    x2 = _merge(x1, u, halo, o, gate, w["pool_w"], w["pool_scale"], w["w_branch_pool"],
                w["w_branch_sb"], w["w_out"], tm=tm_merge, pos0=pos0, seq=seq)
    y = ffn(x2, "ffn2", w["final_norm"])
    new_pool = u.reshape(batch, seq, D_POOL)[:, seq - POOL_STATE:, :]
    return (y.reshape(batch, seq, D_MODEL), k_hm, v_hm, new_pool), w_bf16


def kernel(x_prompt, x_sample, cache_k, cache_v, state_pool, ffn1_norm, ffn1_gate, ffn1_up, ffn1_down, mix_norm, w_in, pool_w, pool_scale, w_branch_pool, w_branch_sb, w_out, ffn2_norm, ffn2_gate, ffn2_up, ffn2_down, final_norm):
    assert ffn1_norm.shape[0] == 1, "single-layer kernel"
    mixer_mats = dict(w_in=w_in, pool_w=pool_w, w_branch_pool=w_branch_pool,
                      w_branch_sb=w_branch_sb, w_out=w_out)
    w = {name: m[0].astype(BF16) for name, m in mixer_mats.items()}
    w.update(ffn1_gate=ffn1_gate[0], ffn1_up=ffn1_up[0], ffn1_down=ffn1_down[0],
             ffn2_gate=ffn2_gate[0], ffn2_up=ffn2_up[0], ffn2_down=ffn2_down[0])
    w.update(ffn1_norm=ffn1_norm, mix_norm=mix_norm, pool_scale=pool_scale, ffn2_norm=ffn2_norm,
             final_norm=final_norm.reshape(1, D_MODEL))

    b_p, s_p, _ = x_prompt.shape
    b_d, s_d, _ = x_sample.shape
    past = cache_k.shape[3]

    (y_d, k_d, v_d, pool_d), ffn_bf16 = _layer(
        x_sample.reshape(b_d * s_d, D_MODEL), w, batch=b_d, seq=s_d, pos0=past,
        k_past=cache_k[0], v_past=cache_v[0], pool_prefix=state_pool[0],
        tm_ffn=b_d * s_d, tm_mix=b_d * s_d, tm_merge=b_d * s_d, tq=s_d, hq=s_d)

    (y_p, k_p, v_p, pool_p), _ = _layer(
        x_prompt.reshape(b_p * s_p, D_MODEL), {**w, **ffn_bf16}, batch=b_p, seq=s_p, pos0=0,
        k_past=None, v_past=None, pool_prefix=jnp.zeros((b_p, POOL_STATE, D_POOL), F32),
        tm_ffn=1024, tm_mix=512, tm_merge=1024, tq=512, hq=64)

    return (y_p, y_d, k_p[None], v_p[None], pool_p[None], k_d[None], v_d[None], pool_d[None])
```

```python
import functools

import jax
import jax.numpy as jnp
from jax import lax
from jax.experimental import pallas as pl
from jax.experimental.pallas import tpu as pltpu

F32 = jnp.float32
BF16 = jnp.bfloat16

D_MODEL = 1024
D_FF = 4 * D_MODEL
D_POOL = D_MODEL // 2
POOL_WINDOWS = (2, 4, 8, 16)
POOL_GROUP = D_POOL // len(POOL_WINDOWS)
POOL_STATE = max(POOL_WINDOWS) - 1
POOL_HALO = POOL_STATE + 1
SB_HEADS = 8
SB_HEAD_DIM = 64
D_SB = SB_HEADS * SB_HEAD_DIM
D_IN = D_POOL + 3 * D_SB + 2 * D_MODEL
RMS_EPS = 1e-6
LANES = 128
BF16_SUBLANES = 16
VMEM_LIMIT_BYTES = 56 * 1024 * 1024
FFN_CHUNK = 1024
ATTN_WINDOW = 2 * LANES

LOG2E = 1.4426950408889634
SURVIVAL_UNDERFLOW_LOG2 = 150.0
MASKED_LOGIT = -1e30


def _rmsnorm(x, g):
    ms = jnp.mean(x * x, axis=-1, keepdims=True)
    return (x * lax.rsqrt(ms + RMS_EPS)) * g


def _ffn_kernel(*refs, n_ff, final, cast):
    refs = list(refs)
    x_ref, xnext_ref, g_ref, wg_ref, wu_ref, wd_ref = refs[:6]
    del refs[:6]
    fg_ref = refs.pop(0) if final else None
    o_ref = refs.pop(0)
    if cast:
        wg_out, wu_out, wd_out = refs[:3]
        del refs[:3]
    xn_ref, acc_ref = refs
    i = pl.program_id(0)
    j = pl.program_id(1)
    slot = i % 2

    @pl.when(jnp.logical_and(i == 0, j == 0))
    def _():
        xn_ref[0] = _rmsnorm(x_ref[...], g_ref[...]).astype(BF16)

    def chunk():
        wg, wu, wd = wg_ref[...], wu_ref[...], wd_ref[...]
        if cast:
            wg, wu, wd = wg.astype(BF16), wu.astype(BF16), wd.astype(BF16)
            wg_out[...] = wg
            wu_out[...] = wu
            wd_out[...] = wd
        xn = xn_ref[slot]
        gate = jnp.dot(xn, wg, preferred_element_type=F32)
        up = jnp.dot(xn, wu, preferred_element_type=F32)
        h = (gate * jax.nn.sigmoid(gate) * up).astype(BF16)
        return jnp.dot(h, wd, preferred_element_type=F32)

    def prepare_next(rows):
        xn_ref[1 - slot, rows, :] = _rmsnorm(xnext_ref[rows, :], g_ref[...]).astype(BF16)

    piece = x_ref.shape[0] // n_ff

    def first():
        acc_ref[...] = chunk()
        prepare_next(pl.ds(0, piece))

    def middle():
        acc_ref[...] += chunk()
        prepare_next(pl.ds(pl.multiple_of(j * piece, piece), piece))

    def last():
        y = x_ref[...] + 0.5 * (acc_ref[...] + chunk())
        if final:
            y = _rmsnorm(y, fg_ref[...])
        o_ref[...] = y
        prepare_next(pl.ds((n_ff - 1) * piece, piece))

    case = jnp.where(j == 0, 0, jnp.where(j == n_ff - 1, 2, 1))
    lax.switch(case, [first, middle, last])


def _ffn(x, norm_g, wg, wu, wd, final_g=None, *, tm, tf):
    n = x.shape[0]
    n_rows = n // tm
    n_ff = D_FF // tf
    assert n_ff >= 2 and tm % (BF16_SUBLANES * n_ff) == 0
    final = final_g is not None
    cast = wg.dtype == F32
    assert not cast or n_rows == 1
    row = pl.BlockSpec((1, D_MODEL), lambda i, j: (0, 0))
    w_specs = [
        pl.BlockSpec((D_MODEL, tf), lambda i, j: (0, j)),
        pl.BlockSpec((D_MODEL, tf), lambda i, j: (0, j)),
        pl.BlockSpec((tf, D_MODEL), lambda i, j: (j, 0)),
    ]
    in_specs = [
        pl.BlockSpec((tm, D_MODEL), lambda i, j: (i, 0)),
        pl.BlockSpec((tm, D_MODEL), lambda i, j: (jnp.minimum(i + 1, n_rows - 1), 0)),
        row,
    ] + w_specs
    args = [x, x, norm_g, wg, wu, wd]
    if final:
        in_specs.append(row)
        args.append(final_g)
    out_specs = [pl.BlockSpec((tm, D_MODEL), lambda i, j: (i, 0))]
    out_shape = [jax.ShapeDtypeStruct((n, D_MODEL), F32)]
    if cast:
        out_specs += w_specs
        out_shape += [jax.ShapeDtypeStruct(m.shape, BF16) for m in (wg, wu, wd)]
    outs = pl.pallas_call(
        functools.partial(_ffn_kernel, n_ff=n_ff, final=final, cast=cast),
        grid=(n_rows, n_ff),
        in_specs=in_specs,
        out_specs=out_specs,
        out_shape=out_shape,
        scratch_shapes=[pltpu.VMEM((2, tm, D_MODEL), BF16), pltpu.VMEM((tm, D_MODEL), F32)],
        compiler_params=pltpu.CompilerParams(
            dimension_semantics=("arbitrary", "arbitrary"), vmem_limit_bytes=VMEM_LIMIT_BYTES),
        name=("ffn_final" if final else "ffn") + ("_cast" if cast else ""),
    )(*args)
    return outs if cast else outs[0]


def _proj_kernel(x_ref, g_ref, w_ref, u_ref, qkv_ref, k_ref, v_ref, gate_ref, w_out_ref=None):
    h = _rmsnorm(x_ref[...], g_ref[...]).astype(BF16)

    def mm(c0, c1):
        w = w_ref[:, c0:c1]
        if w_out_ref is not None:
            w = w.astype(BF16)
            w_out_ref[:, c0:c1] = w
        return jnp.dot(h, w, preferred_element_type=F32)

    c = 0
    u_ref[...] = mm(c, c + D_POOL)
    c += D_POOL
    qkv_ref[:, 0:D_SB] = (mm(c, c + D_SB) * (LOG2E * SB_HEAD_DIM ** -0.5)).astype(BF16)
    c += D_SB
    for idx, hm_ref in ((1, k_ref), (2, v_ref)):
        t = mm(c, c + D_SB)
        c += D_SB
        qkv_ref[:, idx * D_SB:(idx + 1) * D_SB] = t.astype(BF16)
        streams, _, rows, _ = hm_ref.shape
        for b in range(streams):
            for hd in range(SB_HEADS):
                hm_ref[b, hd, :, :] = t[b * rows:(b + 1) * rows,
                                        hd * SB_HEAD_DIM:(hd + 1) * SB_HEAD_DIM]
    gate_ref[:, 0:D_MODEL] = mm(c, c + D_MODEL).astype(BF16)
    c += D_MODEL
    gate_ref[:, D_MODEL:2 * D_MODEL] = mm(c, c + D_MODEL).astype(BF16)


def _proj(x, norm_g, w_in, *, batch, seq, tm):
    n = x.shape[0]
    cast = w_in.dtype == F32
    assert not cast or n == tm
    if tm <= seq:
        tiles_per_seq = seq // tm
        hm_spec = pl.BlockSpec((1, SB_HEADS, tm, SB_HEAD_DIM),
                               lambda i: (i // tiles_per_seq, 0, i % tiles_per_seq, 0))
    else:
        assert tm % seq == 0
        hm_spec = pl.BlockSpec((tm // seq, SB_HEADS, seq, SB_HEAD_DIM), lambda i: (i, 0, 0, 0))
    hm_shape = jax.ShapeDtypeStruct((batch, SB_HEADS, seq, SB_HEAD_DIM), F32)
    w_spec = pl.BlockSpec((D_MODEL, D_IN), lambda i: (0, 0), pipeline_mode=pl.Buffered(1))
    out_specs = [
        pl.BlockSpec((tm, D_POOL), lambda i: (i, 0)),
        pl.BlockSpec((tm, 3 * D_SB), lambda i: (i, 0)),
        hm_spec,
        hm_spec,
        pl.BlockSpec((tm, 2 * D_MODEL), lambda i: (i, 0)),
    ]
    out_shape = [
        jax.ShapeDtypeStruct((n, D_POOL), F32),
        jax.ShapeDtypeStruct((n, 3 * D_SB), BF16),
        hm_shape,
        hm_shape,
        jax.ShapeDtypeStruct((n, 2 * D_MODEL), BF16),
    ]
    if cast:
        out_specs.append(pl.BlockSpec((D_MODEL, D_IN), lambda i: (0, 0)))
        out_shape.append(jax.ShapeDtypeStruct((D_MODEL, D_IN), BF16))
    return pl.pallas_call(
        _proj_kernel,
        grid=(n // tm,),
        in_specs=[
            pl.BlockSpec((tm, D_MODEL), lambda i: (i, 0)),
            pl.BlockSpec((1, D_MODEL), lambda i: (0, 0)),
            w_spec,
        ],
        out_specs=out_specs,
        out_shape=out_shape,
        compiler_params=pltpu.CompilerParams(
            dimension_semantics=("parallel",), vmem_limit_bytes=VMEM_LIMIT_BYTES),
        name="proj_cast" if cast else "proj",
    )(x, norm_g, w_in)


def _attn_kernel(*refs, tq, hq, pos0, split_new):
    win = ATTN_WINDOW
    if split_new:
        q_ref, k_ref, v_ref, k_new_ref, v_new_ref, o_ref, acc_ref, car_ref = refs
    else:
        q_ref, k_ref, v_ref, o_ref, acc_ref, car_ref = refs
        k_new_ref = v_new_ref = None
    q_start = pos0 + pl.program_id(1) * tq
    subs = range(tq // hq)
    pairs = range(SB_HEADS // 2)
    pair_cols = [slice(p * LANES, (p + 1) * LANES) for p in pairs]

    lane = lax.broadcasted_iota(jnp.int32, (1, LANES), 1)
    first_head = lane < SB_HEAD_DIM
    r = lax.broadcasted_iota(jnp.int32, (win, win), 0)
    c = lax.broadcasted_iota(jnp.int32, (win, win), 1)
    newer = jnp.where(r > c, 1.0, 0.0).astype(BF16)

    def softplus2(z):
        return jnp.maximum(z, 0.0) + jnp.log2(1.0 + jnp.exp2(-jnp.abs(z)))

    def visit(m, mode):
        k_ws, v_ws, valids = [], [], []
        for j in subs:
            row0 = q_start + j * hq
            win_end = row0 + hq - m * win
            q_pos = row0 + lax.broadcasted_iota(jnp.int32, (hq, 1), 0)
            if mode == "inner":
                start = pl.multiple_of(win_end - win, hq)
                k_pos = (win_end - LANES) + lax.broadcasted_iota(jnp.int32, (1, LANES), 1)
                valids.append(k_pos < q_pos)
            else:
                start = pl.multiple_of(jnp.maximum(win_end - win, 0), hq)
                k_pos = start + lax.broadcasted_iota(jnp.int32, (1, win), 1)
                valids.append(k_pos < (q_pos if mode == "edge" else win_end))
            for ref, new_ref, out in ((k_ref, k_new_ref, k_ws), (v_ref, v_new_ref, v_ws)):
                if split_new and mode == "inner":
                    tail = win - hq
                    out.append(jnp.concatenate([ref[pos0 - tail:pos0, :], new_ref[...]], axis=0))
                else:
                    out.append(ref[pl.ds(start, win), :])

        def mask(x, j):
            if mode == "inner":
                return jnp.concatenate(
                    [x[:, :win - LANES],
                     jnp.where(valids[j], x[:, win - LANES:], MASKED_LOGIT)], axis=1)
            return jnp.where(valids[j], x, MASKED_LOGIT)

        z_parts = []
        for j in subs:
            rows = slice(j * hq, (j + 1) * hq)
            for p in pairs:
                q_p = q_ref[rows, pair_cols[p]]
                zero = jnp.zeros_like(q_p)
                q_2 = jnp.concatenate([jnp.where(first_head, q_p, zero),
                                       jnp.where(first_head, zero, q_p)], axis=0)
                z_2 = lax.dot_general(q_2, k_ws[j][:, pair_cols[p]], (((1,), (1,)), ((), ())),
                                      preferred_element_type=F32)
                z_parts += [mask(z_2[:hq], j), mask(z_2[hq:], j)]
        z = jnp.concatenate(z_parts, axis=0)
        s = softplus2(z)
        sums = jnp.dot(s.astype(BF16), newer, preferred_element_type=F32)
        total = jnp.broadcast_to(jnp.sum(s, axis=1, keepdims=True), (s.shape[0], LANES))
        arg = (z - s) - sums
        if mode == "older":
            car = car_ref[...]
            arg = arg - jnp.concatenate([car] * (win // LANES), axis=1)
            total = car + total
        car_ref[...] = total
        w = jnp.exp2(arg).astype(BF16)

        i = 0
        for j in subs:
            rows = slice(j * hq, (j + 1) * hq)
            for p in pairs:
                o_2 = jnp.dot(w[i * hq:(i + 2) * hq], v_ws[j][:, pair_cols[p]],
                              preferred_element_type=F32)
                i += 2
                o_p = jnp.where(first_head, o_2[:hq], o_2[hq:])
                if mode == "older":
                    acc_ref[rows, pair_cols[p]] += o_p
                else:
                    acc_ref[rows, pair_cols[p]] = o_p
        return (jnp.min(total) >= SURVIVAL_UNDERFLOW_LOG2).astype(jnp.int32)

    if split_new:
        done = visit(0, "inner")
    else:
        done = lax.cond(q_start + hq >= win, lambda: visit(0, "inner"), lambda: visit(0, "edge"))

    def cond(carry):
        m, done = carry
        return jnp.logical_and(q_start + tq - m * win > 0, done == 0)

    def body(carry):
        m, _ = carry
        return m + 1, visit(m, "older")

    lax.while_loop(cond, body, (jnp.int32(1), done))
    o_ref[...] = acc_ref[...].astype(BF16)


def _attn(qkv, k_past=None, v_past=None, *, batch, q_len, pos0, tq, hq):
    assert tq % hq == 0 and q_len % tq == 0 and pos0 % tq == 0 and ATTN_WINDOW % hq == 0
    assert pos0 + q_len >= ATTN_WINDOW and hq % BF16_SUBLANES == 0
    nq = q_len // tq
    stacked_rows = (tq // hq) * SB_HEADS * hq
    q_spec = pl.BlockSpec((tq, D_SB), lambda b, i: (b * nq + i, 0))
    if k_past is None:
        assert pos0 == 0
        in_specs = [q_spec,
                    pl.BlockSpec((q_len, D_SB), lambda b, i: (b, 1)),
                    pl.BlockSpec((q_len, D_SB), lambda b, i: (b, 2))]
        args = (qkv, qkv, qkv)
    else:
        assert tq == hq == q_len and pos0 >= ATTN_WINDOW and pos0 % BF16_SUBLANES == 0
        past_spec = pl.BlockSpec((pos0, D_SB), lambda b, i: (b, 0))
        in_specs = [q_spec, past_spec, past_spec,
                    pl.BlockSpec((hq, D_SB), lambda b, i: (b, 1)),
                    pl.BlockSpec((hq, D_SB), lambda b, i: (b, 2))]
        args = (qkv, k_past, v_past, qkv, qkv)
    return pl.pallas_call(
        functools.partial(_attn_kernel, tq=tq, hq=hq, pos0=pos0, split_new=k_past is not None),
        grid=(batch, nq),
        in_specs=in_specs,
        out_specs=pl.BlockSpec((tq, D_SB), lambda b, i: (b * nq + i, 0)),
        out_shape=jax.ShapeDtypeStruct((batch * q_len, D_SB), BF16),
        scratch_shapes=[pltpu.VMEM((tq, D_SB), F32), pltpu.VMEM((stacked_rows, LANES), F32)],
        compiler_params=pltpu.CompilerParams(
            dimension_semantics=("parallel", "arbitrary"), vmem_limit_bytes=VMEM_LIMIT_BYTES),
        name="attn",
    )(*args)


def _merge_kernel(x_ref, u_ref, halo_ref, o_ref, gate_ref, pw_ref, ps_ref, wbp_ref, wbs_ref,
                  wout_ref, y_ref, *copy_refs, tm, pos0, tiles_per_seq):
    def weight(ref, position, *idx):
        w = ref[idx] if idx else ref[...]
        if copy_refs:
            w = w.astype(BF16)
            if idx:
                copy_refs[position][idx] = w
            else:
                copy_refs[position][...] = w
        return w

    w_sb = weight(wbs_ref, 2)
    streams = halo_ref.shape[0]
    rows = tm // streams
    if streams == 1:
        first_pos = pos0 + (pl.program_id(0) % tiles_per_seq) * tm
        pos = first_pos + lax.broadcasted_iota(jnp.int32, (tm, 1), 0)
    else:
        pos = jnp.concatenate([pos0 + lax.broadcasted_iota(jnp.int32, (rows, 1), 0)] * streams,
                              axis=0)

    n_groups = len(POOL_WINDOWS)
    n_chunks = n_groups if tm % (BF16_SUBLANES * n_groups) == 0 else 1
    quarter = tm // n_chunks
    span = POOL_HALO + rows
    run = jnp.concatenate(
        [part for b in range(streams)
         for part in (halo_ref[b], u_ref[b * rows:(b + 1) * rows, :])], axis=0)
    width, mixed, gated_sb = 1, [], []
    for gi, win in enumerate(POOL_WINDOWS):
        while width < win:
            run = run + pltpu.roll(run, width, 0)
            width *= 2
        assert width == win, "pool windows must be ascending powers of two"
        cols = slice(gi * POOL_GROUP, (gi + 1) * POOL_GROUP)
        cnt = jnp.minimum(win, pos + 1).astype(F32)
        tot = jnp.concatenate([run[b * span + POOL_HALO:(b + 1) * span, 0:POOL_GROUP]
                               for b in range(streams)], axis=0)
        d = (tot / cnt - u_ref[:, cols]).astype(BF16)
        mixed.append(jnp.dot(d, weight(pw_ref, 0, gi), preferred_element_type=F32)
                     * ps_ref[:, cols])
        if gi + 1 < n_groups:
            run = run[:, POOL_GROUP:]
        if gi < n_chunks:
            part = slice(gi * quarter, (gi + 1) * quarter)
            br_sb = jnp.dot(o_ref[part, :], w_sb, preferred_element_type=F32)
            gated_sb.append(
                jax.nn.sigmoid(gate_ref[part, D_MODEL:2 * D_MODEL].astype(F32)) * br_sb)
    a = jnp.concatenate(mixed, axis=1).astype(BF16)

    br_pool = jnp.dot(a, weight(wbp_ref, 1), preferred_element_type=F32)
    merged = (jax.nn.sigmoid(gate_ref[:, 0:D_MODEL].astype(F32)) * br_pool
              + jnp.concatenate(gated_sb, axis=0))
    y_ref[...] = x_ref[...] + jnp.dot(merged.astype(BF16), weight(wout_ref, 3),
                                      preferred_element_type=F32)


def _merge(x, u, halo, o, gate, pool_w, pool_scale, w_bp, w_bs, w_out, *, tm, pos0, seq):
    n = x.shape[0]
    const2 = lambda i: (0, 0)
    resident = pl.Buffered(1)
    streams = max(tm // seq, 1)
    assert tm % seq == 0 or seq % tm == 0
    weights = (pool_w, w_bp, w_bs, w_out)
    cast = pool_w.dtype == F32
    assert not cast or n == tm
    out_specs = [pl.BlockSpec((tm, D_MODEL), lambda i: (i, 0))]
    out_shape = [jax.ShapeDtypeStruct((n, D_MODEL), F32)]
    if cast:
        out_specs += [pl.BlockSpec(m.shape, lambda i, nd=m.ndim: (0,) * nd) for m in weights]
        out_shape += [jax.ShapeDtypeStruct(m.shape, BF16) for m in weights]
    return pl.pallas_call(
        functools.partial(_merge_kernel, tm=tm, pos0=pos0, tiles_per_seq=max(seq // tm, 1)),
        grid=(n // tm,),
        in_specs=[
            pl.BlockSpec((tm, D_MODEL), lambda i: (i, 0)),
            pl.BlockSpec((tm, D_POOL), lambda i: (i, 0)),
            pl.BlockSpec((streams, POOL_HALO, D_POOL), lambda i: (i, 0, 0)),
            pl.BlockSpec((tm, D_SB), lambda i: (i, 0)),
            pl.BlockSpec((tm, 2 * D_MODEL), lambda i: (i, 0)),
            pl.BlockSpec((len(POOL_WINDOWS), POOL_GROUP, POOL_GROUP), lambda i: (0, 0, 0)),
            pl.BlockSpec((1, D_POOL), const2),
            pl.BlockSpec((D_POOL, D_MODEL), const2, pipeline_mode=resident),
            pl.BlockSpec((D_SB, D_MODEL), const2, pipeline_mode=resident),
            pl.BlockSpec((D_MODEL, D_MODEL), const2, pipeline_mode=resident),
        ],
        out_specs=out_specs,
        out_shape=out_shape,
        compiler_params=pltpu.CompilerParams(
            dimension_semantics=("parallel",), vmem_limit_bytes=VMEM_LIMIT_BYTES),
        name="merge_cast" if cast else "merge",
    )(x, u, halo, o, gate, pool_w, pool_scale, w_bp, w_bs, w_out)


def _layer(x, w, *, batch, seq, pos0, k_past, v_past, pool_prefix, tm_ffn, tm_mix, tm_merge, tq,
           hq):
    assert seq >= POOL_STATE
    w_bf16 = {}

    def ffn(x_in, prefix, final_g=None):
        names = [prefix + "_gate", prefix + "_up", prefix + "_down"]
        res = _ffn(x_in, w[prefix + "_norm"], *[w[nm] for nm in names], final_g,
                   tm=tm_ffn, tf=FFN_CHUNK)
        if isinstance(res, (list, tuple)):
            w_bf16.update(zip(names, res[1:]))
            return res[0]
        return res

    x1 = ffn(x, "ffn1")
    u, qkv, k_hm, v_hm, gate, *w_in_copy = _proj(x1, w["mix_norm"], w["w_in"], batch=batch,
                                                 seq=seq, tm=tm_mix)
    w_bf16.update(zip(["w_in"], w_in_copy))

    o = _attn(qkv, k_past, v_past, batch=batch, q_len=seq, pos0=pos0, tq=tq, hq=hq)

    piece = min(tm_merge, seq)
    pieces = seq // piece
    u4 = u.reshape(batch, pieces, piece, D_POOL)
    first = jnp.pad(pool_prefix, ((0, 0), (1, 0), (0, 0)))[:, None]
    halo = jnp.concatenate([first, u4[:, :-1, piece - POOL_HALO:, :]], axis=1)
    halo = halo.reshape(batch * pieces, POOL_HALO, D_POOL)

    merge_names = ["pool_w", "w_branch_pool", "w_branch_sb", "w_out"]
    x2, *copies = _merge(x1, u, halo, o, gate, w["pool_w"], w["pool_scale"], w["w_branch_pool"],
                         w["w_branch_sb"], w["w_out"], tm=tm_merge, pos0=pos0, seq=seq)
    w_bf16.update(zip(merge_names, copies))
    y = ffn(x2, "ffn2", w["final_norm"])
    new_pool = u.reshape(batch, seq, D_POOL)[:, seq - POOL_STATE:, :]
    return (y.reshape(batch, seq, D_MODEL), k_hm, v_hm, new_pool), w_bf16


def kernel(x_prompt, x_sample, cache_k, cache_v, state_pool, ffn1_norm, ffn1_gate, ffn1_up, ffn1_down, mix_norm, w_in, pool_w, pool_scale, w_branch_pool, w_branch_sb, w_out, ffn2_norm, ffn2_gate, ffn2_up, ffn2_down, final_norm):
    assert ffn1_norm.shape[0] == 1, "single-layer kernel"
    w = dict(w_in=w_in[0], pool_w=pool_w[0], w_branch_pool=w_branch_pool[0],
             w_branch_sb=w_branch_sb[0], w_out=w_out[0],
             ffn1_gate=ffn1_gate[0], ffn1_up=ffn1_up[0], ffn1_down=ffn1_down[0],
             ffn2_gate=ffn2_gate[0], ffn2_up=ffn2_up[0], ffn2_down=ffn2_down[0])
    w.update(ffn1_norm=ffn1_norm, mix_norm=mix_norm, pool_scale=pool_scale, ffn2_norm=ffn2_norm,
             final_norm=final_norm.reshape(1, D_MODEL))

    b_p, s_p, _ = x_prompt.shape
    b_d, s_d, _ = x_sample.shape
    past = cache_k.shape[3]

    def token_major(cache):
        return cache[0].transpose(0, 2, 1, 3).reshape(b_d * past, D_SB).astype(BF16)

    (y_d, k_d, v_d, pool_d), w_bf16 = _layer(
        x_sample.reshape(b_d * s_d, D_MODEL), w, batch=b_d, seq=s_d, pos0=past,
        k_past=token_major(cache_k), v_past=token_major(cache_v), pool_prefix=state_pool[0],
        tm_ffn=b_d * s_d, tm_mix=b_d * s_d, tm_merge=b_d * s_d, tq=s_d, hq=s_d)

    (y_p, k_p, v_p, pool_p), _ = _layer(
        x_prompt.reshape(b_p * s_p, D_MODEL), {**w, **w_bf16}, batch=b_p, seq=s_p, pos0=0,
        k_past=None, v_past=None, pool_prefix=jnp.zeros((b_p, POOL_STATE, D_POOL), F32),
        tm_ffn=1024, tm_mix=512, tm_merge=1024, tq=512, hq=64)

    return (y_p, y_d, k_p[None], v_p[None], pool_p[None], k_d[None], v_d[None], pool_d[None])
```

```python
import functools

import jax
import jax.numpy as jnp
from jax import lax
from jax.experimental import pallas as pl
from jax.experimental.pallas import tpu as pltpu

F32 = jnp.float32
BF16 = jnp.bfloat16

D_MODEL = 1024
D_FF = 4 * D_MODEL
D_POOL = D_MODEL // 2
POOL_WINDOWS = (2, 4, 8, 16)
POOL_GROUP = D_POOL // len(POOL_WINDOWS)
POOL_STATE = max(POOL_WINDOWS) - 1
POOL_HALO = POOL_STATE + 1
SB_HEADS = 8
SB_HEAD_DIM = 64
D_SB = SB_HEADS * SB_HEAD_DIM
D_IN = D_POOL + 3 * D_SB + 2 * D_MODEL
RMS_EPS = 1e-6
LANES = 128
BF16_SUBLANES = 16
VMEM_LIMIT_BYTES = 56 * 1024 * 1024
FFN_CHUNK = 1024
ATTN_WINDOW = 2 * LANES

LOG2E = 1.4426950408889634
SURVIVAL_UNDERFLOW_LOG2 = 150.0
MASKED_LOGIT = -1e30


def _rmsnorm(x, g):
    ms = jnp.mean(x * x, axis=-1, keepdims=True)
    return (x * lax.rsqrt(ms + RMS_EPS)) * g


def _ffn_kernel(*refs, n_ff, final, cast):
    refs = list(refs)
    x_ref, xnext_ref, g_ref, wg_ref, wu_ref, wd_ref = refs[:6]
    del refs[:6]
    fg_ref = refs.pop(0) if final else None
    o_ref = refs.pop(0)
    if cast:
        wg_out, wu_out, wd_out = refs[:3]
        del refs[:3]
    xn_ref, acc_ref = refs
    i = pl.program_id(0)
    j = pl.program_id(1)
    slot = i % 2

    @pl.when(jnp.logical_and(i == 0, j == 0))
    def _():
        xn_ref[0] = _rmsnorm(x_ref[...], g_ref[...]).astype(BF16)

    def chunk():
        wg, wu, wd = wg_ref[...], wu_ref[...], wd_ref[...]
        if cast:
            wg, wu, wd = wg.astype(BF16), wu.astype(BF16), wd.astype(BF16)
            wg_out[...] = wg
            wu_out[...] = wu
            wd_out[...] = wd
        xn = xn_ref[slot]
        gate = jnp.dot(xn, wg, preferred_element_type=F32)
        up = jnp.dot(xn, wu, preferred_element_type=F32)
        h = (gate * jax.nn.sigmoid(gate) * up).astype(BF16)
        return jnp.dot(h, wd, preferred_element_type=F32)

    def prepare_next(rows):
        xn_ref[1 - slot, rows, :] = _rmsnorm(xnext_ref[rows, :], g_ref[...]).astype(BF16)

    piece = x_ref.shape[0] // n_ff

    def first():
        acc_ref[...] = chunk()
        prepare_next(pl.ds(0, piece))

    def middle():
        acc_ref[...] += chunk()
        prepare_next(pl.ds(pl.multiple_of(j * piece, piece), piece))

    def last():
        y = x_ref[...] + 0.5 * (acc_ref[...] + chunk())
        if final:
            y = _rmsnorm(y, fg_ref[...])
        o_ref[...] = y
        prepare_next(pl.ds((n_ff - 1) * piece, piece))

    case = jnp.where(j == 0, 0, jnp.where(j == n_ff - 1, 2, 1))
    lax.switch(case, [first, middle, last])


def _ffn(x, norm_g, wg, wu, wd, final_g=None, *, tm, tf):
    n = x.shape[0]
    n_rows = n // tm
    n_ff = D_FF // tf
    assert n_ff >= 2 and tm % (BF16_SUBLANES * n_ff) == 0
    final = final_g is not None
    cast = wg.dtype == F32
    assert not cast or n_rows == 1
    row = pl.BlockSpec((1, D_MODEL), lambda i, j: (0, 0))
    w_specs = [
        pl.BlockSpec((D_MODEL, tf), lambda i, j: (0, j)),
        pl.BlockSpec((D_MODEL, tf), lambda i, j: (0, j)),
        pl.BlockSpec((tf, D_MODEL), lambda i, j: (j, 0)),
    ]
    in_specs = [
        pl.BlockSpec((tm, D_MODEL), lambda i, j: (i, 0)),
        pl.BlockSpec((tm, D_MODEL), lambda i, j: (jnp.minimum(i + 1, n_rows - 1), 0)),
        row,
    ] + w_specs
    args = [x, x, norm_g, wg, wu, wd]
    if final:
        in_specs.append(row)
        args.append(final_g)
    out_specs = [pl.BlockSpec((tm, D_MODEL), lambda i, j: (i, 0))]
    out_shape = [jax.ShapeDtypeStruct((n, D_MODEL), F32)]
    if cast:
        out_specs += w_specs
        out_shape += [jax.ShapeDtypeStruct(m.shape, BF16) for m in (wg, wu, wd)]
    outs = pl.pallas_call(
        functools.partial(_ffn_kernel, n_ff=n_ff, final=final, cast=cast),
        grid=(n_rows, n_ff),
        in_specs=in_specs,
        out_specs=out_specs,
        out_shape=out_shape,
        scratch_shapes=[pltpu.VMEM((2, tm, D_MODEL), BF16), pltpu.VMEM((tm, D_MODEL), F32)],
        compiler_params=pltpu.CompilerParams(
            dimension_semantics=("arbitrary", "arbitrary"), vmem_limit_bytes=VMEM_LIMIT_BYTES),
        name=("ffn_final" if final else "ffn") + ("_cast" if cast else ""),
    )(*args)
    return outs if cast else outs[0]


def _proj_kernel(x_ref, g_ref, w_ref, u_ref, qkv_ref, k_ref, v_ref, gate_ref, w_out_ref=None):
    h = _rmsnorm(x_ref[...], g_ref[...]).astype(BF16)

    def mm(c0, c1):
        w = w_ref[:, c0:c1]
        if w_out_ref is not None:
            w = w.astype(BF16)
            w_out_ref[:, c0:c1] = w
        return jnp.dot(h, w, preferred_element_type=F32)

    c = 0
    u_ref[...] = mm(c, c + D_POOL)
    c += D_POOL
    qkv_ref[:, 0:D_SB] = (mm(c, c + D_SB) * (LOG2E * SB_HEAD_DIM ** -0.5)).astype(BF16)
    c += D_SB
    for idx, hm_ref in ((1, k_ref), (2, v_ref)):
        t = mm(c, c + D_SB)
        c += D_SB
        qkv_ref[:, idx * D_SB:(idx + 1) * D_SB] = t.astype(BF16)
        streams, _, rows, _ = hm_ref.shape
        for b in range(streams):
            for hd in range(SB_HEADS):
                hm_ref[b, hd, :, :] = t[b * rows:(b + 1) * rows,
                                        hd * SB_HEAD_DIM:(hd + 1) * SB_HEAD_DIM]
    gate_ref[:, 0:D_MODEL] = mm(c, c + D_MODEL).astype(BF16)
    c += D_MODEL
    gate_ref[:, D_MODEL:2 * D_MODEL] = mm(c, c + D_MODEL).astype(BF16)


def _proj(x, norm_g, w_in, *, batch, seq, tm):
    n = x.shape[0]
    cast = w_in.dtype == F32
    assert not cast or n == tm
    if tm <= seq:
        tiles_per_seq = seq // tm
        hm_spec = pl.BlockSpec((1, SB_HEADS, tm, SB_HEAD_DIM),
                               lambda i: (i // tiles_per_seq, 0, i % tiles_per_seq, 0))
    else:
        assert tm % seq == 0
        hm_spec = pl.BlockSpec((tm // seq, SB_HEADS, seq, SB_HEAD_DIM), lambda i: (i, 0, 0, 0))
    hm_shape = jax.ShapeDtypeStruct((batch, SB_HEADS, seq, SB_HEAD_DIM), F32)
    w_spec = pl.BlockSpec((D_MODEL, D_IN), lambda i: (0, 0), pipeline_mode=pl.Buffered(1))
    out_specs = [
        pl.BlockSpec((tm, D_POOL), lambda i: (i, 0)),
        pl.BlockSpec((tm, 3 * D_SB), lambda i: (i, 0)),
        hm_spec,
        hm_spec,
        pl.BlockSpec((tm, 2 * D_MODEL), lambda i: (i, 0)),
    ]
    out_shape = [
        jax.ShapeDtypeStruct((n, D_POOL), F32),
        jax.ShapeDtypeStruct((n, 3 * D_SB), BF16),
        hm_shape,
        hm_shape,
        jax.ShapeDtypeStruct((n, 2 * D_MODEL), BF16),
    ]
    if cast:
        out_specs.append(pl.BlockSpec((D_MODEL, D_IN), lambda i: (0, 0)))
        out_shape.append(jax.ShapeDtypeStruct((D_MODEL, D_IN), BF16))
    return pl.pallas_call(
        _proj_kernel,
        grid=(n // tm,),
        in_specs=[
            pl.BlockSpec((tm, D_MODEL), lambda i: (i, 0)),
            pl.BlockSpec((1, D_MODEL), lambda i: (0, 0)),
            w_spec,
        ],
        out_specs=out_specs,
        out_shape=out_shape,
        compiler_params=pltpu.CompilerParams(
            dimension_semantics=("parallel",), vmem_limit_bytes=VMEM_LIMIT_BYTES),
        name="proj_cast" if cast else "proj",
    )(x, norm_g, w_in)


def _attn_kernel(q_ref, k_ref, v_ref, o_ref, acc_ref, car_ref, *, tq, hq, pos0):
    win = ATTN_WINDOW
    q_start = pos0 + pl.program_id(1) * tq
    subs = range(tq // hq)
    pairs = range(SB_HEADS // 2)
    pair_cols = [slice(p * LANES, (p + 1) * LANES) for p in pairs]

    lane = lax.broadcasted_iota(jnp.int32, (1, LANES), 1)
    first_head = lane < SB_HEAD_DIM
    r = lax.broadcasted_iota(jnp.int32, (win, win), 0)
    c = lax.broadcasted_iota(jnp.int32, (win, win), 1)
    newer = jnp.where(r > c, 1.0, 0.0).astype(BF16)

    def softplus2(z):
        return jnp.maximum(z, 0.0) + jnp.log2(1.0 + jnp.exp2(-jnp.abs(z)))

    def visit(m, mode):
        k_ws, v_ws, valids = [], [], []
        for j in subs:
            row0 = q_start + j * hq
            win_end = row0 + hq - m * win
            q_pos = row0 + lax.broadcasted_iota(jnp.int32, (hq, 1), 0)
            if mode == "inner":
                start = pl.multiple_of(win_end - win, hq)
                k_pos = (win_end - LANES) + lax.broadcasted_iota(jnp.int32, (1, LANES), 1)
                valids.append(k_pos < q_pos)
            else:
                start = pl.multiple_of(jnp.maximum(win_end - win, 0), hq)
                k_pos = start + lax.broadcasted_iota(jnp.int32, (1, win), 1)
                valids.append(k_pos < (q_pos if mode == "edge" else win_end))
            k_ws.append(k_ref[pl.ds(start, win), :])
            v_ws.append(v_ref[pl.ds(start, win), :])

        def mask(x, j):
            if mode == "inner":
                return jnp.concatenate(
                    [x[:, :win - LANES],
                     jnp.where(valids[j], x[:, win - LANES:], MASKED_LOGIT)], axis=1)
            return jnp.where(valids[j], x, MASKED_LOGIT)

        z_parts = []
        for j in subs:
            rows = slice(j * hq, (j + 1) * hq)
            for p in pairs:
                q_p = q_ref[rows, pair_cols[p]]
                zero = jnp.zeros_like(q_p)
                q_2 = jnp.concatenate([jnp.where(first_head, q_p, zero),
                                       jnp.where(first_head, zero, q_p)], axis=0)
                z_2 = lax.dot_general(q_2, k_ws[j][:, pair_cols[p]], (((1,), (1,)), ((), ())),
                                      preferred_element_type=F32)
                z_parts += [mask(z_2[:hq], j), mask(z_2[hq:], j)]
        z = jnp.concatenate(z_parts, axis=0)
        s = softplus2(z)
        sums = jnp.dot(s.astype(BF16), newer, preferred_element_type=F32)
        total = jnp.broadcast_to(jnp.sum(s, axis=1, keepdims=True), (s.shape[0], LANES))
        arg = (z - s) - sums
        if mode == "older":
            car = car_ref[...]
            arg = arg - jnp.concatenate([car] * (win // LANES), axis=1)
            total = car + total
        car_ref[...] = total
        w = jnp.exp2(arg).astype(BF16)

        i = 0
        for j in subs:
            rows = slice(j * hq, (j + 1) * hq)
            for p in pairs:
                o_2 = jnp.dot(w[i * hq:(i + 2) * hq], v_ws[j][:, pair_cols[p]],
                              preferred_element_type=F32)
                i += 2
                o_p = jnp.where(first_head, o_2[:hq], o_2[hq:])
                if mode == "older":
                    acc_ref[rows, pair_cols[p]] += o_p
                else:
                    acc_ref[rows, pair_cols[p]] = o_p
        return (jnp.min(total) >= SURVIVAL_UNDERFLOW_LOG2).astype(jnp.int32)

    done = lax.cond(q_start + hq >= win, lambda: visit(0, "inner"), lambda: visit(0, "edge"))

    def cond(carry):
        m, done = carry
        return jnp.logical_and(q_start + tq - m * win > 0, done == 0)

    def body(carry):
        m, _ = carry
        return m + 1, visit(m, "older")

    lax.while_loop(cond, body, (jnp.int32(1), done))
    o_ref[...] = acc_ref[...].astype(BF16)


def _attn(q_arr, q_col, k_arr, k_col, v_arr, v_col, *, batch, q_len, k_len, pos0, tq, hq):
    assert tq % hq == 0 and q_len % tq == 0 and pos0 % tq == 0 and ATTN_WINDOW % hq == 0
    assert pos0 + q_len <= k_len and k_len >= ATTN_WINDOW and hq % BF16_SUBLANES == 0
    nq = q_len // tq
    stacked_rows = (tq // hq) * SB_HEADS * hq
    return pl.pallas_call(
        functools.partial(_attn_kernel, tq=tq, hq=hq, pos0=pos0),
        grid=(batch, nq),
        in_specs=[
            pl.BlockSpec((tq, D_SB), lambda b, i: (b * nq + i, q_col)),
            pl.BlockSpec((k_len, D_SB), lambda b, i: (b, k_col)),
            pl.BlockSpec((k_len, D_SB), lambda b, i: (b, v_col)),
        ],
        out_specs=pl.BlockSpec((tq, D_SB), lambda b, i: (b * nq + i, 0)),
        out_shape=jax.ShapeDtypeStruct((batch * q_len, D_SB), BF16),
        scratch_shapes=[pltpu.VMEM((tq, D_SB), F32), pltpu.VMEM((stacked_rows, LANES), F32)],
        compiler_params=pltpu.CompilerParams(
            dimension_semantics=("parallel", "arbitrary"), vmem_limit_bytes=VMEM_LIMIT_BYTES),
        name="attn",
    )(q_arr, k_arr, v_arr)


def _merge_kernel(x_ref, u_ref, halo_ref, o_ref, gate_ref, pw_ref, ps_ref, wbp_ref, wbs_ref,
                  wout_ref, y_ref, *copy_refs, tm, pos0, tiles_per_seq):
    def weight(ref, position, *idx):
        w = ref[idx] if idx else ref[...]
        if copy_refs:
            w = w.astype(BF16)
            if idx:
                copy_refs[position][idx] = w
            else:
                copy_refs[position][...] = w
        return w

    w_sb = weight(wbs_ref, 2)
    streams = halo_ref.shape[0]
    rows = tm // streams
    if streams == 1:
        first_pos = pos0 + (pl.program_id(0) % tiles_per_seq) * tm
        pos = first_pos + lax.broadcasted_iota(jnp.int32, (tm, 1), 0)
    else:
        pos = jnp.concatenate([pos0 + lax.broadcasted_iota(jnp.int32, (rows, 1), 0)] * streams,
                              axis=0)

    n_groups = len(POOL_WINDOWS)
    n_chunks = n_groups if tm % (BF16_SUBLANES * n_groups) == 0 else 1
    quarter = tm // n_chunks
    span = POOL_HALO + rows
    run = jnp.concatenate(
        [part for b in range(streams)
         for part in (halo_ref[b], u_ref[b * rows:(b + 1) * rows, :])], axis=0)
    width, mixed, gated_sb = 1, [], []
    for gi, win in enumerate(POOL_WINDOWS):
        while width < win:
            run = run + pltpu.roll(run, width, 0)
            width *= 2
        assert width == win, "pool windows must be ascending powers of two"
        cols = slice(gi * POOL_GROUP, (gi + 1) * POOL_GROUP)
        cnt = jnp.minimum(win, pos + 1).astype(F32)
        tot = jnp.concatenate([run[b * span + POOL_HALO:(b + 1) * span, 0:POOL_GROUP]
                               for b in range(streams)], axis=0)
        d = (tot / cnt - u_ref[:, cols]).astype(BF16)
        mixed.append(jnp.dot(d, weight(pw_ref, 0, gi), preferred_element_type=F32)
                     * ps_ref[:, cols])
        if gi + 1 < n_groups:
            run = run[:, POOL_GROUP:]
        if gi < n_chunks:
            part = slice(gi * quarter, (gi + 1) * quarter)
            br_sb = jnp.dot(o_ref[part, :], w_sb, preferred_element_type=F32)
            gated_sb.append(
                jax.nn.sigmoid(gate_ref[part, D_MODEL:2 * D_MODEL].astype(F32)) * br_sb)
    a = jnp.concatenate(mixed, axis=1).astype(BF16)

    br_pool = jnp.dot(a, weight(wbp_ref, 1), preferred_element_type=F32)
    merged = (jax.nn.sigmoid(gate_ref[:, 0:D_MODEL].astype(F32)) * br_pool
              + jnp.concatenate(gated_sb, axis=0))
    y_ref[...] = x_ref[...] + jnp.dot(merged.astype(BF16), weight(wout_ref, 3),
                                      preferred_element_type=F32)


def _merge(x, u, halo, o, gate, pool_w, pool_scale, w_bp, w_bs, w_out, *, tm, pos0, seq):
    n = x.shape[0]
    const2 = lambda i: (0, 0)
    resident = pl.Buffered(1)
    streams = max(tm // seq, 1)
    assert tm % seq == 0 or seq % tm == 0
    weights = (pool_w, w_bp, w_bs, w_out)
    cast = pool_w.dtype == F32
    assert not cast or n == tm
    out_specs = [pl.BlockSpec((tm, D_MODEL), lambda i: (i, 0))]
    out_shape = [jax.ShapeDtypeStruct((n, D_MODEL), F32)]
    if cast:
        out_specs += [pl.BlockSpec(m.shape, lambda i, nd=m.ndim: (0,) * nd) for m in weights]
        out_shape += [jax.ShapeDtypeStruct(m.shape, BF16) for m in weights]
    return pl.pallas_call(
        functools.partial(_merge_kernel, tm=tm, pos0=pos0, tiles_per_seq=max(seq // tm, 1)),
        grid=(n // tm,),
        in_specs=[
            pl.BlockSpec((tm, D_MODEL), lambda i: (i, 0)),
            pl.BlockSpec((tm, D_POOL), lambda i: (i, 0)),
            pl.BlockSpec((streams, POOL_HALO, D_POOL), lambda i: (i, 0, 0)),
            pl.BlockSpec((tm, D_SB), lambda i: (i, 0)),
            pl.BlockSpec((tm, 2 * D_MODEL), lambda i: (i, 0)),
            pl.BlockSpec((len(POOL_WINDOWS), POOL_GROUP, POOL_GROUP), lambda i: (0, 0, 0)),
            pl.BlockSpec((1, D_POOL), const2),
            pl.BlockSpec((D_POOL, D_MODEL), const2, pipeline_mode=resident),
            pl.BlockSpec((D_SB, D_MODEL), const2, pipeline_mode=resident),
            pl.BlockSpec((D_MODEL, D_MODEL), const2, pipeline_mode=resident),
        ],
        out_specs=out_specs,
        out_shape=out_shape,
        compiler_params=pltpu.CompilerParams(
            dimension_semantics=("parallel",), vmem_limit_bytes=VMEM_LIMIT_BYTES),
        name="merge_cast" if cast else "merge",
    )(x, u, halo, o, gate, pool_w, pool_scale, w_bp, w_bs, w_out)


def _layer(x, w, *, batch, seq, pos0, k_past, v_past, pool_prefix, tm_ffn, tm_mix, tm_merge, tq,
           hq):
    assert seq >= POOL_STATE
    w_bf16 = {}

    def ffn(x_in, prefix, final_g=None):
        names = [prefix + "_gate", prefix + "_up", prefix + "_down"]
        res = _ffn(x_in, w[prefix + "_norm"], *[w[nm] for nm in names], final_g,
                   tm=tm_ffn, tf=FFN_CHUNK)
        if isinstance(res, (list, tuple)):
            w_bf16.update(zip(names, res[1:]))
            return res[0]
        return res

    x1 = ffn(x, "ffn1")
    u, qkv, k_hm, v_hm, gate, *w_in_copy = _proj(x1, w["mix_norm"], w["w_in"], batch=batch,
                                                 seq=seq, tm=tm_mix)
    w_bf16.update(zip(["w_in"], w_in_copy))

    if k_past is None:
        o = _attn(qkv, 0, qkv, 1, qkv, 2, batch=batch, q_len=seq, k_len=seq, pos0=0, tq=tq, hq=hq)
    else:
        def with_past(past, col):
            new = qkv[:, col * D_SB:(col + 1) * D_SB].reshape(batch, seq, D_SB)
            return jnp.concatenate([past, new], axis=1).reshape(batch * (pos0 + seq), D_SB)
        o = _attn(qkv, 0, with_past(k_past, 1), 0, with_past(v_past, 2), 0,
                  batch=batch, q_len=seq, k_len=pos0 + seq, pos0=pos0, tq=tq, hq=hq)

    piece = min(tm_merge, seq)
    pieces = seq // piece
    u4 = u.reshape(batch, pieces, piece, D_POOL)
    first = jnp.pad(pool_prefix, ((0, 0), (1, 0), (0, 0)))[:, None]
    halo = jnp.concatenate([first, u4[:, :-1, piece - POOL_HALO:, :]], axis=1)
    halo = halo.reshape(batch * pieces, POOL_HALO, D_POOL)

    merge_names = ["pool_w", "w_branch_pool", "w_branch_sb", "w_out"]
    x2, *copies = _merge(x1, u, halo, o, gate, w["pool_w"], w["pool_scale"], w["w_branch_pool"],
                         w["w_branch_sb"], w["w_out"], tm=tm_merge, pos0=pos0, seq=seq)
    w_bf16.update(zip(merge_names, copies))
    y = ffn(x2, "ffn2", w["final_norm"])
    new_pool = u.reshape(batch, seq, D_POOL)[:, seq - POOL_STATE:, :]
    return (y.reshape(batch, seq, D_MODEL), k_hm, v_hm, new_pool), w_bf16


def kernel(x_prompt, x_sample, cache_k, cache_v, state_pool, ffn1_norm, ffn1_gate, ffn1_up, ffn1_down, mix_norm, w_in, pool_w, pool_scale, w_branch_pool, w_branch_sb, w_out, ffn2_norm, ffn2_gate, ffn2_up, ffn2_down, final_norm):
    assert ffn1_norm.shape[0] == 1, "single-layer kernel"
    w = dict(w_in=w_in[0], pool_w=pool_w[0], w_branch_pool=w_branch_pool[0],
             w_branch_sb=w_branch_sb[0], w_out=w_out[0],
             ffn1_gate=ffn1_gate[0], ffn1_up=ffn1_up[0], ffn1_down=ffn1_down[0],
             ffn2_gate=ffn2_gate[0], ffn2_up=ffn2_up[0], ffn2_down=ffn2_down[0])
    w.update(ffn1_norm=ffn1_norm, mix_norm=mix_norm, pool_scale=pool_scale, ffn2_norm=ffn2_norm,
             final_norm=final_norm.reshape(1, D_MODEL))

    b_p, s_p, _ = x_prompt.shape
    b_d, s_d, _ = x_sample.shape
    past = cache_k.shape[3]

    def token_major(cache):
        return cache[0].transpose(0, 2, 1, 3).reshape(b_d, past, D_SB).astype(BF16)

    (y_d, k_d, v_d, pool_d), w_bf16 = _layer(
        x_sample.reshape(b_d * s_d, D_MODEL), w, batch=b_d, seq=s_d, pos0=past,
        k_past=token_major(cache_k), v_past=token_major(cache_v), pool_prefix=state_pool[0],
        tm_ffn=b_d * s_d, tm_mix=b_d * s_d, tm_merge=b_d * s_d, tq=s_d, hq=s_d)

    (y_p, k_p, v_p, pool_p), _ = _layer(
        x_prompt.reshape(b_p * s_p, D_MODEL), {**w, **w_bf16}, batch=b_p, seq=s_p, pos0=0,
        k_past=None, v_past=None, pool_prefix=jnp.zeros((b_p, POOL_STATE, D_POOL), F32),
        tm_ffn=1024, tm_mix=512, tm_merge=1024, tq=512, hq=64)

    return (y_p, y_d, k_p[None], v_p[None], pool_p[None], k_d[None], v_d[None], pool_d[None])
```

```python
import functools

import jax
import jax.numpy as jnp
from jax import lax
from jax.experimental import pallas as pl
from jax.experimental.pallas import tpu as pltpu

F32 = jnp.float32
BF16 = jnp.bfloat16

D_MODEL = 1024
D_FF = 4 * D_MODEL
D_POOL = D_MODEL // 2
POOL_WINDOWS = (2, 4, 8, 16)
POOL_GROUP = D_POOL // len(POOL_WINDOWS)
POOL_STATE = max(POOL_WINDOWS) - 1
POOL_HALO = POOL_STATE + 1
SB_HEADS = 8
SB_HEAD_DIM = 64
D_SB = SB_HEADS * SB_HEAD_DIM
D_IN = D_POOL + 3 * D_SB + 2 * D_MODEL
RMS_EPS = 1e-6
LANES = 128
BF16_SUBLANES = 16
VMEM_LIMIT_BYTES = 56 * 1024 * 1024
FFN_CHUNK = 1024
ATTN_WINDOW = 2 * LANES
ATTN_GROUP = 4

LOG2E = 1.4426950408889634
SURVIVAL_UNDERFLOW_LOG2 = 150.0
MASKED_LOGIT = -1e30


def _rmsnorm(x, g):
    ms = jnp.mean(x * x, axis=-1, keepdims=True)
    return (x * lax.rsqrt(ms + RMS_EPS)) * g


def _ffn_kernel(*refs, n_ff, final, cast):
    refs = list(refs)
    x_ref, xnext_ref, g_ref, wg_ref, wu_ref, wd_ref = refs[:6]
    del refs[:6]
    fg_ref = refs.pop(0) if final else None
    o_ref = refs.pop(0)
    if cast:
        wg_out, wu_out, wd_out = refs[:3]
        del refs[:3]
    xn_ref, acc_ref = refs
    i = pl.program_id(0)
    j = pl.program_id(1)
    slot = i % 2

    @pl.when(jnp.logical_and(i == 0, j == 0))
    def _():
        xn_ref[0] = _rmsnorm(x_ref[...], g_ref[...]).astype(BF16)

    def chunk():
        wg, wu, wd = wg_ref[...], wu_ref[...], wd_ref[...]
        if cast:
            wg, wu, wd = wg.astype(BF16), wu.astype(BF16), wd.astype(BF16)
            wg_out[...] = wg
            wu_out[...] = wu
            wd_out[...] = wd
        xn = xn_ref[slot]
        gate = jnp.dot(xn, wg, preferred_element_type=F32)
        up = jnp.dot(xn, wu, preferred_element_type=F32)
        h = (gate * jax.nn.sigmoid(gate) * up).astype(BF16)
        return jnp.dot(h, wd, preferred_element_type=F32)

    def prepare_next(rows):
        xn_ref[1 - slot, rows, :] = _rmsnorm(xnext_ref[rows, :], g_ref[...]).astype(BF16)

    piece = x_ref.shape[0] // n_ff

    def first():
        acc_ref[...] = chunk()
        prepare_next(pl.ds(0, piece))

    def middle():
        acc_ref[...] += chunk()
        prepare_next(pl.ds(pl.multiple_of(j * piece, piece), piece))

    def last():
        y = x_ref[...] + 0.5 * (acc_ref[...] + chunk())
        if final:
            y = _rmsnorm(y, fg_ref[...])
        o_ref[...] = y
        prepare_next(pl.ds((n_ff - 1) * piece, piece))

    case = jnp.where(j == 0, 0, jnp.where(j == n_ff - 1, 2, 1))
    lax.switch(case, [first, middle, last])


def _ffn(x, norm_g, wg, wu, wd, final_g=None, *, tm, tf):
    n = x.shape[0]
    n_rows = n // tm
    n_ff = D_FF // tf
    assert n_ff >= 2 and tm % (BF16_SUBLANES * n_ff) == 0
    final = final_g is not None
    cast = wg.dtype == F32
    assert not cast or n_rows == 1
    row = pl.BlockSpec((1, D_MODEL), lambda i, j: (0, 0))
    w_specs = [
        pl.BlockSpec((D_MODEL, tf), lambda i, j: (0, j)),
        pl.BlockSpec((D_MODEL, tf), lambda i, j: (0, j)),
        pl.BlockSpec((tf, D_MODEL), lambda i, j: (j, 0)),
    ]
    in_specs = [
        pl.BlockSpec((tm, D_MODEL), lambda i, j: (i, 0)),
        pl.BlockSpec((tm, D_MODEL), lambda i, j: (jnp.minimum(i + 1, n_rows - 1), 0)),
        row,
    ] + w_specs
    args = [x, x, norm_g, wg, wu, wd]
    if final:
        in_specs.append(row)
        args.append(final_g)
    out_specs = [pl.BlockSpec((tm, D_MODEL), lambda i, j: (i, 0))]
    out_shape = [jax.ShapeDtypeStruct((n, D_MODEL), F32)]
    if cast:
        out_specs += w_specs
        out_shape += [jax.ShapeDtypeStruct(m.shape, BF16) for m in (wg, wu, wd)]
    outs = pl.pallas_call(
        functools.partial(_ffn_kernel, n_ff=n_ff, final=final, cast=cast),
        grid=(n_rows, n_ff),
        in_specs=in_specs,
        out_specs=out_specs,
        out_shape=out_shape,
        scratch_shapes=[pltpu.VMEM((2, tm, D_MODEL), BF16), pltpu.VMEM((tm, D_MODEL), F32)],
        compiler_params=pltpu.CompilerParams(
            dimension_semantics=("arbitrary", "arbitrary"), vmem_limit_bytes=VMEM_LIMIT_BYTES),
        name=("ffn_final" if final else "ffn") + ("_cast" if cast else ""),
    )(*args)
    return outs if cast else outs[0]


def _proj_kernel(x_ref, g_ref, w_ref, u_ref, qkv_ref, k_ref, v_ref, gate_ref, w_out_ref=None):
    h = _rmsnorm(x_ref[...], g_ref[...]).astype(BF16)

    def mm(c0, c1):
        w = w_ref[:, c0:c1]
        if w_out_ref is not None:
            w = w.astype(BF16)
            w_out_ref[:, c0:c1] = w
        return jnp.dot(h, w, preferred_element_type=F32)

    c = 0
    u_ref[...] = mm(c, c + D_POOL)
    c += D_POOL
    qkv_ref[:, 0:D_SB] = (mm(c, c + D_SB) * (LOG2E * SB_HEAD_DIM ** -0.5)).astype(BF16)
    c += D_SB
    for idx, hm_ref in ((1, k_ref), (2, v_ref)):
        t = mm(c, c + D_SB)
        c += D_SB
        qkv_ref[:, idx * D_SB:(idx + 1) * D_SB] = t.astype(BF16)
        streams, _, rows, _ = hm_ref.shape
        for b in range(streams):
            for hd in range(SB_HEADS):
                hm_ref[b, hd, :, :] = t[b * rows:(b + 1) * rows,
                                        hd * SB_HEAD_DIM:(hd + 1) * SB_HEAD_DIM]
    gate_ref[:, 0:D_MODEL] = mm(c, c + D_MODEL).astype(BF16)
    c += D_MODEL
    gate_ref[:, D_MODEL:2 * D_MODEL] = mm(c, c + D_MODEL).astype(BF16)


def _proj(x, norm_g, w_in, *, batch, seq, tm):
    n = x.shape[0]
    cast = w_in.dtype == F32
    assert not cast or n == tm
    if tm <= seq:
        tiles_per_seq = seq // tm
        hm_spec = pl.BlockSpec((1, SB_HEADS, tm, SB_HEAD_DIM),
                               lambda i: (i // tiles_per_seq, 0, i % tiles_per_seq, 0))
    else:
        assert tm % seq == 0
        hm_spec = pl.BlockSpec((tm // seq, SB_HEADS, seq, SB_HEAD_DIM), lambda i: (i, 0, 0, 0))
    hm_shape = jax.ShapeDtypeStruct((batch, SB_HEADS, seq, SB_HEAD_DIM), F32)
    w_spec = pl.BlockSpec((D_MODEL, D_IN), lambda i: (0, 0), pipeline_mode=pl.Buffered(1))
    out_specs = [
        pl.BlockSpec((tm, D_POOL), lambda i: (i, 0)),
        pl.BlockSpec((tm, 3 * D_SB), lambda i: (i, 0)),
        hm_spec,
        hm_spec,
        pl.BlockSpec((tm, 2 * D_MODEL), lambda i: (i, 0)),
    ]
    out_shape = [
        jax.ShapeDtypeStruct((n, D_POOL), F32),
        jax.ShapeDtypeStruct((n, 3 * D_SB), BF16),
        hm_shape,
        hm_shape,
        jax.ShapeDtypeStruct((n, 2 * D_MODEL), BF16),
    ]
    if cast:
        out_specs.append(pl.BlockSpec((D_MODEL, D_IN), lambda i: (0, 0)))
        out_shape.append(jax.ShapeDtypeStruct((D_MODEL, D_IN), BF16))
    return pl.pallas_call(
        _proj_kernel,
        grid=(n // tm,),
        in_specs=[
            pl.BlockSpec((tm, D_MODEL), lambda i: (i, 0)),
            pl.BlockSpec((1, D_MODEL), lambda i: (0, 0)),
            w_spec,
        ],
        out_specs=out_specs,
        out_shape=out_shape,
        compiler_params=pltpu.CompilerParams(
            dimension_semantics=("parallel",), vmem_limit_bytes=VMEM_LIMIT_BYTES),
        name="proj_cast" if cast else "proj",
    )(x, norm_g, w_in)


def _attn_kernel(q_ref, k_ref, v_ref, o_ref, acc_ref, car_ref, *, tq, hq, pos0):
    win = ATTN_WINDOW
    q_start = pos0 + pl.program_id(1) * tq
    subs = range(tq // hq)
    pairs = range(SB_HEADS // 2)
    pair_cols = [slice(p * LANES, (p + 1) * LANES) for p in pairs]

    lane = lax.broadcasted_iota(jnp.int32, (1, LANES), 1)
    first_head = lane < SB_HEAD_DIM
    r = lax.broadcasted_iota(jnp.int32, (win, win), 0)
    c = lax.broadcasted_iota(jnp.int32, (win, win), 1)
    newer = jnp.where(r > c, 1.0, 0.0).astype(BF16)

    def softplus2(z):
        return jnp.maximum(z, 0.0) + jnp.log2(1.0 + jnp.exp2(-jnp.abs(z)))

    group_size = min(len(subs), ATTN_GROUP)
    groups = [subs[g:g + group_size] for g in range(0, len(subs), group_size)]

    def visit(m, mode):
        flags = [visit_group(m, mode, g, subs_g) for g, subs_g in enumerate(groups)]
        return functools.reduce(jnp.minimum, flags)

    def visit_group(m, mode, g, subs):
        car_rows = pl.ds(g * group_size * SB_HEADS * hq, len(subs) * SB_HEADS * hq)
        k_ws, v_ws, valids = {}, {}, {}
        for j in subs:
            row0 = q_start + j * hq
            win_end = row0 + hq - m * win
            q_pos = row0 + lax.broadcasted_iota(jnp.int32, (hq, 1), 0)
            if mode == "inner":
                start = pl.multiple_of(win_end - win, hq)
                k_pos = (win_end - LANES) + lax.broadcasted_iota(jnp.int32, (1, LANES), 1)
                valids[j] = k_pos < q_pos
            else:
                start = pl.multiple_of(jnp.maximum(win_end - win, 0), hq)
                k_pos = start + lax.broadcasted_iota(jnp.int32, (1, win), 1)
                valids[j] = k_pos < (q_pos if mode == "edge" else win_end)
            k_ws[j] = k_ref[pl.ds(start, win), :]
            v_ws[j] = v_ref[pl.ds(start, win), :]

        def mask(x, j):
            if mode == "inner":
                return jnp.concatenate(
                    [x[:, :win - LANES],
                     jnp.where(valids[j], x[:, win - LANES:], MASKED_LOGIT)], axis=1)
            return jnp.where(valids[j], x, MASKED_LOGIT)

        z_parts = []
        for j in subs:
            rows = slice(j * hq, (j + 1) * hq)
            for p in pairs:
                q_p = q_ref[rows, pair_cols[p]]
                zero = jnp.zeros_like(q_p)
                q_2 = jnp.concatenate([jnp.where(first_head, q_p, zero),
                                       jnp.where(first_head, zero, q_p)], axis=0)
                z_2 = lax.dot_general(q_2, k_ws[j][:, pair_cols[p]], (((1,), (1,)), ((), ())),
                                      preferred_element_type=F32)
                z_parts += [mask(z_2[:hq], j), mask(z_2[hq:], j)]
        z = jnp.concatenate(z_parts, axis=0)
        s = softplus2(z)
        sums = jnp.dot(s.astype(BF16), newer, preferred_element_type=F32)
        total = jnp.broadcast_to(jnp.sum(s, axis=1, keepdims=True), (s.shape[0], LANES))
        arg = (z - s) - sums
        if mode == "older":
            car = car_ref[car_rows, :]
            arg = arg - jnp.concatenate([car] * (win // LANES), axis=1)
            total = car + total
        car_ref[car_rows, :] = total
        w = jnp.exp2(arg).astype(BF16)

        i = 0
        for j in subs:
            rows = slice(j * hq, (j + 1) * hq)
            for p in pairs:
                o_2 = jnp.dot(w[i * hq:(i + 2) * hq], v_ws[j][:, pair_cols[p]],
                              preferred_element_type=F32)
                i += 2
                o_p = jnp.where(first_head, o_2[:hq], o_2[hq:])
                if mode == "older":
                    acc_ref[rows, pair_cols[p]] += o_p
                else:
                    acc_ref[rows, pair_cols[p]] = o_p
        return (jnp.min(total) >= SURVIVAL_UNDERFLOW_LOG2).astype(jnp.int32)

    done = lax.cond(q_start + hq >= win, lambda: visit(0, "inner"), lambda: visit(0, "edge"))

    def cond(carry):
        m, done = carry
        return jnp.logical_and(q_start + tq - m * win > 0, done == 0)

    def body(carry):
        m, _ = carry
        return m + 1, visit(m, "older")

    lax.while_loop(cond, body, (jnp.int32(1), done))
    o_ref[...] = acc_ref[...].astype(BF16)


def _attn(q_arr, q_col, k_arr, k_col, v_arr, v_col, *, batch, q_len, k_len, pos0, tq, hq):
    assert tq % hq == 0 and q_len % tq == 0 and pos0 % tq == 0 and ATTN_WINDOW % hq == 0
    assert pos0 + q_len <= k_len and k_len >= ATTN_WINDOW and hq % BF16_SUBLANES == 0
    nq = q_len // tq
    stacked_rows = (tq // hq) * SB_HEADS * hq
    return pl.pallas_call(
        functools.partial(_attn_kernel, tq=tq, hq=hq, pos0=pos0),
        grid=(batch, nq),
        in_specs=[
            pl.BlockSpec((tq, D_SB), lambda b, i: (b * nq + i, q_col)),
            pl.BlockSpec((k_len, D_SB), lambda b, i: (b, k_col)),
            pl.BlockSpec((k_len, D_SB), lambda b, i: (b, v_col)),
        ],
        out_specs=pl.BlockSpec((tq, D_SB), lambda b, i: (b * nq + i, 0)),
        out_shape=jax.ShapeDtypeStruct((batch * q_len, D_SB), BF16),
        scratch_shapes=[pltpu.VMEM((tq, D_SB), F32), pltpu.VMEM((stacked_rows, LANES), F32)],
        compiler_params=pltpu.CompilerParams(
            dimension_semantics=("parallel", "arbitrary"), vmem_limit_bytes=VMEM_LIMIT_BYTES),
        name="attn",
    )(q_arr, k_arr, v_arr)


def _merge_kernel(x_ref, u_ref, halo_ref, o_ref, gate_ref, pw_ref, ps_ref, wbp_ref, wbs_ref,
                  wout_ref, y_ref, *copy_refs, tm, pos0, tiles_per_seq):
    def weight(ref, position, *idx):
        w = ref[idx] if idx else ref[...]
        if copy_refs:
            w = w.astype(BF16)
            if idx:
                copy_refs[position][idx] = w
            else:
                copy_refs[position][...] = w
        return w

    w_sb = weight(wbs_ref, 2)
    streams = halo_ref.shape[0]
    rows = tm // streams
    if streams == 1:
        first_pos = pos0 + (pl.program_id(0) % tiles_per_seq) * tm
        pos = first_pos + lax.broadcasted_iota(jnp.int32, (tm, 1), 0)
    else:
        pos = jnp.concatenate([pos0 + lax.broadcasted_iota(jnp.int32, (rows, 1), 0)] * streams,
                              axis=0)

    n_groups = len(POOL_WINDOWS)
    n_chunks = n_groups if tm % (BF16_SUBLANES * n_groups) == 0 else 1
    quarter = tm // n_chunks
    span = POOL_HALO + rows
    run = jnp.concatenate(
        [part for b in range(streams)
         for part in (halo_ref[b], u_ref[b * rows:(b + 1) * rows, :])], axis=0)
    width, mixed, gated_sb = 1, [], []
    for gi, win in enumerate(POOL_WINDOWS):
        while width < win:
            run = run + pltpu.roll(run, width, 0)
            width *= 2
        assert width == win, "pool windows must be ascending powers of two"
        cols = slice(gi * POOL_GROUP, (gi + 1) * POOL_GROUP)
        cnt = jnp.minimum(win, pos + 1).astype(F32)
        tot = jnp.concatenate([run[b * span + POOL_HALO:(b + 1) * span, 0:POOL_GROUP]
                               for b in range(streams)], axis=0)
        d = (tot / cnt - u_ref[:, cols]).astype(BF16)
        mixed.append(jnp.dot(d, weight(pw_ref, 0, gi), preferred_element_type=F32)
                     * ps_ref[:, cols])
        if gi + 1 < n_groups:
            run = run[:, POOL_GROUP:]
        if gi < n_chunks:
            part = slice(gi * quarter, (gi + 1) * quarter)
            br_sb = jnp.dot(o_ref[part, :], w_sb, preferred_element_type=F32)
            gated_sb.append(
                jax.nn.sigmoid(gate_ref[part, D_MODEL:2 * D_MODEL].astype(F32)) * br_sb)
    a = jnp.concatenate(mixed, axis=1).astype(BF16)

    br_pool = jnp.dot(a, weight(wbp_ref, 1), preferred_element_type=F32)
    merged = (jax.nn.sigmoid(gate_ref[:, 0:D_MODEL].astype(F32)) * br_pool
              + jnp.concatenate(gated_sb, axis=0))
    y_ref[...] = x_ref[...] + jnp.dot(merged.astype(BF16), weight(wout_ref, 3),
                                      preferred_element_type=F32)


def _merge(x, u, halo, o, gate, pool_w, pool_scale, w_bp, w_bs, w_out, *, tm, pos0, seq):
    n = x.shape[0]
    const2 = lambda i: (0, 0)
    resident = pl.Buffered(1)
    streams = max(tm // seq, 1)
    assert tm % seq == 0 or seq % tm == 0
    weights = (pool_w, w_bp, w_bs, w_out)
    cast = pool_w.dtype == F32
    assert not cast or n == tm
    out_specs = [pl.BlockSpec((tm, D_MODEL), lambda i: (i, 0))]
    out_shape = [jax.ShapeDtypeStruct((n, D_MODEL), F32)]
    if cast:
        out_specs += [pl.BlockSpec(m.shape, lambda i, nd=m.ndim: (0,) * nd) for m in weights]
        out_shape += [jax.ShapeDtypeStruct(m.shape, BF16) for m in weights]
    return pl.pallas_call(
        functools.partial(_merge_kernel, tm=tm, pos0=pos0, tiles_per_seq=max(seq // tm, 1)),
        grid=(n // tm,),
        in_specs=[
            pl.BlockSpec((tm, D_MODEL), lambda i: (i, 0)),
            pl.BlockSpec((tm, D_POOL), lambda i: (i, 0)),
            pl.BlockSpec((streams, POOL_HALO, D_POOL), lambda i: (i, 0, 0)),
            pl.BlockSpec((tm, D_SB), lambda i: (i, 0)),
            pl.BlockSpec((tm, 2 * D_MODEL), lambda i: (i, 0)),
            pl.BlockSpec((len(POOL_WINDOWS), POOL_GROUP, POOL_GROUP), lambda i: (0, 0, 0)),
            pl.BlockSpec((1, D_POOL), const2),
            pl.BlockSpec((D_POOL, D_MODEL), const2, pipeline_mode=resident),
            pl.BlockSpec((D_SB, D_MODEL), const2, pipeline_mode=resident),
            pl.BlockSpec((D_MODEL, D_MODEL), const2, pipeline_mode=resident),
        ],
        out_specs=out_specs,
        out_shape=out_shape,
        compiler_params=pltpu.CompilerParams(
            dimension_semantics=("parallel",), vmem_limit_bytes=VMEM_LIMIT_BYTES),
        name="merge_cast" if cast else "merge",
    )(x, u, halo, o, gate, pool_w, pool_scale, w_bp, w_bs, w_out)


def _layer(x, w, *, batch, seq, pos0, k_past, v_past, pool_prefix, tm_ffn, tm_mix, tm_merge, tq,
           hq):
    assert seq >= POOL_STATE
    w_bf16 = {}

    def ffn(x_in, prefix, final_g=None):
        names = [prefix + "_gate", prefix + "_up", prefix + "_down"]
        res = _ffn(x_in, w[prefix + "_norm"], *[w[nm] for nm in names], final_g,
                   tm=tm_ffn, tf=FFN_CHUNK)
        if isinstance(res, (list, tuple)):
            w_bf16.update(zip(names, res[1:]))
            return res[0]
        return res

    x1 = ffn(x, "ffn1")
    u, qkv, k_hm, v_hm, gate, *w_in_copy = _proj(x1, w["mix_norm"], w["w_in"], batch=batch,
                                                 seq=seq, tm=tm_mix)
    w_bf16.update(zip(["w_in"], w_in_copy))

    if k_past is None:
        o = _attn(qkv, 0, qkv, 1, qkv, 2, batch=batch, q_len=seq, k_len=seq, pos0=0, tq=tq, hq=hq)
    else:
        def with_past(past, col):
            new = qkv[:, col * D_SB:(col + 1) * D_SB].reshape(batch, seq, D_SB)
            return jnp.concatenate([past, new], axis=1).reshape(batch * (pos0 + seq), D_SB)
        o = _attn(qkv, 0, with_past(k_past, 1), 0, with_past(v_past, 2), 0,
                  batch=batch, q_len=seq, k_len=pos0 + seq, pos0=pos0, tq=tq, hq=hq)

    piece = min(tm_merge, seq)
    pieces = seq // piece
    u4 = u.reshape(batch, pieces, piece, D_POOL)
    first = jnp.pad(pool_prefix, ((0, 0), (1, 0), (0, 0)))[:, None]
    halo = jnp.concatenate([first, u4[:, :-1, piece - POOL_HALO:, :]], axis=1)
    halo = halo.reshape(batch * pieces, POOL_HALO, D_POOL)

    merge_names = ["pool_w", "w_branch_pool", "w_branch_sb", "w_out"]
    x2, *copies = _merge(x1, u, halo, o, gate, w["pool_w"], w["pool_scale"], w["w_branch_pool"],
                         w["w_branch_sb"], w["w_out"], tm=tm_merge, pos0=pos0, seq=seq)
    w_bf16.update(zip(merge_names, copies))
    y = ffn(x2, "ffn2", w["final_norm"])
    new_pool = u.reshape(batch, seq, D_POOL)[:, seq - POOL_STATE:, :]
    return (y.reshape(batch, seq, D_MODEL), k_hm, v_hm, new_pool), w_bf16


def kernel(x_prompt, x_sample, cache_k, cache_v, state_pool, ffn1_norm, ffn1_gate, ffn1_up, ffn1_down, mix_norm, w_in, pool_w, pool_scale, w_branch_pool, w_branch_sb, w_out, ffn2_norm, ffn2_gate, ffn2_up, ffn2_down, final_norm):
    assert ffn1_norm.shape[0] == 1, "single-layer kernel"
    w = dict(w_in=w_in[0], pool_w=pool_w[0], w_branch_pool=w_branch_pool[0],
             w_branch_sb=w_branch_sb[0], w_out=w_out[0],
             ffn1_gate=ffn1_gate[0], ffn1_up=ffn1_up[0], ffn1_down=ffn1_down[0],
             ffn2_gate=ffn2_gate[0], ffn2_up=ffn2_up[0], ffn2_down=ffn2_down[0])
    w.update(ffn1_norm=ffn1_norm, mix_norm=mix_norm, pool_scale=pool_scale, ffn2_norm=ffn2_norm,
             final_norm=final_norm.reshape(1, D_MODEL))

    b_p, s_p, _ = x_prompt.shape
    b_d, s_d, _ = x_sample.shape
    past = cache_k.shape[3]

    def token_major(cache):
        return cache[0].transpose(0, 2, 1, 3).reshape(b_d, past, D_SB).astype(BF16)

    (y_d, k_d, v_d, pool_d), w_bf16 = _layer(
        x_sample.reshape(b_d * s_d, D_MODEL), w, batch=b_d, seq=s_d, pos0=past,
        k_past=token_major(cache_k), v_past=token_major(cache_v), pool_prefix=state_pool[0],
        tm_ffn=b_d * s_d, tm_mix=b_d * s_d, tm_merge=b_d * s_d, tq=s_d, hq=s_d)

    (y_p, k_p, v_p, pool_p), _ = _layer(
        x_prompt.reshape(b_p * s_p, D_MODEL), {**w, **w_bf16}, batch=b_p, seq=s_p, pos0=0,
        k_past=None, v_past=None, pool_prefix=jnp.zeros((b_p, POOL_STATE, D_POOL), F32),
        tm_ffn=1024, tm_mix=512, tm_merge=1024, tq=512, hq=64)

    return (y_p, y_d, k_p[None], v_p[None], pool_p[None], k_d[None], v_d[None], pool_d[None])
```

```python
import functools

import jax
import jax.numpy as jnp
from jax import lax
from jax.experimental import pallas as pl
from jax.experimental.pallas import tpu as pltpu

F32 = jnp.float32
BF16 = jnp.bfloat16

D_MODEL = 1024
D_FF = 4 * D_MODEL
D_POOL = D_MODEL // 2
POOL_WINDOWS = (2, 4, 8, 16)
POOL_GROUP = D_POOL // len(POOL_WINDOWS)
POOL_STATE = max(POOL_WINDOWS) - 1
POOL_HALO = POOL_STATE + 1
SB_HEADS = 8
SB_HEAD_DIM = 64
D_SB = SB_HEADS * SB_HEAD_DIM
D_IN = D_POOL + 3 * D_SB + 2 * D_MODEL
RMS_EPS = 1e-6
LANES = 128
BF16_SUBLANES = 16
VMEM_LIMIT_BYTES = 56 * 1024 * 1024
FFN_CHUNK = 1024
ATTN_WINDOW = 2 * LANES

LOG2E = 1.4426950408889634
SURVIVAL_UNDERFLOW_LOG2 = 150.0
MASKED_LOGIT = -1e30


def _rmsnorm(x, g):
    ms = jnp.mean(x * x, axis=-1, keepdims=True)
    return (x * lax.rsqrt(ms + RMS_EPS)) * g


def _ffn_kernel(*refs, n_ff, final, cast):
    refs = list(refs)
    x_ref, xnext_ref, g_ref, wg_ref, wu_ref, wd_ref = refs[:6]
    del refs[:6]
    fg_ref = refs.pop(0) if final else None
    o_ref = refs.pop(0)
    if cast:
        wg_out, wu_out, wd_out = refs[:3]
        del refs[:3]
    xn_ref, acc_ref = refs
    i = pl.program_id(0)
    j = pl.program_id(1)
    slot = i % 2

    @pl.when(jnp.logical_and(i == 0, j == 0))
    def _():
        xn_ref[0] = _rmsnorm(x_ref[...], g_ref[...]).astype(BF16)

    def chunk():
        wg, wu, wd = wg_ref[...], wu_ref[...], wd_ref[...]
        if cast:
            wg, wu, wd = wg.astype(BF16), wu.astype(BF16), wd.astype(BF16)
            wg_out[...] = wg
            wu_out[...] = wu
            wd_out[...] = wd
        xn = xn_ref[slot]
        gate = jnp.dot(xn, wg, preferred_element_type=F32)
        up = jnp.dot(xn, wu, preferred_element_type=F32)
        h = (gate * jax.nn.sigmoid(gate) * up).astype(BF16)
        return jnp.dot(h, wd, preferred_element_type=F32)

    def prepare_next(rows):
        xn_ref[1 - slot, rows, :] = _rmsnorm(xnext_ref[rows, :], g_ref[...]).astype(BF16)

    piece = x_ref.shape[0] // n_ff

    def first():
        acc_ref[...] = chunk()
        prepare_next(pl.ds(0, piece))

    def middle():
        acc_ref[...] += chunk()
        prepare_next(pl.ds(pl.multiple_of(j * piece, piece), piece))

    def last():
        y = x_ref[...] + 0.5 * (acc_ref[...] + chunk())
        if final:
            y = _rmsnorm(y, fg_ref[...])
        o_ref[...] = y
        prepare_next(pl.ds((n_ff - 1) * piece, piece))

    case = jnp.where(j == 0, 0, jnp.where(j == n_ff - 1, 2, 1))
    lax.switch(case, [first, middle, last])


def _ffn(x, norm_g, wg, wu, wd, final_g=None, *, tm, tf):
    n = x.shape[0]
    n_rows = n // tm
    n_ff = D_FF // tf
    assert n_ff >= 2 and tm % (BF16_SUBLANES * n_ff) == 0
    final = final_g is not None
    cast = wg.dtype == F32
    assert not cast or n_rows == 1
    row = pl.BlockSpec((1, D_MODEL), lambda i, j: (0, 0))
    w_specs = [
        pl.BlockSpec((D_MODEL, tf), lambda i, j: (0, j)),
        pl.BlockSpec((D_MODEL, tf), lambda i, j: (0, j)),
        pl.BlockSpec((tf, D_MODEL), lambda i, j: (j, 0)),
    ]
    in_specs = [
        pl.BlockSpec((tm, D_MODEL), lambda i, j: (i, 0)),
        pl.BlockSpec((tm, D_MODEL), lambda i, j: (jnp.minimum(i + 1, n_rows - 1), 0)),
        row,
    ] + w_specs
    args = [x, x, norm_g, wg, wu, wd]
    if final:
        in_specs.append(row)
        args.append(final_g)
    out_specs = [pl.BlockSpec((tm, D_MODEL), lambda i, j: (i, 0))]
    out_shape = [jax.ShapeDtypeStruct((n, D_MODEL), F32)]
    if cast:
        out_specs += w_specs
        out_shape += [jax.ShapeDtypeStruct(m.shape, BF16) for m in (wg, wu, wd)]
    outs = pl.pallas_call(
        functools.partial(_ffn_kernel, n_ff=n_ff, final=final, cast=cast),
        grid=(n_rows, n_ff),
        in_specs=in_specs,
        out_specs=out_specs,
        out_shape=out_shape,
        scratch_shapes=[pltpu.VMEM((2, tm, D_MODEL), BF16), pltpu.VMEM((tm, D_MODEL), F32)],
        compiler_params=pltpu.CompilerParams(
            dimension_semantics=("arbitrary", "arbitrary"), vmem_limit_bytes=VMEM_LIMIT_BYTES),
        name=("ffn_final" if final else "ffn") + ("_cast" if cast else ""),
    )(*args)
    return outs if cast else outs[0]


def _proj_kernel(x_ref, g_ref, w_ref, u_ref, qkv_ref, k_ref, v_ref, gate_ref, w_out_ref=None):
    h = _rmsnorm(x_ref[...], g_ref[...]).astype(BF16)

    def mm(c0, c1):
        w = w_ref[:, c0:c1]
        if w_out_ref is not None:
            w = w.astype(BF16)
            w_out_ref[:, c0:c1] = w
        return jnp.dot(h, w, preferred_element_type=F32)

    c = 0
    u_ref[...] = mm(c, c + D_POOL)
    c += D_POOL
    qkv_ref[:, 0:D_SB] = (mm(c, c + D_SB) * (LOG2E * SB_HEAD_DIM ** -0.5)).astype(BF16)
    c += D_SB
    for idx, hm_ref in ((1, k_ref), (2, v_ref)):
        t = mm(c, c + D_SB)
        c += D_SB
        qkv_ref[:, idx * D_SB:(idx + 1) * D_SB] = t.astype(BF16)
        streams, _, rows, _ = hm_ref.shape
        for b in range(streams):
            for hd in range(SB_HEADS):
                hm_ref[b, hd, :, :] = t[b * rows:(b + 1) * rows,
                                        hd * SB_HEAD_DIM:(hd + 1) * SB_HEAD_DIM]
    gate_ref[:, 0:D_MODEL] = mm(c, c + D_MODEL).astype(BF16)
    c += D_MODEL
    gate_ref[:, D_MODEL:2 * D_MODEL] = mm(c, c + D_MODEL).astype(BF16)


def _proj(x, norm_g, w_in, *, batch, seq, tm):
    n = x.shape[0]
    cast = w_in.dtype == F32
    assert not cast or n == tm
    if tm <= seq:
        tiles_per_seq = seq // tm
        hm_spec = pl.BlockSpec((1, SB_HEADS, tm, SB_HEAD_DIM),
                               lambda i: (i // tiles_per_seq, 0, i % tiles_per_seq, 0))
    else:
        assert tm % seq == 0
        hm_spec = pl.BlockSpec((tm // seq, SB_HEADS, seq, SB_HEAD_DIM), lambda i: (i, 0, 0, 0))
    hm_shape = jax.ShapeDtypeStruct((batch, SB_HEADS, seq, SB_HEAD_DIM), F32)
    w_spec = pl.BlockSpec((D_MODEL, D_IN), lambda i: (0, 0), pipeline_mode=pl.Buffered(1))
    out_specs = [
        pl.BlockSpec((tm, D_POOL), lambda i: (i, 0)),
        pl.BlockSpec((tm, 3 * D_SB), lambda i: (i, 0)),
        hm_spec,
        hm_spec,
        pl.BlockSpec((tm, 2 * D_MODEL), lambda i: (i, 0)),
    ]
    out_shape = [
        jax.ShapeDtypeStruct((n, D_POOL), F32),
        jax.ShapeDtypeStruct((n, 3 * D_SB), BF16),
        hm_shape,
        hm_shape,
        jax.ShapeDtypeStruct((n, 2 * D_MODEL), BF16),
    ]
    if cast:
        out_specs.append(pl.BlockSpec((D_MODEL, D_IN), lambda i: (0, 0)))
        out_shape.append(jax.ShapeDtypeStruct((D_MODEL, D_IN), BF16))
    return pl.pallas_call(
        _proj_kernel,
        grid=(n // tm,),
        in_specs=[
            pl.BlockSpec((tm, D_MODEL), lambda i: (i, 0)),
            pl.BlockSpec((1, D_MODEL), lambda i: (0, 0)),
            w_spec,
        ],
        out_specs=out_specs,
        out_shape=out_shape,
        compiler_params=pltpu.CompilerParams(
            dimension_semantics=("parallel",), vmem_limit_bytes=VMEM_LIMIT_BYTES),
        name="proj_cast" if cast else "proj",
    )(x, norm_g, w_in)


def _attn_kernel(q_ref, k_ref, v_ref, o_ref, acc_ref, car_ref, *, tq, hq, pos0):
    win = ATTN_WINDOW
    q_start = pos0 + pl.program_id(1) * tq
    subs = range(tq // hq)
    pairs = range(SB_HEADS // 2)
    pair_cols = [slice(p * LANES, (p + 1) * LANES) for p in pairs]

    lane = lax.broadcasted_iota(jnp.int32, (1, LANES), 1)
    first_head = lane < SB_HEAD_DIM
    r = lax.broadcasted_iota(jnp.int32, (win, win), 0)
    c = lax.broadcasted_iota(jnp.int32, (win, win), 1)
    newer = jnp.where(r > c, 1.0, 0.0).astype(BF16)

    def softplus2(z):
        return jnp.maximum(z, 0.0) + jnp.log2(1.0 + jnp.exp2(-jnp.abs(z)))

    def visit(m, mode):
        k_ws, v_ws, valids = [], [], []
        for j in subs:
            row0 = q_start + j * hq
            win_end = row0 + hq - m * win
            q_pos = row0 + lax.broadcasted_iota(jnp.int32, (hq, 1), 0)
            if mode == "inner":
                start = pl.multiple_of(win_end - win, hq)
                k_pos = (win_end - LANES) + lax.broadcasted_iota(jnp.int32, (1, LANES), 1)
                valids.append(k_pos < q_pos)
            else:
                start = pl.multiple_of(jnp.maximum(win_end - win, 0), hq)
                k_pos = start + lax.broadcasted_iota(jnp.int32, (1, win), 1)
                valids.append(k_pos < (q_pos if mode == "edge" else win_end))
            k_ws.append(k_ref[pl.ds(start, win), :])
            v_ws.append(v_ref[pl.ds(start, win), :])

        def mask(x, j):
            if mode == "inner":
                return jnp.concatenate(
                    [x[:, :win - LANES],
                     jnp.where(valids[j], x[:, win - LANES:], MASKED_LOGIT)], axis=1)
            return jnp.where(valids[j], x, MASKED_LOGIT)

        z_parts = []
        for j in subs:
            rows = slice(j * hq, (j + 1) * hq)
            for p in pairs:
                q_p = q_ref[rows, pair_cols[p]]
                zero = jnp.zeros_like(q_p)
                q_2 = jnp.concatenate([jnp.where(first_head, q_p, zero),
                                       jnp.where(first_head, zero, q_p)], axis=0)
                z_2 = lax.dot_general(q_2, k_ws[j][:, pair_cols[p]], (((1,), (1,)), ((), ())),
                                      preferred_element_type=F32)
                z_parts += [mask(z_2[:hq], j), mask(z_2[hq:], j)]
        z = jnp.concatenate(z_parts, axis=0)
        s = softplus2(z)
        sums = jnp.dot(s.astype(BF16), newer, preferred_element_type=F32)
        total = jnp.broadcast_to(jnp.sum(s, axis=1, keepdims=True), (s.shape[0], LANES))
        arg = (z - s) - sums
        if mode == "older":
            car = car_ref[...]
            arg = arg - jnp.concatenate([car] * (win // LANES), axis=1)
            total = car + total
        car_ref[...] = total
        w = jnp.exp2(arg).astype(BF16)

        i = 0
        for j in subs:
            rows = slice(j * hq, (j + 1) * hq)
            for p in pairs:
                o_2 = jnp.dot(w[i * hq:(i + 2) * hq], v_ws[j][:, pair_cols[p]],
                              preferred_element_type=F32)
                i += 2
                o_p = jnp.where(first_head, o_2[:hq], o_2[hq:])
                if mode == "older":
                    acc_ref[rows, pair_cols[p]] += o_p
                else:
                    acc_ref[rows, pair_cols[p]] = o_p
        return (jnp.min(total) >= SURVIVAL_UNDERFLOW_LOG2).astype(jnp.int32)

    done = lax.cond(q_start + hq >= win, lambda: visit(0, "inner"), lambda: visit(0, "edge"))

    def cond(carry):
        m, done = carry
        return jnp.logical_and(q_start + tq - m * win > 0, done == 0)

    def body(carry):
        m, _ = carry
        return m + 1, visit(m, "older")

    lax.while_loop(cond, body, (jnp.int32(1), done))
    o_ref[...] = acc_ref[...].astype(BF16)


def _attn(q_arr, q_col, k_arr, k_col, v_arr, v_col, *, batch, q_len, k_len, pos0, tq, hq):
    assert tq % hq == 0 and q_len % tq == 0 and pos0 % tq == 0 and ATTN_WINDOW % hq == 0
    assert pos0 + q_len <= k_len and k_len >= ATTN_WINDOW and hq % BF16_SUBLANES == 0
    nq = q_len // tq
    stacked_rows = (tq // hq) * SB_HEADS * hq
    return pl.pallas_call(
        functools.partial(_attn_kernel, tq=tq, hq=hq, pos0=pos0),
        grid=(batch, nq),
        in_specs=[
            pl.BlockSpec((tq, D_SB), lambda b, i: (b * nq + i, q_col)),
            pl.BlockSpec((k_len, D_SB), lambda b, i: (b, k_col)),
            pl.BlockSpec((k_len, D_SB), lambda b, i: (b, v_col)),
        ],
        out_specs=pl.BlockSpec((tq, D_SB), lambda b, i: (b * nq + i, 0)),
        out_shape=jax.ShapeDtypeStruct((batch * q_len, D_SB), BF16),
        scratch_shapes=[pltpu.VMEM((tq, D_SB), F32), pltpu.VMEM((stacked_rows, LANES), F32)],
        compiler_params=pltpu.CompilerParams(
            dimension_semantics=("parallel", "arbitrary"), vmem_limit_bytes=VMEM_LIMIT_BYTES),
        name="attn",
    )(q_arr, k_arr, v_arr)


def _merge_kernel(x_ref, u_ref, halo_ref, o_ref, gate_ref, pw_ref, ps_ref, wbp_ref, wbs_ref,
                  wout_ref, y_ref, *copy_refs, tm, pos0, tiles_per_seq):
    def weight(ref, position, *idx):
        w = ref[idx] if idx else ref[...]
        if copy_refs:
            w = w.astype(BF16)
            if idx:
                copy_refs[position][idx] = w
            else:
                copy_refs[position][...] = w
        return w

    w_sb = weight(wbs_ref, 2)
    streams = halo_ref.shape[0]
    rows = tm // streams
    if streams == 1:
        first_pos = pos0 + (pl.program_id(0) % tiles_per_seq) * tm
        pos = first_pos + lax.broadcasted_iota(jnp.int32, (tm, 1), 0)
    else:
        pos = jnp.concatenate([pos0 + lax.broadcasted_iota(jnp.int32, (rows, 1), 0)] * streams,
                              axis=0)

    n_groups = len(POOL_WINDOWS)
    n_chunks = n_groups if tm % (BF16_SUBLANES * n_groups) == 0 else 1
    quarter = tm // n_chunks
    span = POOL_HALO + rows
    run = jnp.concatenate(
        [part for b in range(streams)
         for part in (halo_ref[b], u_ref[b * rows:(b + 1) * rows, :])], axis=0)
    width, mixed, gated_sb = 1, [], []
    for gi, win in enumerate(POOL_WINDOWS):
        while width < win:
            run = run + pltpu.roll(run, width, 0)
            width *= 2
        assert width == win, "pool windows must be ascending powers of two"
        cols = slice(gi * POOL_GROUP, (gi + 1) * POOL_GROUP)
        cnt = jnp.minimum(win, pos + 1).astype(F32)
        tot = jnp.concatenate([run[b * span + POOL_HALO:(b + 1) * span, 0:POOL_GROUP]
                               for b in range(streams)], axis=0)
        d = (tot / cnt - u_ref[:, cols]).astype(BF16)
        mixed.append(jnp.dot(d, weight(pw_ref, 0, gi), preferred_element_type=F32)
                     * ps_ref[:, cols])
        if gi + 1 < n_groups:
            run = run[:, POOL_GROUP:]
        if gi < n_chunks:
            part = slice(gi * quarter, (gi + 1) * quarter)
            br_sb = jnp.dot(o_ref[part, :], w_sb, preferred_element_type=F32)
            gated_sb.append(
                jax.nn.sigmoid(gate_ref[part, D_MODEL:2 * D_MODEL].astype(F32)) * br_sb)
    a = jnp.concatenate(mixed, axis=1).astype(BF16)

    br_pool = jnp.dot(a, weight(wbp_ref, 1), preferred_element_type=F32)
    merged = (jax.nn.sigmoid(gate_ref[:, 0:D_MODEL].astype(F32)) * br_pool
              + jnp.concatenate(gated_sb, axis=0))
    y_ref[...] = x_ref[...] + jnp.dot(merged.astype(BF16), weight(wout_ref, 3),
                                      preferred_element_type=F32)


def _merge(x, u, halo, o, gate, pool_w, pool_scale, w_bp, w_bs, w_out, *, tm, pos0, seq):
    n = x.shape[0]
    const2 = lambda i: (0, 0)
    resident = pl.Buffered(1)
    streams = max(tm // seq, 1)
    assert tm % seq == 0 or seq % tm == 0
    weights = (pool_w, w_bp, w_bs, w_out)
    cast = pool_w.dtype == F32
    assert not cast or n == tm
    out_specs = [pl.BlockSpec((tm, D_MODEL), lambda i: (i, 0))]
    out_shape = [jax.ShapeDtypeStruct((n, D_MODEL), F32)]
    if cast:
        out_specs += [pl.BlockSpec(m.shape, lambda i, nd=m.ndim: (0,) * nd) for m in weights]
        out_shape += [jax.ShapeDtypeStruct(m.shape, BF16) for m in weights]
    return pl.pallas_call(
        functools.partial(_merge_kernel, tm=tm, pos0=pos0, tiles_per_seq=max(seq // tm, 1)),
        grid=(n // tm,),
        in_specs=[
            pl.BlockSpec((tm, D_MODEL), lambda i: (i, 0)),
            pl.BlockSpec((tm, D_POOL), lambda i: (i, 0)),
            pl.BlockSpec((streams, POOL_HALO, D_POOL), lambda i: (i, 0, 0)),
            pl.BlockSpec((tm, D_SB), lambda i: (i, 0)),
            pl.BlockSpec((tm, 2 * D_MODEL), lambda i: (i, 0)),
            pl.BlockSpec((len(POOL_WINDOWS), POOL_GROUP, POOL_GROUP), lambda i: (0, 0, 0)),
            pl.BlockSpec((1, D_POOL), const2),
            pl.BlockSpec((D_POOL, D_MODEL), const2, pipeline_mode=resident),
            pl.BlockSpec((D_SB, D_MODEL), const2, pipeline_mode=resident),
            pl.BlockSpec((D_MODEL, D_MODEL), const2, pipeline_mode=resident),
        ],
        out_specs=out_specs,
        out_shape=out_shape,
        compiler_params=pltpu.CompilerParams(
            dimension_semantics=("parallel",), vmem_limit_bytes=VMEM_LIMIT_BYTES),
        name="merge_cast" if cast else "merge",
    )(x, u, halo, o, gate, pool_w, pool_scale, w_bp, w_bs, w_out)


def _layer(x, w, *, batch, seq, pos0, k_past, v_past, pool_prefix, tm_ffn, tm_mix, tm_merge, tq,
           hq):
    assert seq >= POOL_STATE
    w_bf16 = {}

    def ffn(x_in, prefix, final_g=None):
        names = [prefix + "_gate", prefix + "_up", prefix + "_down"]
        res = _ffn(x_in, w[prefix + "_norm"], *[w[nm] for nm in names], final_g,
                   tm=tm_ffn, tf=FFN_CHUNK)
        if isinstance(res, (list, tuple)):
            w_bf16.update(zip(names, res[1:]))
            return res[0]
        return res

    x1 = ffn(x, "ffn1")
    u, qkv, k_hm, v_hm, gate, *w_in_copy = _proj(x1, w["mix_norm"], w["w_in"], batch=batch,
                                                 seq=seq, tm=tm_mix)
    w_bf16.update(zip(["w_in"], w_in_copy))

    if k_past is None:
        o = _attn(qkv, 0, qkv, 1, qkv, 2, batch=batch, q_len=seq, k_len=seq, pos0=0, tq=tq, hq=hq)
    else:
        def with_past(past, col):
            new = qkv[:, col * D_SB:(col + 1) * D_SB].reshape(batch, seq, D_SB)
            return jnp.concatenate([past, new], axis=1).reshape(batch * (pos0 + seq), D_SB)
        o = _attn(qkv, 0, with_past(k_past, 1), 0, with_past(v_past, 2), 0,
                  batch=batch, q_len=seq, k_len=pos0 + seq, pos0=pos0, tq=tq, hq=hq)

    piece = min(tm_merge, seq)
    pieces = seq // piece
    u4 = u.reshape(batch, pieces, piece, D_POOL)
    first = jnp.pad(pool_prefix, ((0, 0), (1, 0), (0, 0)))[:, None]
    halo = jnp.concatenate([first, u4[:, :-1, piece - POOL_HALO:, :]], axis=1)
    halo = halo.reshape(batch * pieces, POOL_HALO, D_POOL)

    merge_names = ["pool_w", "w_branch_pool", "w_branch_sb", "w_out"]
    x2, *copies = _merge(x1, u, halo, o, gate, w["pool_w"], w["pool_scale"], w["w_branch_pool"],
                         w["w_branch_sb"], w["w_out"], tm=tm_merge, pos0=pos0, seq=seq)
    w_bf16.update(zip(merge_names, copies))
    y = ffn(x2, "ffn2", w["final_norm"])
    new_pool = u.reshape(batch, seq, D_POOL)[:, seq - POOL_STATE:, :]
    return (y.reshape(batch, seq, D_MODEL), k_hm, v_hm, new_pool), w_bf16


def kernel(x_prompt, x_sample, cache_k, cache_v, state_pool, ffn1_norm, ffn1_gate, ffn1_up, ffn1_down, mix_norm, w_in, pool_w, pool_scale, w_branch_pool, w_branch_sb, w_out, ffn2_norm, ffn2_gate, ffn2_up, ffn2_down, final_norm):
    assert ffn1_norm.shape[0] == 1, "single-layer kernel"
    w = dict(w_in=w_in[0], pool_w=pool_w[0], w_branch_pool=w_branch_pool[0],
             w_branch_sb=w_branch_sb[0], w_out=w_out[0],
             ffn1_gate=ffn1_gate[0], ffn1_up=ffn1_up[0], ffn1_down=ffn1_down[0],
             ffn2_gate=ffn2_gate[0], ffn2_up=ffn2_up[0], ffn2_down=ffn2_down[0])
    w.update(ffn1_norm=ffn1_norm, mix_norm=mix_norm, pool_scale=pool_scale, ffn2_norm=ffn2_norm,
             final_norm=final_norm.reshape(1, D_MODEL))

    b_p, s_p, _ = x_prompt.shape
    b_d, s_d, _ = x_sample.shape
    past = cache_k.shape[3]

    def token_major(cache):
        return cache[0].transpose(0, 2, 1, 3).reshape(b_d, past, D_SB).astype(BF16)

    (y_d, k_d, v_d, pool_d), w_bf16 = _layer(
        x_sample.reshape(b_d * s_d, D_MODEL), w, batch=b_d, seq=s_d, pos0=past,
        k_past=token_major(cache_k), v_past=token_major(cache_v), pool_prefix=state_pool[0],
        tm_ffn=b_d * s_d, tm_mix=b_d * s_d, tm_merge=b_d * s_d, tq=s_d, hq=s_d)

    (y_p, k_p, v_p, pool_p), _ = _layer(
        x_prompt.reshape(b_p * s_p, D_MODEL), {**w, **w_bf16}, batch=b_p, seq=s_p, pos0=0,
        k_past=None, v_past=None, pool_prefix=jnp.zeros((b_p, POOL_STATE, D_POOL), F32),
        tm_ffn=1024, tm_mix=1024, tm_merge=1024, tq=512, hq=64)

    return (y_p, y_d, k_p[None], v_p[None], pool_p[None], k_d[None], v_d[None], pool_d[None])
```

```python
import functools

import jax
import jax.numpy as jnp
from jax import lax
from jax.experimental import pallas as pl
from jax.experimental.pallas import tpu as pltpu

F32 = jnp.float32
BF16 = jnp.bfloat16

D_MODEL = 1024
D_FF = 4 * D_MODEL
D_POOL = D_MODEL // 2
POOL_WINDOWS = (2, 4, 8, 16)
POOL_GROUP = D_POOL // len(POOL_WINDOWS)
POOL_STATE = max(POOL_WINDOWS) - 1
POOL_HALO = POOL_STATE + 1
SB_HEADS = 8
SB_HEAD_DIM = 64
D_SB = SB_HEADS * SB_HEAD_DIM
D_IN = D_POOL + 3 * D_SB + 2 * D_MODEL
RMS_EPS = 1e-6
LANES = 128
BF16_SUBLANES = 16
VMEM_LIMIT_BYTES = 56 * 1024 * 1024
FFN_CHUNK = 1024
ATTN_WINDOW = 2 * LANES

LOG2E = 1.4426950408889634
SURVIVAL_UNDERFLOW_LOG2 = 150.0
MASKED_LOGIT = -1e30


def _rmsnorm(x, g):
    ms = jnp.mean(x * x, axis=-1, keepdims=True)
    return (x * lax.rsqrt(ms + RMS_EPS)) * g


def _sigmoid(x):
    return 0.5 * jnp.tanh(0.5 * x) + 0.5


def _ffn_kernel(*refs, n_ff, final, cast):
    refs = list(refs)
    x_ref, xnext_ref, g_ref, wg_ref, wu_ref, wd_ref = refs[:6]
    del refs[:6]
    fg_ref = refs.pop(0) if final else None
    o_ref = refs.pop(0)
    if cast:
        wg_out, wu_out, wd_out = refs[:3]
        del refs[:3]
    xn_ref, acc_ref = refs
    i = pl.program_id(0)
    j = pl.program_id(1)
    slot = i % 2

    @pl.when(jnp.logical_and(i == 0, j == 0))
    def _():
        xn_ref[0] = _rmsnorm(x_ref[...], g_ref[...]).astype(BF16)

    def chunk():
        wg, wu, wd = wg_ref[...], wu_ref[...], wd_ref[...]
        if cast:
            wg, wu, wd = wg.astype(BF16), wu.astype(BF16), wd.astype(BF16)
            wg_out[...] = wg
            wu_out[...] = wu
            wd_out[...] = wd
        xn = xn_ref[slot]
        gate = jnp.dot(xn, wg, preferred_element_type=F32)
        up = jnp.dot(xn, wu, preferred_element_type=F32)
        h = (gate * _sigmoid(gate) * up).astype(BF16)
        return jnp.dot(h, wd, preferred_element_type=F32)

    def prepare_next(rows):
        xn_ref[1 - slot, rows, :] = _rmsnorm(xnext_ref[rows, :], g_ref[...]).astype(BF16)

    piece = x_ref.shape[0] // n_ff

    def first():
        acc_ref[...] = chunk()
        prepare_next(pl.ds(0, piece))

    def middle():
        acc_ref[...] += chunk()
        prepare_next(pl.ds(pl.multiple_of(j * piece, piece), piece))

    def last():
        y = x_ref[...] + 0.5 * (acc_ref[...] + chunk())
        if final:
            y = _rmsnorm(y, fg_ref[...])
        o_ref[...] = y
        prepare_next(pl.ds((n_ff - 1) * piece, piece))

    case = jnp.where(j == 0, 0, jnp.where(j == n_ff - 1, 2, 1))
    lax.switch(case, [first, middle, last])


def _ffn(x, norm_g, wg, wu, wd, final_g=None, *, tm, tf):
    n = x.shape[0]
    n_rows = n // tm
    n_ff = D_FF // tf
    assert n_ff >= 2 and tm % (BF16_SUBLANES * n_ff) == 0
    final = final_g is not None
    cast = wg.dtype == F32
    assert not cast or n_rows == 1
    row = pl.BlockSpec((1, D_MODEL), lambda i, j: (0, 0))
    w_specs = [
        pl.BlockSpec((D_MODEL, tf), lambda i, j: (0, j)),
        pl.BlockSpec((D_MODEL, tf), lambda i, j: (0, j)),
        pl.BlockSpec((tf, D_MODEL), lambda i, j: (j, 0)),
    ]
    in_specs = [
        pl.BlockSpec((tm, D_MODEL), lambda i, j: (i, 0)),
        pl.BlockSpec((tm, D_MODEL), lambda i, j: (jnp.minimum(i + 1, n_rows - 1), 0)),
        row,
    ] + w_specs
    args = [x, x, norm_g, wg, wu, wd]
    if final:
        in_specs.append(row)
        args.append(final_g)
    out_specs = [pl.BlockSpec((tm, D_MODEL), lambda i, j: (i, 0))]
    out_shape = [jax.ShapeDtypeStruct((n, D_MODEL), F32)]
    if cast:
        out_specs += w_specs
        out_shape += [jax.ShapeDtypeStruct(m.shape, BF16) for m in (wg, wu, wd)]
    outs = pl.pallas_call(
        functools.partial(_ffn_kernel, n_ff=n_ff, final=final, cast=cast),
        grid=(n_rows, n_ff),
        in_specs=in_specs,
        out_specs=out_specs,
        out_shape=out_shape,
        scratch_shapes=[pltpu.VMEM((2, tm, D_MODEL), BF16), pltpu.VMEM((tm, D_MODEL), F32)],
        compiler_params=pltpu.CompilerParams(
            dimension_semantics=("arbitrary", "arbitrary"), vmem_limit_bytes=VMEM_LIMIT_BYTES),
        name=("ffn_final" if final else "ffn") + ("_cast" if cast else ""),
    )(*args)
    return outs if cast else outs[0]


def _proj_kernel(x_ref, g_ref, w_ref, u_ref, qkv_ref, k_ref, v_ref, gate_ref, w_out_ref=None):
    h = _rmsnorm(x_ref[...], g_ref[...]).astype(BF16)

    def mm(c0, c1):
        w = w_ref[:, c0:c1]
        if w_out_ref is not None:
            w = w.astype(BF16)
            w_out_ref[:, c0:c1] = w
        return jnp.dot(h, w, preferred_element_type=F32)

    c = 0
    u_ref[...] = mm(c, c + D_POOL)
    c += D_POOL
    qkv_ref[:, 0:D_SB] = (mm(c, c + D_SB) * (LOG2E * SB_HEAD_DIM ** -0.5)).astype(BF16)
    c += D_SB
    for idx, hm_ref in ((1, k_ref), (2, v_ref)):
        t = mm(c, c + D_SB)
        c += D_SB
        qkv_ref[:, idx * D_SB:(idx + 1) * D_SB] = t.astype(BF16)
        streams, _, rows, _ = hm_ref.shape
        for b in range(streams):
            for hd in range(SB_HEADS):
                hm_ref[b, hd, :, :] = t[b * rows:(b + 1) * rows,
                                        hd * SB_HEAD_DIM:(hd + 1) * SB_HEAD_DIM]
    gate_ref[:, 0:D_MODEL] = mm(c, c + D_MODEL).astype(BF16)
    c += D_MODEL
    gate_ref[:, D_MODEL:2 * D_MODEL] = mm(c, c + D_MODEL).astype(BF16)


def _proj(x, norm_g, w_in, *, batch, seq, tm):
    n = x.shape[0]
    cast = w_in.dtype == F32
    assert not cast or n == tm
    if tm <= seq:
        tiles_per_seq = seq // tm
        hm_spec = pl.BlockSpec((1, SB_HEADS, tm, SB_HEAD_DIM),
                               lambda i: (i // tiles_per_seq, 0, i % tiles_per_seq, 0))
    else:
        assert tm % seq == 0
        hm_spec = pl.BlockSpec((tm // seq, SB_HEADS, seq, SB_HEAD_DIM), lambda i: (i, 0, 0, 0))
    hm_shape = jax.ShapeDtypeStruct((batch, SB_HEADS, seq, SB_HEAD_DIM), F32)
    w_spec = pl.BlockSpec((D_MODEL, D_IN), lambda i: (0, 0), pipeline_mode=pl.Buffered(1))
    out_specs = [
        pl.BlockSpec((tm, D_POOL), lambda i: (i, 0)),
        pl.BlockSpec((tm, 3 * D_SB), lambda i: (i, 0)),
        hm_spec,
        hm_spec,
        pl.BlockSpec((tm, 2 * D_MODEL), lambda i: (i, 0)),
    ]
    out_shape = [
        jax.ShapeDtypeStruct((n, D_POOL), F32),
        jax.ShapeDtypeStruct((n, 3 * D_SB), BF16),
        hm_shape,
        hm_shape,
        jax.ShapeDtypeStruct((n, 2 * D_MODEL), BF16),
    ]
    if cast:
        out_specs.append(pl.BlockSpec((D_MODEL, D_IN), lambda i: (0, 0)))
        out_shape.append(jax.ShapeDtypeStruct((D_MODEL, D_IN), BF16))
    return pl.pallas_call(
        _proj_kernel,
        grid=(n // tm,),
        in_specs=[
            pl.BlockSpec((tm, D_MODEL), lambda i: (i, 0)),
            pl.BlockSpec((1, D_MODEL), lambda i: (0, 0)),
            w_spec,
        ],
        out_specs=out_specs,
        out_shape=out_shape,
        compiler_params=pltpu.CompilerParams(
            dimension_semantics=("parallel",), vmem_limit_bytes=VMEM_LIMIT_BYTES),
        name="proj_cast" if cast else "proj",
    )(x, norm_g, w_in)


def _attn_kernel(q_ref, k_ref, v_ref, o_ref, acc_ref, car_ref, *, tq, hq, pos0):
    win = ATTN_WINDOW
    q_start = pos0 + pl.program_id(1) * tq
    subs = range(tq // hq)
    pairs = range(SB_HEADS // 2)
    pair_cols = [slice(p * LANES, (p + 1) * LANES) for p in pairs]

    lane = lax.broadcasted_iota(jnp.int32, (1, LANES), 1)
    first_head = lane < SB_HEAD_DIM
    r = lax.broadcasted_iota(jnp.int32, (win, win), 0)
    c = lax.broadcasted_iota(jnp.int32, (win, win), 1)
    newer = jnp.where(r > c, 1.0, 0.0).astype(BF16)

    def softplus2(z):
        return jnp.maximum(z, 0.0) + jnp.log2(1.0 + jnp.exp2(-jnp.abs(z)))

    def visit(m, mode):
        k_ws, v_ws, valids = [], [], []
        for j in subs:
            row0 = q_start + j * hq
            win_end = row0 + hq - m * win
            q_pos = row0 + lax.broadcasted_iota(jnp.int32, (hq, 1), 0)
            if mode == "inner":
                start = pl.multiple_of(win_end - win, hq)
                k_pos = (win_end - LANES) + lax.broadcasted_iota(jnp.int32, (1, LANES), 1)
                valids.append(k_pos < q_pos)
            else:
                start = pl.multiple_of(jnp.maximum(win_end - win, 0), hq)
                k_pos = start + lax.broadcasted_iota(jnp.int32, (1, win), 1)
                valids.append(k_pos < (q_pos if mode == "edge" else win_end))
            k_ws.append(k_ref[pl.ds(start, win), :])
            v_ws.append(v_ref[pl.ds(start, win), :])

        def mask(x, j):
            if mode == "inner":
                return jnp.concatenate(
                    [x[:, :win - LANES],
                     jnp.where(valids[j], x[:, win - LANES:], MASKED_LOGIT)], axis=1)
            return jnp.where(valids[j], x, MASKED_LOGIT)

        z_parts = []
        for j in subs:
            rows = slice(j * hq, (j + 1) * hq)
            for p in pairs:
                q_p = q_ref[rows, pair_cols[p]]
                zero = jnp.zeros_like(q_p)
                q_2 = jnp.concatenate([jnp.where(first_head, q_p, zero),
                                       jnp.where(first_head, zero, q_p)], axis=0)
                z_2 = lax.dot_general(q_2, k_ws[j][:, pair_cols[p]], (((1,), (1,)), ((), ())),
                                      preferred_element_type=F32)
                z_parts += [mask(z_2[:hq], j), mask(z_2[hq:], j)]
        z = jnp.concatenate(z_parts, axis=0)
        s = softplus2(z)
        sums = jnp.dot(s.astype(BF16), newer, preferred_element_type=F32)
        total = jnp.broadcast_to(jnp.sum(s, axis=1, keepdims=True), (s.shape[0], LANES))
        arg = (z - s) - sums
        if mode == "older":
            car = car_ref[...]
            arg = arg - jnp.concatenate([car] * (win // LANES), axis=1)
            total = car + total
        car_ref[...] = total
        w = jnp.exp2(arg).astype(BF16)

        i = 0
        for j in subs:
            rows = slice(j * hq, (j + 1) * hq)
            for p in pairs:
                o_2 = jnp.dot(w[i * hq:(i + 2) * hq], v_ws[j][:, pair_cols[p]],
                              preferred_element_type=F32)
                i += 2
                o_p = jnp.where(first_head, o_2[:hq], o_2[hq:])
                if mode == "older":
                    acc_ref[rows, pair_cols[p]] += o_p
                else:
                    acc_ref[rows, pair_cols[p]] = o_p
        return (jnp.min(total) >= SURVIVAL_UNDERFLOW_LOG2).astype(jnp.int32)

    done = lax.cond(q_start + hq >= win, lambda: visit(0, "inner"), lambda: visit(0, "edge"))

    def cond(carry):
        m, done = carry
        return jnp.logical_and(q_start + tq - m * win > 0, done == 0)

    def body(carry):
        m, _ = carry
        return m + 1, visit(m, "older")

    lax.while_loop(cond, body, (jnp.int32(1), done))
    o_ref[...] = acc_ref[...].astype(BF16)


def _attn(q_arr, q_col, k_arr, k_col, v_arr, v_col, *, batch, q_len, k_len, pos0, tq, hq):
    assert tq % hq == 0 and q_len % tq == 0 and pos0 % tq == 0 and ATTN_WINDOW % hq == 0
    assert pos0 + q_len <= k_len and k_len >= ATTN_WINDOW and hq % BF16_SUBLANES == 0
    nq = q_len // tq
    stacked_rows = (tq // hq) * SB_HEADS * hq
    return pl.pallas_call(
        functools.partial(_attn_kernel, tq=tq, hq=hq, pos0=pos0),
        grid=(batch, nq),
        in_specs=[
            pl.BlockSpec((tq, D_SB), lambda b, i: (b * nq + i, q_col)),
            pl.BlockSpec((k_len, D_SB), lambda b, i: (b, k_col)),
            pl.BlockSpec((k_len, D_SB), lambda b, i: (b, v_col)),
        ],
        out_specs=pl.BlockSpec((tq, D_SB), lambda b, i: (b * nq + i, 0)),
        out_shape=jax.ShapeDtypeStruct((batch * q_len, D_SB), BF16),
        scratch_shapes=[pltpu.VMEM((tq, D_SB), F32), pltpu.VMEM((stacked_rows, LANES), F32)],
        compiler_params=pltpu.CompilerParams(
            dimension_semantics=("parallel", "arbitrary"), vmem_limit_bytes=VMEM_LIMIT_BYTES),
        name="attn",
    )(q_arr, k_arr, v_arr)


def _merge_kernel(x_ref, u_ref, halo_ref, o_ref, gate_ref, pw_ref, ps_ref, wbp_ref, wbs_ref,
                  wout_ref, y_ref, *copy_refs, tm, pos0, tiles_per_seq):
    def weight(ref, position, *idx):
        w = ref[idx] if idx else ref[...]
        if copy_refs:
            w = w.astype(BF16)
            if idx:
                copy_refs[position][idx] = w
            else:
                copy_refs[position][...] = w
        return w

    w_sb = weight(wbs_ref, 2)
    streams = halo_ref.shape[0]
    rows = tm // streams
    if streams == 1:
        first_pos = pos0 + (pl.program_id(0) % tiles_per_seq) * tm
        pos = first_pos + lax.broadcasted_iota(jnp.int32, (tm, 1), 0)
    else:
        pos = jnp.concatenate([pos0 + lax.broadcasted_iota(jnp.int32, (rows, 1), 0)] * streams,
                              axis=0)

    n_groups = len(POOL_WINDOWS)
    n_chunks = n_groups if tm % (BF16_SUBLANES * n_groups) == 0 else 1
    quarter = tm // n_chunks
    span = POOL_HALO + rows
    run = jnp.concatenate(
        [part for b in range(streams)
         for part in (halo_ref[b], u_ref[b * rows:(b + 1) * rows, :])], axis=0)
    width, mixed, gated_sb = 1, [], []
    for gi, win in enumerate(POOL_WINDOWS):
        while width < win:
            run = run + pltpu.roll(run, width, 0)
            width *= 2
        assert width == win, "pool windows must be ascending powers of two"
        cols = slice(gi * POOL_GROUP, (gi + 1) * POOL_GROUP)
        cnt = jnp.minimum(win, pos + 1).astype(F32)
        tot = jnp.concatenate([run[b * span + POOL_HALO:(b + 1) * span, 0:POOL_GROUP]
                               for b in range(streams)], axis=0)
        d = (tot / cnt - u_ref[:, cols]).astype(BF16)
        mixed.append(jnp.dot(d, weight(pw_ref, 0, gi), preferred_element_type=F32)
                     * ps_ref[:, cols])
        if gi + 1 < n_groups:
            run = run[:, POOL_GROUP:]
        if gi < n_chunks:
            part = slice(gi * quarter, (gi + 1) * quarter)
            br_sb = jnp.dot(o_ref[part, :], w_sb, preferred_element_type=F32)
            gated_sb.append(
                _sigmoid(gate_ref[part, D_MODEL:2 * D_MODEL].astype(F32)) * br_sb)
    a = jnp.concatenate(mixed, axis=1).astype(BF16)

    br_pool = jnp.dot(a, weight(wbp_ref, 1), preferred_element_type=F32)
    merged = (_sigmoid(gate_ref[:, 0:D_MODEL].astype(F32)) * br_pool
              + jnp.concatenate(gated_sb, axis=0))
    y_ref[...] = x_ref[...] + jnp.dot(merged.astype(BF16), weight(wout_ref, 3),
                                      preferred_element_type=F32)


def _merge(x, u, halo, o, gate, pool_w, pool_scale, w_bp, w_bs, w_out, *, tm, pos0, seq):
    n = x.shape[0]
    const2 = lambda i: (0, 0)
    resident = pl.Buffered(1)
    streams = max(tm // seq, 1)
    assert tm % seq == 0 or seq % tm == 0
    weights = (pool_w, w_bp, w_bs, w_out)
    cast = pool_w.dtype == F32
    assert not cast or n == tm
    out_specs = [pl.BlockSpec((tm, D_MODEL), lambda i: (i, 0))]
    out_shape = [jax.ShapeDtypeStruct((n, D_MODEL), F32)]
    if cast:
        out_specs += [pl.BlockSpec(m.shape, lambda i, nd=m.ndim: (0,) * nd) for m in weights]
        out_shape += [jax.ShapeDtypeStruct(m.shape, BF16) for m in weights]
    return pl.pallas_call(
        functools.partial(_merge_kernel, tm=tm, pos0=pos0, tiles_per_seq=max(seq // tm, 1)),
        grid=(n // tm,),
        in_specs=[
            pl.BlockSpec((tm, D_MODEL), lambda i: (i, 0)),
            pl.BlockSpec((tm, D_POOL), lambda i: (i, 0)),
            pl.BlockSpec((streams, POOL_HALO, D_POOL), lambda i: (i, 0, 0)),
            pl.BlockSpec((tm, D_SB), lambda i: (i, 0)),
            pl.BlockSpec((tm, 2 * D_MODEL), lambda i: (i, 0)),
            pl.BlockSpec((len(POOL_WINDOWS), POOL_GROUP, POOL_GROUP), lambda i: (0, 0, 0)),
            pl.BlockSpec((1, D_POOL), const2),
            pl.BlockSpec((D_POOL, D_MODEL), const2, pipeline_mode=resident),
            pl.BlockSpec((D_SB, D_MODEL), const2, pipeline_mode=resident),
            pl.BlockSpec((D_MODEL, D_MODEL), const2, pipeline_mode=resident),
        ],
        out_specs=out_specs,
        out_shape=out_shape,
        compiler_params=pltpu.CompilerParams(
            dimension_semantics=("parallel",), vmem_limit_bytes=VMEM_LIMIT_BYTES),
        name="merge_cast" if cast else "merge",
    )(x, u, halo, o, gate, pool_w, pool_scale, w_bp, w_bs, w_out)


def _layer(x, w, *, batch, seq, pos0, k_past, v_past, pool_prefix, tm_ffn, tm_mix, tm_merge, tq,
           hq):
    assert seq >= POOL_STATE
    w_bf16 = {}

    def ffn(x_in, prefix, final_g=None):
        names = [prefix + "_gate", prefix + "_up", prefix + "_down"]
        res = _ffn(x_in, w[prefix + "_norm"], *[w[nm] for nm in names], final_g,
                   tm=tm_ffn, tf=FFN_CHUNK)
        if isinstance(res, (list, tuple)):
            w_bf16.update(zip(names, res[1:]))
            return res[0]
        return res

    x1 = ffn(x, "ffn1")
    u, qkv, k_hm, v_hm, gate, *w_in_copy = _proj(x1, w["mix_norm"], w["w_in"], batch=batch,
                                                 seq=seq, tm=tm_mix)
    w_bf16.update(zip(["w_in"], w_in_copy))

    if k_past is None:
        o = _attn(qkv, 0, qkv, 1, qkv, 2, batch=batch, q_len=seq, k_len=seq, pos0=0, tq=tq, hq=hq)
    else:
        def with_past(past, col):
            new = qkv[:, col * D_SB:(col + 1) * D_SB].reshape(batch, seq, D_SB)
            return jnp.concatenate([past, new], axis=1).reshape(batch * (pos0 + seq), D_SB)
        o = _attn(qkv, 0, with_past(k_past, 1), 0, with_past(v_past, 2), 0,
                  batch=batch, q_len=seq, k_len=pos0 + seq, pos0=pos0, tq=tq, hq=hq)

    piece = min(tm_merge, seq)
    pieces = seq // piece
    u4 = u.reshape(batch, pieces, piece, D_POOL)
    first = jnp.pad(pool_prefix, ((0, 0), (1, 0), (0, 0)))[:, None]
    halo = jnp.concatenate([first, u4[:, :-1, piece - POOL_HALO:, :]], axis=1)
    halo = halo.reshape(batch * pieces, POOL_HALO, D_POOL)

    merge_names = ["pool_w", "w_branch_pool", "w_branch_sb", "w_out"]
    x2, *copies = _merge(x1, u, halo, o, gate, w["pool_w"], w["pool_scale"], w["w_branch_pool"],
                         w["w_branch_sb"], w["w_out"], tm=tm_merge, pos0=pos0, seq=seq)
    w_bf16.update(zip(merge_names, copies))
    y = ffn(x2, "ffn2", w["final_norm"])
    new_pool = u.reshape(batch, seq, D_POOL)[:, seq - POOL_STATE:, :]
    return (y.reshape(batch, seq, D_MODEL), k_hm, v_hm, new_pool), w_bf16


def kernel(x_prompt, x_sample, cache_k, cache_v, state_pool, ffn1_norm, ffn1_gate, ffn1_up, ffn1_down, mix_norm, w_in, pool_w, pool_scale, w_branch_pool, w_branch_sb, w_out, ffn2_norm, ffn2_gate, ffn2_up, ffn2_down, final_norm):
    assert ffn1_norm.shape[0] == 1, "single-layer kernel"
    w = dict(w_in=w_in[0], pool_w=pool_w[0], w_branch_pool=w_branch_pool[0],
             w_branch_sb=w_branch_sb[0], w_out=w_out[0],
             ffn1_gate=ffn1_gate[0], ffn1_up=ffn1_up[0], ffn1_down=ffn1_down[0],
             ffn2_gate=ffn2_gate[0], ffn2_up=ffn2_up[0], ffn2_down=ffn2_down[0])
    w.update(ffn1_norm=ffn1_norm, mix_norm=mix_norm, pool_scale=pool_scale, ffn2_norm=ffn2_norm,
             final_norm=final_norm.reshape(1, D_MODEL))

    b_p, s_p, _ = x_prompt.shape
    b_d, s_d, _ = x_sample.shape
    past = cache_k.shape[3]

    def token_major(cache):
        return cache[0].transpose(0, 2, 1, 3).reshape(b_d, past, D_SB).astype(BF16)

    (y_d, k_d, v_d, pool_d), w_bf16 = _layer(
        x_sample.reshape(b_d * s_d, D_MODEL), w, batch=b_d, seq=s_d, pos0=past,
        k_past=token_major(cache_k), v_past=token_major(cache_v), pool_prefix=state_pool[0],
        tm_ffn=b_d * s_d, tm_mix=b_d * s_d, tm_merge=b_d * s_d, tq=s_d, hq=s_d)

    (y_p, k_p, v_p, pool_p), _ = _layer(
        x_prompt.reshape(b_p * s_p, D_MODEL), {**w, **w_bf16}, batch=b_p, seq=s_p, pos0=0,
        k_past=None, v_past=None, pool_prefix=jnp.zeros((b_p, POOL_STATE, D_POOL), F32),
        tm_ffn=1024, tm_mix=1024, tm_merge=1024, tq=512, hq=64)

    return (y_p, y_d, k_p[None], v_p[None], pool_p[None], k_d[None], v_d[None], pool_d[None])
```

```python
import functools

import jax
import jax.numpy as jnp
from jax import lax
from jax.experimental import pallas as pl
from jax.experimental.pallas import tpu as pltpu

F32 = jnp.float32
BF16 = jnp.bfloat16

D_MODEL = 1024
D_FF = 4 * D_MODEL
D_POOL = D_MODEL // 2
POOL_WINDOWS = (2, 4, 8, 16)
POOL_GROUP = D_POOL // len(POOL_WINDOWS)
POOL_STATE = max(POOL_WINDOWS) - 1
POOL_HALO = POOL_STATE + 1
SB_HEADS = 8
SB_HEAD_DIM = 64
D_SB = SB_HEADS * SB_HEAD_DIM
D_IN = D_POOL + 3 * D_SB + 2 * D_MODEL
RMS_EPS = 1e-6
LANES = 128
BF16_SUBLANES = 16
VMEM_LIMIT_BYTES = 56 * 1024 * 1024
FFN_CHUNK = 1024
ATTN_WINDOW = 2 * LANES

LOG2E = 1.4426950408889634
SURVIVAL_UNDERFLOW_LOG2 = 150.0 * (1.0 + 2.0 ** -8)
MASKED_LOGIT = -1e30


def _rmsnorm(x, g):
    ms = jnp.mean(x * x, axis=-1, keepdims=True)
    return (x * lax.rsqrt(ms + RMS_EPS)) * g


def _sigmoid(x):
    return 0.5 * jnp.tanh(0.5 * x) + 0.5


def _ffn_kernel(*refs, n_ff, final, cast):
    refs = list(refs)
    x_ref, xnext_ref, g_ref, wg_ref, wu_ref, wd_ref = refs[:6]
    del refs[:6]
    fg_ref = refs.pop(0) if final else None
    o_ref = refs.pop(0)
    if cast:
        wg_out, wu_out, wd_out = refs[:3]
        del refs[:3]
    xn_ref, acc_ref = refs
    i = pl.program_id(0)
    j = pl.program_id(1)
    slot = i % 2

    @pl.when(jnp.logical_and(i == 0, j == 0))
    def _():
        xn_ref[0] = _rmsnorm(x_ref[...], g_ref[...]).astype(BF16)

    def chunk():
        wg, wu, wd = wg_ref[...], wu_ref[...], wd_ref[...]
        if cast:
            wg, wu, wd = wg.astype(BF16), wu.astype(BF16), wd.astype(BF16)
            wg_out[...] = wg
            wu_out[...] = wu
            wd_out[...] = wd
        xn = xn_ref[slot]
        gate = jnp.dot(xn, wg, preferred_element_type=F32)
        up = jnp.dot(xn, wu, preferred_element_type=F32)
        h = (gate * _sigmoid(gate) * up).astype(BF16)
        return jnp.dot(h, wd, preferred_element_type=F32)

    def prepare_next(rows):
        xn_ref[1 - slot, rows, :] = _rmsnorm(xnext_ref[rows, :], g_ref[...]).astype(BF16)

    piece = x_ref.shape[0] // n_ff

    def first():
        acc_ref[...] = chunk()
        prepare_next(pl.ds(0, piece))

    def middle():
        acc_ref[...] += chunk()
        prepare_next(pl.ds(pl.multiple_of(j * piece, piece), piece))

    def last():
        y = x_ref[...] + 0.5 * (acc_ref[...] + chunk())
        if final:
            y = _rmsnorm(y, fg_ref[...])
        o_ref[...] = y
        prepare_next(pl.ds((n_ff - 1) * piece, piece))

    case = jnp.where(j == 0, 0, jnp.where(j == n_ff - 1, 2, 1))
    lax.switch(case, [first, middle, last])


def _ffn(x, norm_g, wg, wu, wd, final_g=None, *, tm, tf):
    n = x.shape[0]
    n_rows = n // tm
    n_ff = D_FF // tf
    assert n_ff >= 2 and tm % (BF16_SUBLANES * n_ff) == 0
    final = final_g is not None
    cast = wg.dtype == F32
    assert not cast or n_rows == 1
    row = pl.BlockSpec((1, D_MODEL), lambda i, j: (0, 0))
    w_specs = [
        pl.BlockSpec((D_MODEL, tf), lambda i, j: (0, j)),
        pl.BlockSpec((D_MODEL, tf), lambda i, j: (0, j)),
        pl.BlockSpec((tf, D_MODEL), lambda i, j: (j, 0)),
    ]
    in_specs = [
        pl.BlockSpec((tm, D_MODEL), lambda i, j: (i, 0)),
        pl.BlockSpec((tm, D_MODEL), lambda i, j: (jnp.minimum(i + 1, n_rows - 1), 0)),
        row,
    ] + w_specs
    args = [x, x, norm_g, wg, wu, wd]
    if final:
        in_specs.append(row)
        args.append(final_g)
    out_specs = [pl.BlockSpec((tm, D_MODEL), lambda i, j: (i, 0))]
    out_shape = [jax.ShapeDtypeStruct((n, D_MODEL), F32)]
    if cast:
        out_specs += w_specs
        out_shape += [jax.ShapeDtypeStruct(m.shape, BF16) for m in (wg, wu, wd)]
    outs = pl.pallas_call(
        functools.partial(_ffn_kernel, n_ff=n_ff, final=final, cast=cast),
        grid=(n_rows, n_ff),
        in_specs=in_specs,
        out_specs=out_specs,
        out_shape=out_shape,
        scratch_shapes=[pltpu.VMEM((2, tm, D_MODEL), BF16), pltpu.VMEM((tm, D_MODEL), F32)],
        compiler_params=pltpu.CompilerParams(
            dimension_semantics=("arbitrary", "arbitrary"), vmem_limit_bytes=VMEM_LIMIT_BYTES),
        name=("ffn_final" if final else "ffn") + ("_cast" if cast else ""),
    )(*args)
    return outs if cast else outs[0]


def _proj_kernel(x_ref, g_ref, w_ref, u_ref, qkv_ref, k_ref, v_ref, gate_ref, w_out_ref=None):
    h = _rmsnorm(x_ref[...], g_ref[...]).astype(BF16)

    def mm(c0, c1):
        w = w_ref[:, c0:c1]
        if w_out_ref is not None:
            w = w.astype(BF16)
            w_out_ref[:, c0:c1] = w
        return jnp.dot(h, w, preferred_element_type=F32)

    c = 0
    u_ref[...] = mm(c, c + D_POOL)
    c += D_POOL
    qkv_ref[:, 0:D_SB] = (mm(c, c + D_SB) * (LOG2E * SB_HEAD_DIM ** -0.5)).astype(BF16)
    c += D_SB
    for idx, hm_ref in ((1, k_ref), (2, v_ref)):
        t = mm(c, c + D_SB)
        c += D_SB
        qkv_ref[:, idx * D_SB:(idx + 1) * D_SB] = t.astype(BF16)
        streams, _, rows, _ = hm_ref.shape
        for b in range(streams):
            for hd in range(SB_HEADS):
                hm_ref[b, hd, :, :] = t[b * rows:(b + 1) * rows,
                                        hd * SB_HEAD_DIM:(hd + 1) * SB_HEAD_DIM]
    gate_ref[:, 0:D_MODEL] = mm(c, c + D_MODEL).astype(BF16)
    c += D_MODEL
    gate_ref[:, D_MODEL:2 * D_MODEL] = mm(c, c + D_MODEL).astype(BF16)


def _proj(x, norm_g, w_in, *, batch, seq, tm):
    n = x.shape[0]
    cast = w_in.dtype == F32
    assert not cast or n == tm
    if tm <= seq:
        tiles_per_seq = seq // tm
        hm_spec = pl.BlockSpec((1, SB_HEADS, tm, SB_HEAD_DIM),
                               lambda i: (i // tiles_per_seq, 0, i % tiles_per_seq, 0))
    else:
        assert tm % seq == 0
        hm_spec = pl.BlockSpec((tm // seq, SB_HEADS, seq, SB_HEAD_DIM), lambda i: (i, 0, 0, 0))
    hm_shape = jax.ShapeDtypeStruct((batch, SB_HEADS, seq, SB_HEAD_DIM), F32)
    w_spec = pl.BlockSpec((D_MODEL, D_IN), lambda i: (0, 0), pipeline_mode=pl.Buffered(1))
    out_specs = [
        pl.BlockSpec((tm, D_POOL), lambda i: (i, 0)),
        pl.BlockSpec((tm, 3 * D_SB), lambda i: (i, 0)),
        hm_spec,
        hm_spec,
        pl.BlockSpec((tm, 2 * D_MODEL), lambda i: (i, 0)),
    ]
    out_shape = [
        jax.ShapeDtypeStruct((n, D_POOL), F32),
        jax.ShapeDtypeStruct((n, 3 * D_SB), BF16),
        hm_shape,
        hm_shape,
        jax.ShapeDtypeStruct((n, 2 * D_MODEL), BF16),
    ]
    if cast:
        out_specs.append(pl.BlockSpec((D_MODEL, D_IN), lambda i: (0, 0)))
        out_shape.append(jax.ShapeDtypeStruct((D_MODEL, D_IN), BF16))
    return pl.pallas_call(
        _proj_kernel,
        grid=(n // tm,),
        in_specs=[
            pl.BlockSpec((tm, D_MODEL), lambda i: (i, 0)),
            pl.BlockSpec((1, D_MODEL), lambda i: (0, 0)),
            w_spec,
        ],
        out_specs=out_specs,
        out_shape=out_shape,
        compiler_params=pltpu.CompilerParams(
            dimension_semantics=("parallel",), vmem_limit_bytes=VMEM_LIMIT_BYTES),
        name="proj_cast" if cast else "proj",
    )(x, norm_g, w_in)


def _attn_kernel(q_ref, k_ref, v_ref, o_ref, acc_ref, car_ref, *, tq, hq, pos0):
    win = ATTN_WINDOW
    q_start = pos0 + pl.program_id(1) * tq
    subs = range(tq // hq)
    pairs = range(SB_HEADS // 2)
    pair_cols = [slice(p * LANES, (p + 1) * LANES) for p in pairs]

    lane = lax.broadcasted_iota(jnp.int32, (1, LANES), 1)
    first_head = lane < SB_HEAD_DIM
    r = lax.broadcasted_iota(jnp.int32, (win, win), 0)
    c = lax.broadcasted_iota(jnp.int32, (win, win), 1)
    newer = jnp.where(r > c, 1.0, 0.0).astype(BF16)

    def softplus2(z):
        return jnp.maximum(z, 0.0) + jnp.log2(1.0 + jnp.exp2(-jnp.abs(z)))

    def visit(m, mode):
        k_ws, v_ws, valids = [], [], []
        for j in subs:
            row0 = q_start + j * hq
            win_end = row0 + hq - m * win
            q_pos = row0 + lax.broadcasted_iota(jnp.int32, (hq, 1), 0)
            if mode == "inner":
                start = pl.multiple_of(win_end - win, hq)
                k_pos = (win_end - LANES) + lax.broadcasted_iota(jnp.int32, (1, LANES), 1)
                valids.append(k_pos < q_pos)
            else:
                start = pl.multiple_of(jnp.maximum(win_end - win, 0), hq)
                k_pos = start + lax.broadcasted_iota(jnp.int32, (1, win), 1)
                valids.append(k_pos < (q_pos if mode == "edge" else win_end))
            k_ws.append(k_ref[pl.ds(start, win), :])
            v_ws.append(v_ref[pl.ds(start, win), :])

        def mask(x, j):
            if mode == "inner":
                return jnp.concatenate(
                    [x[:, :win - LANES],
                     jnp.where(valids[j], x[:, win - LANES:], MASKED_LOGIT)], axis=1)
            return jnp.where(valids[j], x, MASKED_LOGIT)

        z_parts = []
        for j in subs:
            rows = slice(j * hq, (j + 1) * hq)
            for p in pairs:
                q_p = q_ref[rows, pair_cols[p]]
                zero = jnp.zeros_like(q_p)
                q_2 = jnp.concatenate([jnp.where(first_head, q_p, zero),
                                       jnp.where(first_head, zero, q_p)], axis=0)
                z_2 = lax.dot_general(q_2, k_ws[j][:, pair_cols[p]], (((1,), (1,)), ((), ())),
                                      preferred_element_type=F32)
                z_parts += [mask(z_2[:hq], j), mask(z_2[hq:], j)]
        z = jnp.concatenate(z_parts, axis=0)
        s = softplus2(z)
        sums = jnp.dot(s.astype(BF16), newer, preferred_element_type=F32)
        arg = (z - s) - sums
        total = sums[:, 0:LANES] + s[:, 0:LANES]
        if mode == "older":
            car = car_ref[...]
            car_row = jnp.broadcast_to(car[:, 0:1], car.shape)
            arg = arg - jnp.concatenate([car_row] * (win // LANES), axis=1)
            total = car + total
        car_ref[...] = total
        w = jnp.exp2(arg).astype(BF16)

        i = 0
        for j in subs:
            rows = slice(j * hq, (j + 1) * hq)
            for p in pairs:
                o_2 = jnp.dot(w[i * hq:(i + 2) * hq], v_ws[j][:, pair_cols[p]],
                              preferred_element_type=F32)
                i += 2
                o_p = jnp.where(first_head, o_2[:hq], o_2[hq:])
                if mode == "older":
                    acc_ref[rows, pair_cols[p]] += o_p
                else:
                    acc_ref[rows, pair_cols[p]] = o_p
        row_total = jnp.where(lane == 0, total, SURVIVAL_UNDERFLOW_LOG2)
        return (jnp.min(row_total) >= SURVIVAL_UNDERFLOW_LOG2).astype(jnp.int32)

    done = lax.cond(q_start + hq >= win, lambda: visit(0, "inner"), lambda: visit(0, "edge"))

    def cond(carry):
        m, done = carry
        return jnp.logical_and(q_start + tq - m * win > 0, done == 0)

    def body(carry):
        m, _ = carry
        return m + 1, visit(m, "older")

    lax.while_loop(cond, body, (jnp.int32(1), done))
    o_ref[...] = acc_ref[...].astype(BF16)


def _attn(q_arr, q_col, k_arr, k_col, v_arr, v_col, *, batch, q_len, k_len, pos0, tq, hq):
    assert tq % hq == 0 and q_len % tq == 0 and pos0 % tq == 0 and ATTN_WINDOW % hq == 0
    assert pos0 + q_len <= k_len and k_len >= ATTN_WINDOW and hq % BF16_SUBLANES == 0
    nq = q_len // tq
    stacked_rows = (tq // hq) * SB_HEADS * hq
    return pl.pallas_call(
        functools.partial(_attn_kernel, tq=tq, hq=hq, pos0=pos0),
        grid=(batch, nq),
        in_specs=[
            pl.BlockSpec((tq, D_SB), lambda b, i: (b * nq + i, q_col)),
            pl.BlockSpec((k_len, D_SB), lambda b, i: (b, k_col)),
            pl.BlockSpec((k_len, D_SB), lambda b, i: (b, v_col)),
        ],
        out_specs=pl.BlockSpec((tq, D_SB), lambda b, i: (b * nq + i, 0)),
        out_shape=jax.ShapeDtypeStruct((batch * q_len, D_SB), BF16),
        scratch_shapes=[pltpu.VMEM((tq, D_SB), F32), pltpu.VMEM((stacked_rows, LANES), F32)],
        compiler_params=pltpu.CompilerParams(
            dimension_semantics=("parallel", "arbitrary"), vmem_limit_bytes=VMEM_LIMIT_BYTES),
        name="attn",
    )(q_arr, k_arr, v_arr)


def _merge_kernel(x_ref, u_ref, halo_ref, o_ref, gate_ref, pw_ref, ps_ref, wbp_ref, wbs_ref,
                  wout_ref, y_ref, *copy_refs, tm, pos0, tiles_per_seq):
    def weight(ref, position, *idx):
        w = ref[idx] if idx else ref[...]
        if copy_refs:
            w = w.astype(BF16)
            if idx:
                copy_refs[position][idx] = w
            else:
                copy_refs[position][...] = w
        return w

    w_sb = weight(wbs_ref, 2)
    streams = halo_ref.shape[0]
    rows = tm // streams
    if streams == 1:
        first_pos = pos0 + (pl.program_id(0) % tiles_per_seq) * tm
        pos = first_pos + lax.broadcasted_iota(jnp.int32, (tm, 1), 0)
    else:
        pos = jnp.concatenate([pos0 + lax.broadcasted_iota(jnp.int32, (rows, 1), 0)] * streams,
                              axis=0)

    n_groups = len(POOL_WINDOWS)
    n_chunks = n_groups if tm % (BF16_SUBLANES * n_groups) == 0 else 1
    quarter = tm // n_chunks
    span = POOL_HALO + rows
    run = jnp.concatenate(
        [part for b in range(streams)
         for part in (halo_ref[b], u_ref[b * rows:(b + 1) * rows, :])], axis=0)
    width, mixed, gated_sb = 1, [], []
    for gi, win in enumerate(POOL_WINDOWS):
        while width < win:
            run = run + pltpu.roll(run, width, 0)
            width *= 2
        assert width == win, "pool windows must be ascending powers of two"
        cols = slice(gi * POOL_GROUP, (gi + 1) * POOL_GROUP)
        cnt = jnp.minimum(win, pos + 1).astype(F32)
        tot = jnp.concatenate([run[b * span + POOL_HALO:(b + 1) * span, 0:POOL_GROUP]
                               for b in range(streams)], axis=0)
        d = (tot / cnt - u_ref[:, cols]).astype(BF16)
        mixed.append(jnp.dot(d, weight(pw_ref, 0, gi), preferred_element_type=F32)
                     * ps_ref[:, cols])
        if gi + 1 < n_groups:
            run = run[:, POOL_GROUP:]
        if gi < n_chunks:
            part = slice(gi * quarter, (gi + 1) * quarter)
            br_sb = jnp.dot(o_ref[part, :], w_sb, preferred_element_type=F32)
            gated_sb.append(
                _sigmoid(gate_ref[part, D_MODEL:2 * D_MODEL].astype(F32)) * br_sb)
    a = jnp.concatenate(mixed, axis=1).astype(BF16)

    br_pool = jnp.dot(a, weight(wbp_ref, 1), preferred_element_type=F32)
    merged = (_sigmoid(gate_ref[:, 0:D_MODEL].astype(F32)) * br_pool
              + jnp.concatenate(gated_sb, axis=0))
    y_ref[...] = x_ref[...] + jnp.dot(merged.astype(BF16), weight(wout_ref, 3),
                                      preferred_element_type=F32)


def _merge(x, u, halo, o, gate, pool_w, pool_scale, w_bp, w_bs, w_out, *, tm, pos0, seq):
    n = x.shape[0]
    const2 = lambda i: (0, 0)
    resident = pl.Buffered(1)
    streams = max(tm // seq, 1)
    assert tm % seq == 0 or seq % tm == 0
    weights = (pool_w, w_bp, w_bs, w_out)
    cast = pool_w.dtype == F32
    assert not cast or n == tm
    out_specs = [pl.BlockSpec((tm, D_MODEL), lambda i: (i, 0))]
    out_shape = [jax.ShapeDtypeStruct((n, D_MODEL), F32)]
    if cast:
        out_specs += [pl.BlockSpec(m.shape, lambda i, nd=m.ndim: (0,) * nd) for m in weights]
        out_shape += [jax.ShapeDtypeStruct(m.shape, BF16) for m in weights]
    return pl.pallas_call(
        functools.partial(_merge_kernel, tm=tm, pos0=pos0, tiles_per_seq=max(seq // tm, 1)),
        grid=(n // tm,),
        in_specs=[
            pl.BlockSpec((tm, D_MODEL), lambda i: (i, 0)),
            pl.BlockSpec((tm, D_POOL), lambda i: (i, 0)),
            pl.BlockSpec((streams, POOL_HALO, D_POOL), lambda i: (i, 0, 0)),
            pl.BlockSpec((tm, D_SB), lambda i: (i, 0)),
            pl.BlockSpec((tm, 2 * D_MODEL), lambda i: (i, 0)),
            pl.BlockSpec((len(POOL_WINDOWS), POOL_GROUP, POOL_GROUP), lambda i: (0, 0, 0)),
            pl.BlockSpec((1, D_POOL), const2),
            pl.BlockSpec((D_POOL, D_MODEL), const2, pipeline_mode=resident),
            pl.BlockSpec((D_SB, D_MODEL), const2, pipeline_mode=resident),
            pl.BlockSpec((D_MODEL, D_MODEL), const2, pipeline_mode=resident),
        ],
        out_specs=out_specs,
        out_shape=out_shape,
        compiler_params=pltpu.CompilerParams(
            dimension_semantics=("parallel",), vmem_limit_bytes=VMEM_LIMIT_BYTES),
        name="merge_cast" if cast else "merge",
    )(x, u, halo, o, gate, pool_w, pool_scale, w_bp, w_bs, w_out)


def _layer(x, w, *, batch, seq, pos0, k_past, v_past, pool_prefix, tm_ffn, tm_mix, tm_merge, tq,
           hq):
    assert seq >= POOL_STATE
    w_bf16 = {}

    def ffn(x_in, prefix, final_g=None):
        names = [prefix + "_gate", prefix + "_up", prefix + "_down"]
        res = _ffn(x_in, w[prefix + "_norm"], *[w[nm] for nm in names], final_g,
                   tm=tm_ffn, tf=FFN_CHUNK)
        if isinstance(res, (list, tuple)):
            w_bf16.update(zip(names, res[1:]))
            return res[0]
        return res

    x1 = ffn(x, "ffn1")
    u, qkv, k_hm, v_hm, gate, *w_in_copy = _proj(x1, w["mix_norm"], w["w_in"], batch=batch,
                                                 seq=seq, tm=tm_mix)
    w_bf16.update(zip(["w_in"], w_in_copy))

    if k_past is None:
        o = _attn(qkv, 0, qkv, 1, qkv, 2, batch=batch, q_len=seq, k_len=seq, pos0=0, tq=tq, hq=hq)
    else:
        def with_past(past, col):
            new = qkv[:, col * D_SB:(col + 1) * D_SB].reshape(batch, seq, D_SB)
            return jnp.concatenate([past, new], axis=1).reshape(batch * (pos0 + seq), D_SB)
        o = _attn(qkv, 0, with_past(k_past, 1), 0, with_past(v_past, 2), 0,
                  batch=batch, q_len=seq, k_len=pos0 + seq, pos0=pos0, tq=tq, hq=hq)

    piece = min(tm_merge, seq)
    pieces = seq // piece
    u4 = u.reshape(batch, pieces, piece, D_POOL)
    first = jnp.pad(pool_prefix, ((0, 0), (1, 0), (0, 0)))[:, None]
    halo = jnp.concatenate([first, u4[:, :-1, piece - POOL_HALO:, :]], axis=1)
    halo = halo.reshape(batch * pieces, POOL_HALO, D_POOL)

    merge_names = ["pool_w", "w_branch_pool", "w_branch_sb", "w_out"]
    x2, *copies = _merge(x1, u, halo, o, gate, w["pool_w"], w["pool_scale"], w["w_branch_pool"],
                         w["w_branch_sb"], w["w_out"], tm=tm_merge, pos0=pos0, seq=seq)
    w_bf16.update(zip(merge_names, copies))
    y = ffn(x2, "ffn2", w["final_norm"])
    new_pool = u.reshape(batch, seq, D_POOL)[:, seq - POOL_STATE:, :]
    return (y.reshape(batch, seq, D_MODEL), k_hm, v_hm, new_pool), w_bf16


def kernel(x_prompt, x_sample, cache_k, cache_v, state_pool, ffn1_norm, ffn1_gate, ffn1_up, ffn1_down, mix_norm, w_in, pool_w, pool_scale, w_branch_pool, w_branch_sb, w_out, ffn2_norm, ffn2_gate, ffn2_up, ffn2_down, final_norm):
    assert ffn1_norm.shape[0] == 1, "single-layer kernel"
    w = dict(w_in=w_in[0], pool_w=pool_w[0], w_branch_pool=w_branch_pool[0],
             w_branch_sb=w_branch_sb[0], w_out=w_out[0],
             ffn1_gate=ffn1_gate[0], ffn1_up=ffn1_up[0], ffn1_down=ffn1_down[0],
             ffn2_gate=ffn2_gate[0], ffn2_up=ffn2_up[0], ffn2_down=ffn2_down[0])
    w.update(ffn1_norm=ffn1_norm, mix_norm=mix_norm, pool_scale=pool_scale, ffn2_norm=ffn2_norm,
             final_norm=final_norm.reshape(1, D_MODEL))

    b_p, s_p, _ = x_prompt.shape
    b_d, s_d, _ = x_sample.shape
    past = cache_k.shape[3]

    def token_major(cache):
        return cache[0].transpose(0, 2, 1, 3).reshape(b_d, past, D_SB).astype(BF16)

    (y_d, k_d, v_d, pool_d), w_bf16 = _layer(
        x_sample.reshape(b_d * s_d, D_MODEL), w, batch=b_d, seq=s_d, pos0=past,
        k_past=token_major(cache_k), v_past=token_major(cache_v), pool_prefix=state_pool[0],
        tm_ffn=b_d * s_d, tm_mix=b_d * s_d, tm_merge=b_d * s_d, tq=s_d, hq=s_d)

    (y_p, k_p, v_p, pool_p), _ = _layer(
        x_prompt.reshape(b_p * s_p, D_MODEL), {**w, **w_bf16}, batch=b_p, seq=s_p, pos0=0,
        k_past=None, v_past=None, pool_prefix=jnp.zeros((b_p, POOL_STATE, D_POOL), F32),
        tm_ffn=1024, tm_mix=1024, tm_merge=1024, tq=512, hq=64)

    return (y_p, y_d, k_p[None], v_p[None], pool_p[None], k_d[None], v_d[None], pool_d[None])
```

```python
import functools

import jax
import jax.numpy as jnp
from jax import lax
from jax.experimental import pallas as pl
from jax.experimental.pallas import tpu as pltpu

F32 = jnp.float32
BF16 = jnp.bfloat16

D_MODEL = 1024
D_FF = 4 * D_MODEL
D_POOL = D_MODEL // 2
POOL_WINDOWS = (2, 4, 8, 16)
POOL_GROUP = D_POOL // len(POOL_WINDOWS)
POOL_STATE = max(POOL_WINDOWS) - 1
POOL_HALO = POOL_STATE + 1
SB_HEADS = 8
SB_HEAD_DIM = 64
D_SB = SB_HEADS * SB_HEAD_DIM
D_IN = D_POOL + 3 * D_SB + 2 * D_MODEL
RMS_EPS = 1e-6
LANES = 128
BF16_SUBLANES = 16
VMEM_LIMIT_BYTES = 56 * 1024 * 1024
FFN_CHUNK = 1024
ATTN_WINDOW = 2 * LANES

LOG2E = 1.4426950408889634
SURVIVAL_UNDERFLOW_LOG2 = 150.0 * (1.0 + 2.0 ** -8)
MASKED_LOGIT = -1e30


def _rmsnorm(x, g):
    ms = jnp.mean(x * x, axis=-1, keepdims=True)
    return (x * lax.rsqrt(ms + RMS_EPS)) * g


def _sigmoid(x):
    return 0.5 * jnp.tanh(0.5 * x) + 0.5


def _ffn_kernel(*refs, n_ff, final, cast):
    refs = list(refs)
    x_ref, xnext_ref, g_ref, wg_ref, wu_ref, wd_ref = refs[:6]
    del refs[:6]
    fg_ref = refs.pop(0) if final else None
    o_ref = refs.pop(0)
    if cast:
        wg_out, wu_out, wd_out = refs[:3]
        del refs[:3]
    xn_ref, acc_ref = refs
    i = pl.program_id(0)
    j = pl.program_id(1)
    slot = i % 2

    @pl.when(jnp.logical_and(i == 0, j == 0))
    def _():
        xn_ref[0] = _rmsnorm(x_ref[...], g_ref[...]).astype(BF16)

    def chunk():
        wg, wu, wd = wg_ref[...], wu_ref[...], wd_ref[...]
        if cast:
            wg, wu, wd = wg.astype(BF16), wu.astype(BF16), wd.astype(BF16)
            wg_out[...] = wg
            wu_out[...] = wu
            wd_out[...] = wd
        xn = xn_ref[slot]
        gate = jnp.dot(xn, wg, preferred_element_type=F32)
        up = jnp.dot(xn, wu, preferred_element_type=F32)
        h = (gate * _sigmoid(gate) * up).astype(BF16)
        return jnp.dot(h, wd, preferred_element_type=F32)

    def prepare_next(rows):
        xn_ref[1 - slot, rows, :] = _rmsnorm(xnext_ref[rows, :], g_ref[...]).astype(BF16)

    piece = x_ref.shape[0] // n_ff

    def first():
        acc_ref[...] = chunk()
        prepare_next(pl.ds(0, piece))

    def middle():
        acc_ref[...] += chunk()
        prepare_next(pl.ds(pl.multiple_of(j * piece, piece), piece))

    def last():
        y = x_ref[...] + 0.5 * (acc_ref[...] + chunk())
        if final:
            y = _rmsnorm(y, fg_ref[...])
        o_ref[...] = y
        prepare_next(pl.ds((n_ff - 1) * piece, piece))

    case = jnp.where(j == 0, 0, jnp.where(j == n_ff - 1, 2, 1))
    lax.switch(case, [first, middle, last])


def _ffn(x, norm_g, wg, wu, wd, final_g=None, *, tm, tf):
    n = x.shape[0]
    n_rows = n // tm
    n_ff = D_FF // tf
    assert n_ff >= 2 and tm % (BF16_SUBLANES * n_ff) == 0
    final = final_g is not None
    cast = wg.dtype == F32
    assert not cast or n_rows == 1
    row = pl.BlockSpec((1, D_MODEL), lambda i, j: (0, 0))
    w_specs = [
        pl.BlockSpec((D_MODEL, tf), lambda i, j: (0, j)),
        pl.BlockSpec((D_MODEL, tf), lambda i, j: (0, j)),
        pl.BlockSpec((tf, D_MODEL), lambda i, j: (j, 0)),
    ]
    in_specs = [
        pl.BlockSpec((tm, D_MODEL), lambda i, j: (i, 0)),
        pl.BlockSpec((tm, D_MODEL), lambda i, j: (jnp.minimum(i + 1, n_rows - 1), 0)),
        row,
    ] + w_specs
    args = [x, x, norm_g, wg, wu, wd]
    if final:
        in_specs.append(row)
        args.append(final_g)
    out_specs = [pl.BlockSpec((tm, D_MODEL), lambda i, j: (i, 0))]
    out_shape = [jax.ShapeDtypeStruct((n, D_MODEL), F32)]
    if cast:
        out_specs += w_specs
        out_shape += [jax.ShapeDtypeStruct(m.shape, BF16) for m in (wg, wu, wd)]
    outs = pl.pallas_call(
        functools.partial(_ffn_kernel, n_ff=n_ff, final=final, cast=cast),
        grid=(n_rows, n_ff),
        in_specs=in_specs,
        out_specs=out_specs,
        out_shape=out_shape,
        scratch_shapes=[pltpu.VMEM((2, tm, D_MODEL), BF16), pltpu.VMEM((tm, D_MODEL), F32)],
        compiler_params=pltpu.CompilerParams(
            dimension_semantics=("arbitrary", "arbitrary"), vmem_limit_bytes=VMEM_LIMIT_BYTES),
        name=("ffn_final" if final else "ffn") + ("_cast" if cast else ""),
    )(*args)
    return outs if cast else outs[0]


def _proj_kernel(x_ref, g_ref, w_ref, u_ref, qkv_ref, k_ref, v_ref, gate_ref, w_out_ref=None):
    h = _rmsnorm(x_ref[...], g_ref[...]).astype(BF16)

    def mm(c0, c1):
        w = w_ref[:, c0:c1]
        if w_out_ref is not None:
            w = w.astype(BF16)
            w_out_ref[:, c0:c1] = w
        return jnp.dot(h, w, preferred_element_type=F32)

    c = 0
    u_ref[...] = mm(c, c + D_POOL)
    c += D_POOL
    qkv_ref[:, 0:D_SB] = (mm(c, c + D_SB) * (LOG2E * SB_HEAD_DIM ** -0.5)).astype(BF16)
    c += D_SB
    for idx, hm_ref in ((1, k_ref), (2, v_ref)):
        t = mm(c, c + D_SB)
        c += D_SB
        qkv_ref[:, idx * D_SB:(idx + 1) * D_SB] = t.astype(BF16)
        streams, _, rows, _ = hm_ref.shape
        for b in range(streams):
            for hd in range(SB_HEADS):
                hm_ref[b, hd, :, :] = t[b * rows:(b + 1) * rows,
                                        hd * SB_HEAD_DIM:(hd + 1) * SB_HEAD_DIM]
    gate_ref[:, 0:D_MODEL] = mm(c, c + D_MODEL).astype(BF16)
    c += D_MODEL
    gate_ref[:, D_MODEL:2 * D_MODEL] = mm(c, c + D_MODEL).astype(BF16)


def _proj(x, norm_g, w_in, *, batch, seq, tm):
    n = x.shape[0]
    cast = w_in.dtype == F32
    assert not cast or n == tm
    if tm <= seq:
        tiles_per_seq = seq // tm
        hm_spec = pl.BlockSpec((1, SB_HEADS, tm, SB_HEAD_DIM),
                               lambda i: (i // tiles_per_seq, 0, i % tiles_per_seq, 0))
    else:
        assert tm % seq == 0
        hm_spec = pl.BlockSpec((tm // seq, SB_HEADS, seq, SB_HEAD_DIM), lambda i: (i, 0, 0, 0))
    hm_shape = jax.ShapeDtypeStruct((batch, SB_HEADS, seq, SB_HEAD_DIM), F32)
    w_spec = pl.BlockSpec((D_MODEL, D_IN), lambda i: (0, 0), pipeline_mode=pl.Buffered(1))
    out_specs = [
        pl.BlockSpec((tm, D_POOL), lambda i: (i, 0)),
        pl.BlockSpec((tm, 3 * D_SB), lambda i: (i, 0)),
        hm_spec,
        hm_spec,
        pl.BlockSpec((tm, 2 * D_MODEL), lambda i: (i, 0)),
    ]
    out_shape = [
        jax.ShapeDtypeStruct((n, D_POOL), F32),
        jax.ShapeDtypeStruct((n, 3 * D_SB), BF16),
        hm_shape,
        hm_shape,
        jax.ShapeDtypeStruct((n, 2 * D_MODEL), BF16),
    ]
    if cast:
        out_specs.append(pl.BlockSpec((D_MODEL, D_IN), lambda i: (0, 0)))
        out_shape.append(jax.ShapeDtypeStruct((D_MODEL, D_IN), BF16))
    return pl.pallas_call(
        _proj_kernel,
        grid=(n // tm,),
        in_specs=[
            pl.BlockSpec((tm, D_MODEL), lambda i: (i, 0)),
            pl.BlockSpec((1, D_MODEL), lambda i: (0, 0)),
            w_spec,
        ],
        out_specs=out_specs,
        out_shape=out_shape,
        compiler_params=pltpu.CompilerParams(
            dimension_semantics=("parallel",), vmem_limit_bytes=VMEM_LIMIT_BYTES),
        name="proj_cast" if cast else "proj",
    )(x, norm_g, w_in)


def _attn_kernel(q_ref, k_ref, v_ref, o_ref, acc_ref, car_ref, *, tq, hq, pos0):
    win = ATTN_WINDOW
    q_start = pos0 + pl.program_id(1) * tq
    subs = range(tq // hq)
    pairs = range(SB_HEADS // 2)
    pair_cols = [slice(p * LANES, (p + 1) * LANES) for p in pairs]

    lane = lax.broadcasted_iota(jnp.int32, (1, LANES), 1)
    first_head = lane < SB_HEAD_DIM
    r = lax.broadcasted_iota(jnp.int32, (win, win), 0)
    c = lax.broadcasted_iota(jnp.int32, (win, win), 1)
    newer = jnp.where(r > c, 1.0, 0.0).astype(BF16)

    def softplus2(z):
        return jnp.maximum(z, 0.0) + jnp.log2(1.0 + jnp.exp2(-jnp.abs(z)))

    def visit(m, mode):
        k_ws, v_ws, valids = [], [], []
        for j in subs:
            row0 = q_start + j * hq
            win_end = row0 + hq - m * win
            q_pos = row0 + lax.broadcasted_iota(jnp.int32, (hq, 1), 0)
            if mode == "inner":
                start = pl.multiple_of(win_end - win, hq)
                k_pos = (win_end - LANES) + lax.broadcasted_iota(jnp.int32, (1, LANES), 1)
                valids.append(k_pos < q_pos)
            else:
                start = pl.multiple_of(jnp.maximum(win_end - win, 0), hq)
                k_pos = start + lax.broadcasted_iota(jnp.int32, (1, win), 1)
                valids.append(k_pos < (q_pos if mode == "edge" else win_end))
            k_ws.append(k_ref[pl.ds(start, win), :])
            v_ws.append(v_ref[pl.ds(start, win), :])

        def mask(x, j):
            if mode == "inner":
                return jnp.concatenate(
                    [x[:, :win - LANES],
                     jnp.where(valids[j], x[:, win - LANES:], MASKED_LOGIT)], axis=1)
            return jnp.where(valids[j], x, MASKED_LOGIT)

        z_parts = []
        for j in subs:
            rows = slice(j * hq, (j + 1) * hq)
            for p in pairs:
                q_p = q_ref[rows, pair_cols[p]]
                zero = jnp.zeros_like(q_p)
                q_2 = jnp.concatenate([jnp.where(first_head, q_p, zero),
                                       jnp.where(first_head, zero, q_p)], axis=0)
                z_2 = lax.dot_general(q_2, k_ws[j][:, pair_cols[p]], (((1,), (1,)), ((), ())),
                                      preferred_element_type=F32)
                z_parts += [mask(z_2[:hq], j), mask(z_2[hq:], j)]
        z = jnp.concatenate(z_parts, axis=0)
        s = softplus2(z)
        sums = jnp.dot(s.astype(BF16), newer, preferred_element_type=F32)
        arg = (z - s) - sums
        total = sums[:, 0:LANES] + s[:, 0:LANES]
        if mode == "older":
            car = car_ref[...]
            car_row = jnp.broadcast_to(car[:, 0:1], car.shape)
            arg = arg - jnp.concatenate([car_row] * (win // LANES), axis=1)
            total = car + total
        car_ref[...] = total
        w = jnp.exp2(arg).astype(BF16)

        i = 0
        for j in subs:
            rows = slice(j * hq, (j + 1) * hq)
            for p in pairs:
                o_2 = jnp.dot(w[i * hq:(i + 2) * hq], v_ws[j][:, pair_cols[p]],
                              preferred_element_type=F32)
                i += 2
                o_p = jnp.where(first_head, o_2[:hq], o_2[hq:])
                if mode == "older":
                    acc_ref[rows, pair_cols[p]] += o_p
                else:
                    acc_ref[rows, pair_cols[p]] = o_p
        least = jnp.min(total, axis=0, keepdims=True)
        least = jnp.min(jnp.where(lane == 0, least, SURVIVAL_UNDERFLOW_LOG2))
        return (least >= SURVIVAL_UNDERFLOW_LOG2).astype(jnp.int32)

    done = lax.cond(q_start + hq >= win, lambda: visit(0, "inner"), lambda: visit(0, "edge"))

    def cond(carry):
        m, done = carry
        return jnp.logical_and(q_start + tq - m * win > 0, done == 0)

    def body(carry):
        m, _ = carry
        return m + 1, visit(m, "older")

    lax.while_loop(cond, body, (jnp.int32(1), done))
    o_ref[...] = acc_ref[...].astype(BF16)


def _attn(q_arr, q_col, k_arr, k_col, v_arr, v_col, *, batch, q_len, k_len, pos0, tq, hq):
    assert tq % hq == 0 and q_len % tq == 0 and pos0 % tq == 0 and ATTN_WINDOW % hq == 0
    assert pos0 + q_len <= k_len and k_len >= ATTN_WINDOW and hq % BF16_SUBLANES == 0
    nq = q_len // tq
    stacked_rows = (tq // hq) * SB_HEADS * hq
    return pl.pallas_call(
        functools.partial(_attn_kernel, tq=tq, hq=hq, pos0=pos0),
        grid=(batch, nq),
        in_specs=[
            pl.BlockSpec((tq, D_SB), lambda b, i: (b * nq + i, q_col)),
            pl.BlockSpec((k_len, D_SB), lambda b, i: (b, k_col)),
            pl.BlockSpec((k_len, D_SB), lambda b, i: (b, v_col)),
        ],
        out_specs=pl.BlockSpec((tq, D_SB), lambda b, i: (b * nq + i, 0)),
        out_shape=jax.ShapeDtypeStruct((batch * q_len, D_SB), BF16),
        scratch_shapes=[pltpu.VMEM((tq, D_SB), F32), pltpu.VMEM((stacked_rows, LANES), F32)],
        compiler_params=pltpu.CompilerParams(
            dimension_semantics=("parallel", "arbitrary"), vmem_limit_bytes=VMEM_LIMIT_BYTES),
        name="attn",
    )(q_arr, k_arr, v_arr)


def _merge_kernel(x_ref, u_ref, halo_ref, o_ref, gate_ref, pw_ref, ps_ref, wbp_ref, wbs_ref,
                  wout_ref, y_ref, *copy_refs, tm, pos0, tiles_per_seq):
    def weight(ref, position, *idx):
        w = ref[idx] if idx else ref[...]
        if copy_refs:
            w = w.astype(BF16)
            if idx:
                copy_refs[position][idx] = w
            else:
                copy_refs[position][...] = w
        return w

    w_sb = weight(wbs_ref, 2)
    streams = halo_ref.shape[0]
    rows = tm // streams
    if streams == 1:
        first_pos = pos0 + (pl.program_id(0) % tiles_per_seq) * tm
        pos = first_pos + lax.broadcasted_iota(jnp.int32, (tm, 1), 0)
    else:
        pos = jnp.concatenate([pos0 + lax.broadcasted_iota(jnp.int32, (rows, 1), 0)] * streams,
                              axis=0)

    n_groups = len(POOL_WINDOWS)
    n_chunks = n_groups if tm % (BF16_SUBLANES * n_groups) == 0 else 1
    quarter = tm // n_chunks
    span = POOL_HALO + rows
    run = jnp.concatenate(
        [part for b in range(streams)
         for part in (halo_ref[b], u_ref[b * rows:(b + 1) * rows, :])], axis=0)
    width, mixed, gated_sb = 1, [], []
    for gi, win in enumerate(POOL_WINDOWS):
        while width < win:
            run = run + pltpu.roll(run, width, 0)
            width *= 2
        assert width == win, "pool windows must be ascending powers of two"
        cols = slice(gi * POOL_GROUP, (gi + 1) * POOL_GROUP)
        cnt = jnp.minimum(win, pos + 1).astype(F32)
        tot = jnp.concatenate([run[b * span + POOL_HALO:(b + 1) * span, 0:POOL_GROUP]
                               for b in range(streams)], axis=0)
        d = (tot / cnt - u_ref[:, cols]).astype(BF16)
        mixed.append(jnp.dot(d, weight(pw_ref, 0, gi), preferred_element_type=F32)
                     * ps_ref[:, cols])
        if gi + 1 < n_groups:
            run = run[:, POOL_GROUP:]
        if gi < n_chunks:
            part = slice(gi * quarter, (gi + 1) * quarter)
            br_sb = jnp.dot(o_ref[part, :], w_sb, preferred_element_type=F32)
            gated_sb.append(
                _sigmoid(gate_ref[part, D_MODEL:2 * D_MODEL].astype(F32)) * br_sb)
    a = jnp.concatenate(mixed, axis=1).astype(BF16)

    br_pool = jnp.dot(a, weight(wbp_ref, 1), preferred_element_type=F32)
    merged = (_sigmoid(gate_ref[:, 0:D_MODEL].astype(F32)) * br_pool
              + jnp.concatenate(gated_sb, axis=0))
    y_ref[...] = x_ref[...] + jnp.dot(merged.astype(BF16), weight(wout_ref, 3),
                                      preferred_element_type=F32)


def _merge(x, u, halo, o, gate, pool_w, pool_scale, w_bp, w_bs, w_out, *, tm, pos0, seq):
    n = x.shape[0]
    const2 = lambda i: (0, 0)
    resident = pl.Buffered(1)
    streams = max(tm // seq, 1)
    assert tm % seq == 0 or seq % tm == 0
    weights = (pool_w, w_bp, w_bs, w_out)
    cast = pool_w.dtype == F32
    assert not cast or n == tm
    out_specs = [pl.BlockSpec((tm, D_MODEL), lambda i: (i, 0))]
    out_shape = [jax.ShapeDtypeStruct((n, D_MODEL), F32)]
    if cast:
        out_specs += [pl.BlockSpec(m.shape, lambda i, nd=m.ndim: (0,) * nd) for m in weights]
        out_shape += [jax.ShapeDtypeStruct(m.shape, BF16) for m in weights]
    return pl.pallas_call(
        functools.partial(_merge_kernel, tm=tm, pos0=pos0, tiles_per_seq=max(seq // tm, 1)),
        grid=(n // tm,),
        in_specs=[
            pl.BlockSpec((tm, D_MODEL), lambda i: (i, 0)),
            pl.BlockSpec((tm, D_POOL), lambda i: (i, 0)),
            pl.BlockSpec((streams, POOL_HALO, D_POOL), lambda i: (i, 0, 0)),
            pl.BlockSpec((tm, D_SB), lambda i: (i, 0)),
            pl.BlockSpec((tm, 2 * D_MODEL), lambda i: (i, 0)),
            pl.BlockSpec((len(POOL_WINDOWS), POOL_GROUP, POOL_GROUP), lambda i: (0, 0, 0)),
            pl.BlockSpec((1, D_POOL), const2),
            pl.BlockSpec((D_POOL, D_MODEL), const2, pipeline_mode=resident),
            pl.BlockSpec((D_SB, D_MODEL), const2, pipeline_mode=resident),
            pl.BlockSpec((D_MODEL, D_MODEL), const2, pipeline_mode=resident),
        ],
        out_specs=out_specs,
        out_shape=out_shape,
        compiler_params=pltpu.CompilerParams(
            dimension_semantics=("parallel",), vmem_limit_bytes=VMEM_LIMIT_BYTES),
        name="merge_cast" if cast else "merge",
    )(x, u, halo, o, gate, pool_w, pool_scale, w_bp, w_bs, w_out)


def _layer(x, w, *, batch, seq, pos0, k_past, v_past, pool_prefix, tm_ffn, tm_mix, tm_merge, tq,
           hq):
    assert seq >= POOL_STATE
    w_bf16 = {}

    def ffn(x_in, prefix, final_g=None):
        names = [prefix + "_gate", prefix + "_up", prefix + "_down"]
        res = _ffn(x_in, w[prefix + "_norm"], *[w[nm] for nm in names], final_g,
                   tm=tm_ffn, tf=FFN_CHUNK)
        if isinstance(res, (list, tuple)):
            w_bf16.update(zip(names, res[1:]))
            return res[0]
        return res

    x1 = ffn(x, "ffn1")
    u, qkv, k_hm, v_hm, gate, *w_in_copy = _proj(x1, w["mix_norm"], w["w_in"], batch=batch,
                                                 seq=seq, tm=tm_mix)
    w_bf16.update(zip(["w_in"], w_in_copy))

    if k_past is None:
        o = _attn(qkv, 0, qkv, 1, qkv, 2, batch=batch, q_len=seq, k_len=seq, pos0=0, tq=tq, hq=hq)
    else:
        def with_past(past, col):
            new = qkv[:, col * D_SB:(col + 1) * D_SB].reshape(batch, seq, D_SB)
            return jnp.concatenate([past, new], axis=1).reshape(batch * (pos0 + seq), D_SB)
        o = _attn(qkv, 0, with_past(k_past, 1), 0, with_past(v_past, 2), 0,
                  batch=batch, q_len=seq, k_len=pos0 + seq, pos0=pos0, tq=tq, hq=hq)

    piece = min(tm_merge, seq)
    pieces = seq // piece
    u4 = u.reshape(batch, pieces, piece, D_POOL)
    first = jnp.pad(pool_prefix, ((0, 0), (1, 0), (0, 0)))[:, None]
    halo = jnp.concatenate([first, u4[:, :-1, piece - POOL_HALO:, :]], axis=1)
    halo = halo.reshape(batch * pieces, POOL_HALO, D_POOL)

    merge_names = ["pool_w", "w_branch_pool", "w_branch_sb", "w_out"]
    x2, *copies = _merge(x1, u, halo, o, gate, w["pool_w"], w["pool_scale"], w["w_branch_pool"],
                         w["w_branch_sb"], w["w_out"], tm=tm_merge, pos0=pos0, seq=seq)
    w_bf16.update(zip(merge_names, copies))
    y = ffn(x2, "ffn2", w["final_norm"])
    new_pool = u.reshape(batch, seq, D_POOL)[:, seq - POOL_STATE:, :]
    return (y.reshape(batch, seq, D_MODEL), k_hm, v_hm, new_pool), w_bf16


def kernel(x_prompt, x_sample, cache_k, cache_v, state_pool, ffn1_norm, ffn1_gate, ffn1_up, ffn1_down, mix_norm, w_in, pool_w, pool_scale, w_branch_pool, w_branch_sb, w_out, ffn2_norm, ffn2_gate, ffn2_up, ffn2_down, final_norm):
    assert ffn1_norm.shape[0] == 1, "single-layer kernel"
    w = dict(w_in=w_in[0], pool_w=pool_w[0], w_branch_pool=w_branch_pool[0],
             w_branch_sb=w_branch_sb[0], w_out=w_out[0],
             ffn1_gate=ffn1_gate[0], ffn1_up=ffn1_up[0], ffn1_down=ffn1_down[0],
             ffn2_gate=ffn2_gate[0], ffn2_up=ffn2_up[0], ffn2_down=ffn2_down[0])
    w.update(ffn1_norm=ffn1_norm, mix_norm=mix_norm, pool_scale=pool_scale, ffn2_norm=ffn2_norm,
             final_norm=final_norm.reshape(1, D_MODEL))

    b_p, s_p, _ = x_prompt.shape
    b_d, s_d, _ = x_sample.shape
    past = cache_k.shape[3]

    def token_major(cache):
        return cache[0].transpose(0, 2, 1, 3).reshape(b_d, past, D_SB).astype(BF16)

    (y_d, k_d, v_d, pool_d), w_bf16 = _layer(
        x_sample.reshape(b_d * s_d, D_MODEL), w, batch=b_d, seq=s_d, pos0=past,
        k_past=token_major(cache_k), v_past=token_major(cache_v), pool_prefix=state_pool[0],
        tm_ffn=b_d * s_d, tm_mix=b_d * s_d, tm_merge=b_d * s_d, tq=s_d, hq=s_d)

    (y_p, k_p, v_p, pool_p), _ = _layer(
        x_prompt.reshape(b_p * s_p, D_MODEL), {**w, **w_bf16}, batch=b_p, seq=s_p, pos0=0,
        k_past=None, v_past=None, pool_prefix=jnp.zeros((b_p, POOL_STATE, D_POOL), F32),
        tm_ffn=1024, tm_mix=1024, tm_merge=1024, tq=512, hq=64)

    return (y_p, y_d, k_p[None], v_p[None], pool_p[None], k_d[None], v_d[None], pool_d[None])
```

```python
import functools

import jax
import jax.numpy as jnp
from jax import lax
from jax.experimental import pallas as pl
from jax.experimental.pallas import tpu as pltpu

F32 = jnp.float32
BF16 = jnp.bfloat16

D_MODEL = 1024
D_FF = 4 * D_MODEL
D_POOL = D_MODEL // 2
POOL_WINDOWS = (2, 4, 8, 16)
POOL_GROUP = D_POOL // len(POOL_WINDOWS)
POOL_STATE = max(POOL_WINDOWS) - 1
POOL_HALO = POOL_STATE + 1
SB_HEADS = 8
SB_HEAD_DIM = 64
D_SB = SB_HEADS * SB_HEAD_DIM
D_IN = D_POOL + 3 * D_SB + 2 * D_MODEL
RMS_EPS = 1e-6
LANES = 128
BF16_SUBLANES = 16
VMEM_LIMIT_BYTES = 56 * 1024 * 1024
FFN_CHUNK = 1024
ATTN_WINDOW = 2 * LANES

LOG2E = 1.4426950408889634
SURVIVAL_UNDERFLOW_LOG2 = 150.0 * (1.0 + 2.0 ** -8)
MASKED_LOGIT = -1e30


def _rmsnorm(x, g):
    ms = jnp.mean(x * x, axis=-1, keepdims=True)
    return (x * lax.rsqrt(ms + RMS_EPS)) * g


def _sigmoid(x):
    return 0.5 * jnp.tanh(0.5 * x) + 0.5


def _ffn_kernel(*refs, n_ff, final, cast):
    refs = list(refs)
    x_ref, xnext_ref, g_ref, wg_ref, wu_ref, wd_ref = refs[:6]
    del refs[:6]
    fg_ref = refs.pop(0) if final else None
    o_ref = refs.pop(0)
    if cast:
        wg_out, wu_out, wd_out = refs[:3]
        del refs[:3]
    xn_ref, acc_ref = refs
    i = pl.program_id(0)
    j = pl.program_id(1)
    slot = i % 2

    @pl.when(jnp.logical_and(i == 0, j == 0))
    def _():
        xn_ref[0] = _rmsnorm(x_ref[...], g_ref[...]).astype(BF16)

    def chunk():
        wg, wu, wd = wg_ref[...], wu_ref[...], wd_ref[...]
        if cast:
            wg, wu, wd = wg.astype(BF16), wu.astype(BF16), wd.astype(BF16)
            wg_out[...] = wg
            wu_out[...] = wu
            wd_out[...] = wd
        xn = xn_ref[slot]
        gate = jnp.dot(xn, wg, preferred_element_type=F32)
        up = jnp.dot(xn, wu, preferred_element_type=F32)
        h = (gate * _sigmoid(gate) * up).astype(BF16)
        return jnp.dot(h, wd, preferred_element_type=F32)

    def prepare_next(rows):
        xn_ref[1 - slot, rows, :] = _rmsnorm(xnext_ref[rows, :], g_ref[...]).astype(BF16)

    piece = x_ref.shape[0] // n_ff

    def first():
        acc_ref[...] = chunk()
        prepare_next(pl.ds(0, piece))

    def middle():
        acc_ref[...] += chunk()
        prepare_next(pl.ds(pl.multiple_of(j * piece, piece), piece))

    def last():
        y = x_ref[...] + 0.5 * (acc_ref[...] + chunk())
        if final:
            y = _rmsnorm(y, fg_ref[...])
        o_ref[...] = y
        prepare_next(pl.ds((n_ff - 1) * piece, piece))

    case = jnp.where(j == 0, 0, jnp.where(j == n_ff - 1, 2, 1))
    lax.switch(case, [first, middle, last])


def _ffn(x, norm_g, wg, wu, wd, final_g=None, *, tm, tf):
    n = x.shape[0]
    n_rows = n // tm
    n_ff = D_FF // tf
    assert n_ff >= 2 and tm % (BF16_SUBLANES * n_ff) == 0
    final = final_g is not None
    cast = wg.dtype == F32
    assert not cast or n_rows == 1
    row = pl.BlockSpec((1, D_MODEL), lambda i, j: (0, 0))
    w_specs = [
        pl.BlockSpec((D_MODEL, tf), lambda i, j: (0, j)),
        pl.BlockSpec((D_MODEL, tf), lambda i, j: (0, j)),
        pl.BlockSpec((tf, D_MODEL), lambda i, j: (j, 0)),
    ]
    in_specs = [
        pl.BlockSpec((tm, D_MODEL), lambda i, j: (i, 0)),
        pl.BlockSpec((tm, D_MODEL), lambda i, j: (jnp.minimum(i + 1, n_rows - 1), 0)),
        row,
    ] + w_specs
    args = [x, x, norm_g, wg, wu, wd]
    if final:
        in_specs.append(row)
        args.append(final_g)
    out_specs = [pl.BlockSpec((tm, D_MODEL), lambda i, j: (i, 0))]
    out_shape = [jax.ShapeDtypeStruct((n, D_MODEL), F32)]
    if cast:
        out_specs += w_specs
        out_shape += [jax.ShapeDtypeStruct(m.shape, BF16) for m in (wg, wu, wd)]
    outs = pl.pallas_call(
        functools.partial(_ffn_kernel, n_ff=n_ff, final=final, cast=cast),
        grid=(n_rows, n_ff),
        in_specs=in_specs,
        out_specs=out_specs,
        out_shape=out_shape,
        scratch_shapes=[pltpu.VMEM((2, tm, D_MODEL), BF16), pltpu.VMEM((tm, D_MODEL), F32)],
        compiler_params=pltpu.CompilerParams(
            dimension_semantics=("arbitrary", "arbitrary"), vmem_limit_bytes=VMEM_LIMIT_BYTES),
        name=("ffn_final" if final else "ffn") + ("_cast" if cast else ""),
    )(*args)
    return outs if cast else outs[0]


def _proj_kernel(x_ref, g_ref, w_ref, u_ref, qkv_ref, k_ref, v_ref, gate_ref, w_out_ref=None):
    h = _rmsnorm(x_ref[...], g_ref[...]).astype(BF16)

    def mm(c0, c1):
        w = w_ref[:, c0:c1]
        if w_out_ref is not None:
            w = w.astype(BF16)
            w_out_ref[:, c0:c1] = w
        return jnp.dot(h, w, preferred_element_type=F32)

    c = 0
    u_ref[...] = mm(c, c + D_POOL)
    c += D_POOL
    qkv_ref[:, 0:D_SB] = (mm(c, c + D_SB) * (LOG2E * SB_HEAD_DIM ** -0.5)).astype(BF16)
    c += D_SB
    for idx, hm_ref in ((1, k_ref), (2, v_ref)):
        t = mm(c, c + D_SB)
        c += D_SB
        qkv_ref[:, idx * D_SB:(idx + 1) * D_SB] = t.astype(BF16)
        streams, _, rows, _ = hm_ref.shape
        for b in range(streams):
            for hd in range(SB_HEADS):
                hm_ref[b, hd, :, :] = t[b * rows:(b + 1) * rows,
                                        hd * SB_HEAD_DIM:(hd + 1) * SB_HEAD_DIM]
    gate_ref[:, 0:D_MODEL] = mm(c, c + D_MODEL).astype(BF16)
    c += D_MODEL
    gate_ref[:, D_MODEL:2 * D_MODEL] = mm(c, c + D_MODEL).astype(BF16)


def _proj(x, norm_g, w_in, *, batch, seq, tm):
    n = x.shape[0]
    cast = w_in.dtype == F32
    assert not cast or n == tm
    if tm <= seq:
        tiles_per_seq = seq // tm
        hm_spec = pl.BlockSpec((1, SB_HEADS, tm, SB_HEAD_DIM),
                               lambda i: (i // tiles_per_seq, 0, i % tiles_per_seq, 0))
    else:
        assert tm % seq == 0
        hm_spec = pl.BlockSpec((tm // seq, SB_HEADS, seq, SB_HEAD_DIM), lambda i: (i, 0, 0, 0))
    hm_shape = jax.ShapeDtypeStruct((batch, SB_HEADS, seq, SB_HEAD_DIM), F32)
    w_spec = pl.BlockSpec((D_MODEL, D_IN), lambda i: (0, 0), pipeline_mode=pl.Buffered(1))
    out_specs = [
        pl.BlockSpec((tm, D_POOL), lambda i: (i, 0)),
        pl.BlockSpec((tm, 3 * D_SB), lambda i: (i, 0)),
        hm_spec,
        hm_spec,
        pl.BlockSpec((tm, 2 * D_MODEL), lambda i: (i, 0)),
    ]
    out_shape = [
        jax.ShapeDtypeStruct((n, D_POOL), F32),
        jax.ShapeDtypeStruct((n, 3 * D_SB), BF16),
        hm_shape,
        hm_shape,
        jax.ShapeDtypeStruct((n, 2 * D_MODEL), BF16),
    ]
    if cast:
        out_specs.append(pl.BlockSpec((D_MODEL, D_IN), lambda i: (0, 0)))
        out_shape.append(jax.ShapeDtypeStruct((D_MODEL, D_IN), BF16))
    return pl.pallas_call(
        _proj_kernel,
        grid=(n // tm,),
        in_specs=[
            pl.BlockSpec((tm, D_MODEL), lambda i: (i, 0)),
            pl.BlockSpec((1, D_MODEL), lambda i: (0, 0)),
            w_spec,
        ],
        out_specs=out_specs,
        out_shape=out_shape,
        compiler_params=pltpu.CompilerParams(
            dimension_semantics=("parallel",), vmem_limit_bytes=VMEM_LIMIT_BYTES),
        name="proj_cast" if cast else "proj",
    )(x, norm_g, w_in)


def _attn_kernel(q_ref, k_ref, v_ref, o_ref, acc_ref, car_ref, *, tq, hq, pos0):
    win = ATTN_WINDOW
    q_start = pos0 + pl.program_id(1) * tq
    subs = range(tq // hq)
    pairs = range(SB_HEADS // 2)
    pair_cols = [slice(p * LANES, (p + 1) * LANES) for p in pairs]

    lane = lax.broadcasted_iota(jnp.int32, (1, LANES), 1)
    first_head = lane < SB_HEAD_DIM
    r = lax.broadcasted_iota(jnp.int32, (win, win), 0)
    c = lax.broadcasted_iota(jnp.int32, (win, win), 1)
    newer = jnp.where(r > c, 1.0, 0.0).astype(BF16)

    def softplus2(z):
        return jnp.maximum(z, 0.0) + jnp.log2(1.0 + jnp.exp2(-jnp.abs(z)))

    def visit(m, mode):
        k_ws, v_ws, valids = [], [], []
        for j in subs:
            row0 = q_start + j * hq
            win_end = row0 + hq - m * win
            q_pos = row0 + lax.broadcasted_iota(jnp.int32, (hq, 1), 0)
            if mode == "inner":
                start = pl.multiple_of(win_end - win, hq)
                k_pos = (win_end - LANES) + lax.broadcasted_iota(jnp.int32, (1, LANES), 1)
                valids.append(k_pos < q_pos)
            else:
                start = pl.multiple_of(jnp.maximum(win_end - win, 0), hq)
                k_pos = start + lax.broadcasted_iota(jnp.int32, (1, win), 1)
                valids.append(k_pos < (q_pos if mode == "edge" else win_end))
            k_ws.append(k_ref[pl.ds(start, win), :])
            v_ws.append(v_ref[pl.ds(start, win), :])

        def mask(x, j):
            if mode == "inner":
                return jnp.concatenate(
                    [x[:, :win - LANES],
                     jnp.where(valids[j], x[:, win - LANES:], MASKED_LOGIT)], axis=1)
            return jnp.where(valids[j], x, MASKED_LOGIT)

        z_parts = []
        for j in subs:
            rows = slice(j * hq, (j + 1) * hq)
            for p in pairs:
                q_p = q_ref[rows, pair_cols[p]]
                zero = jnp.zeros_like(q_p)
                q_2 = jnp.concatenate([jnp.where(first_head, q_p, zero),
                                       jnp.where(first_head, zero, q_p)], axis=0)
                z_2 = lax.dot_general(q_2, k_ws[j][:, pair_cols[p]], (((1,), (1,)), ((), ())),
                                      preferred_element_type=F32)
                z_parts += [mask(z_2[:hq], j), mask(z_2[hq:], j)]
        z = jnp.concatenate(z_parts, axis=0)
        s = softplus2(z)
        sums = jnp.dot(s.astype(BF16), newer, preferred_element_type=F32)
        arg = (z - s) - sums
        total = sums[:, 0:LANES] + s[:, 0:LANES]
        if mode == "older":
            car = car_ref[...]
            car_row = jnp.broadcast_to(car[:, 0:1], car.shape)
            arg = arg - jnp.concatenate([car_row] * (win // LANES), axis=1)
            total = car + total
        car_ref[...] = total
        w = jnp.exp2(arg).astype(BF16)

        i = 0
        for j in subs:
            rows = slice(j * hq, (j + 1) * hq)
            for p in pairs:
                o_2 = jnp.dot(w[i * hq:(i + 2) * hq], v_ws[j][:, pair_cols[p]],
                              preferred_element_type=F32)
                i += 2
                o_p = jnp.where(first_head, o_2[:hq], o_2[hq:])
                if mode == "older":
                    acc_ref[rows, pair_cols[p]] += o_p
                else:
                    acc_ref[rows, pair_cols[p]] = o_p
        least = jnp.min(total, axis=0, keepdims=True)
        least = jnp.min(jnp.where(lane == 0, least, SURVIVAL_UNDERFLOW_LOG2))
        return (least >= SURVIVAL_UNDERFLOW_LOG2).astype(jnp.int32)

    done = lax.cond(q_start + hq >= win, lambda: visit(0, "inner"), lambda: visit(0, "edge"))

    def cond(carry):
        m, done = carry
        return jnp.logical_and(q_start + tq - m * win > 0, done == 0)

    def body(carry):
        m, _ = carry
        return m + 1, visit(m, "older")

    lax.while_loop(cond, body, (jnp.int32(1), done))
    o_ref[...] = acc_ref[...].astype(BF16)


def _attn(q_arr, q_col, k_arr, k_col, v_arr, v_col, *, batch, q_len, k_len, pos0, tq, hq):
    assert tq % hq == 0 and q_len % tq == 0 and pos0 % tq == 0 and ATTN_WINDOW % hq == 0
    assert pos0 + q_len <= k_len and k_len >= ATTN_WINDOW and hq % BF16_SUBLANES == 0
    nq = q_len // tq
    stacked_rows = (tq // hq) * SB_HEADS * hq
    return pl.pallas_call(
        functools.partial(_attn_kernel, tq=tq, hq=hq, pos0=pos0),
        grid=(batch, nq),
        in_specs=[
            pl.BlockSpec((tq, D_SB), lambda b, i: (b * nq + i, q_col)),
            pl.BlockSpec((k_len, D_SB), lambda b, i: (b, k_col)),
            pl.BlockSpec((k_len, D_SB), lambda b, i: (b, v_col)),
        ],
        out_specs=pl.BlockSpec((tq, D_SB), lambda b, i: (b * nq + i, 0)),
        out_shape=jax.ShapeDtypeStruct((batch * q_len, D_SB), BF16),
        scratch_shapes=[pltpu.VMEM((tq, D_SB), F32), pltpu.VMEM((stacked_rows, LANES), F32)],
        compiler_params=pltpu.CompilerParams(
            dimension_semantics=("parallel", "arbitrary"), vmem_limit_bytes=VMEM_LIMIT_BYTES),
        name="attn",
    )(q_arr, k_arr, v_arr)


def _merge_kernel(x_ref, u_ref, halo_ref, o_ref, gate_ref, pw_ref, ps_ref, wbp_ref, wbs_ref,
                  wout_ref, y_ref, *copy_refs, tm, pos0, tiles_per_seq):
    def weight(ref, position, *idx):
        w = ref[idx] if idx else ref[...]
        if copy_refs:
            w = w.astype(BF16)
            if idx:
                copy_refs[position][idx] = w
            else:
                copy_refs[position][...] = w
        return w

    w_sb = weight(wbs_ref, 2)
    streams = halo_ref.shape[0]
    rows = tm // streams
    if streams == 1:
        first_pos = pos0 + (pl.program_id(0) % tiles_per_seq) * tm
        pos = first_pos + lax.broadcasted_iota(jnp.int32, (tm, 1), 0)
    else:
        pos = jnp.concatenate([pos0 + lax.broadcasted_iota(jnp.int32, (rows, 1), 0)] * streams,
                              axis=0)

    n_groups = len(POOL_WINDOWS)
    n_chunks = n_groups if tm % (BF16_SUBLANES * n_groups) == 0 else 1
    quarter = tm // n_chunks
    span = POOL_HALO + rows
    run = jnp.concatenate(
        [part for b in range(streams)
         for part in (halo_ref[b], u_ref[b * rows:(b + 1) * rows, :])], axis=0)
    width, mixed, gated_sb = 1, [], []
    for gi, win in enumerate(POOL_WINDOWS):
        while width < win:
            run = run + pltpu.roll(run, width, 0)
            width *= 2
        assert width == win, "pool windows must be ascending powers of two"
        cols = slice(gi * POOL_GROUP, (gi + 1) * POOL_GROUP)
        cnt = jnp.minimum(win, pos + 1).astype(F32)
        tot = jnp.concatenate([run[b * span + POOL_HALO:(b + 1) * span, 0:POOL_GROUP]
                               for b in range(streams)], axis=0)
        d = (tot / cnt - u_ref[:, cols]).astype(BF16)
        mixed.append(jnp.dot(d, weight(pw_ref, 0, gi), preferred_element_type=F32)
                     * ps_ref[:, cols])
        if gi + 1 < n_groups:
            run = run[:, POOL_GROUP:]
        if gi < n_chunks:
            part = slice(gi * quarter, (gi + 1) * quarter)
            br_sb = jnp.dot(o_ref[part, :], w_sb, preferred_element_type=F32)
            gated_sb.append(
                _sigmoid(gate_ref[part, D_MODEL:2 * D_MODEL].astype(F32)) * br_sb)
    a = jnp.concatenate(mixed, axis=1).astype(BF16)

    br_pool = jnp.dot(a, weight(wbp_ref, 1), preferred_element_type=F32)
    merged = (_sigmoid(gate_ref[:, 0:D_MODEL].astype(F32)) * br_pool
              + jnp.concatenate(gated_sb, axis=0))
    y_ref[...] = x_ref[...] + jnp.dot(merged.astype(BF16), weight(wout_ref, 3),
                                      preferred_element_type=F32)


def _merge(x, u, halo, o, gate, pool_w, pool_scale, w_bp, w_bs, w_out, *, tm, pos0, seq):
    n = x.shape[0]
    const2 = lambda i: (0, 0)
    resident = pl.Buffered(1)
    streams = max(tm // seq, 1)
    assert tm % seq == 0 or seq % tm == 0
    weights = (pool_w, w_bp, w_bs, w_out)
    cast = pool_w.dtype == F32
    assert not cast or n == tm
    out_specs = [pl.BlockSpec((tm, D_MODEL), lambda i: (i, 0))]
    out_shape = [jax.ShapeDtypeStruct((n, D_MODEL), F32)]
    if cast:
        out_specs += [pl.BlockSpec(m.shape, lambda i, nd=m.ndim: (0,) * nd) for m in weights]
        out_shape += [jax.ShapeDtypeStruct(m.shape, BF16) for m in weights]
    return pl.pallas_call(
        functools.partial(_merge_kernel, tm=tm, pos0=pos0, tiles_per_seq=max(seq // tm, 1)),
        grid=(n // tm,),
        in_specs=[
            pl.BlockSpec((tm, D_MODEL), lambda i: (i, 0)),
            pl.BlockSpec((tm, D_POOL), lambda i: (i, 0)),
            pl.BlockSpec((streams, POOL_HALO, D_POOL), lambda i: (i, 0, 0)),
            pl.BlockSpec((tm, D_SB), lambda i: (i, 0)),
            pl.BlockSpec((tm, 2 * D_MODEL), lambda i: (i, 0)),
            pl.BlockSpec((len(POOL_WINDOWS), POOL_GROUP, POOL_GROUP), lambda i: (0, 0, 0)),
            pl.BlockSpec((1, D_POOL), const2),
            pl.BlockSpec((D_POOL, D_MODEL), const2, pipeline_mode=resident),
            pl.BlockSpec((D_SB, D_MODEL), const2, pipeline_mode=resident),
            pl.BlockSpec((D_MODEL, D_MODEL), const2, pipeline_mode=resident),
        ],
        out_specs=out_specs,
        out_shape=out_shape,
        compiler_params=pltpu.CompilerParams(
            dimension_semantics=("parallel",), vmem_limit_bytes=VMEM_LIMIT_BYTES),
        name="merge_cast" if cast else "merge",
    )(x, u, halo, o, gate, pool_w, pool_scale, w_bp, w_bs, w_out)


def _layer(x, w, *, batch, seq, pos0, k_past, v_past, pool_prefix, tm_ffn, tm_mix, tm_merge, tq,
           hq):
    assert seq >= POOL_STATE
    w_bf16 = {}

    def ffn(x_in, prefix, final_g=None):
        names = [prefix + "_gate", prefix + "_up", prefix + "_down"]
        res = _ffn(x_in, w[prefix + "_norm"], *[w[nm] for nm in names], final_g,
                   tm=tm_ffn, tf=FFN_CHUNK)
        if isinstance(res, (list, tuple)):
            w_bf16.update(zip(names, res[1:]))
            return res[0]
        return res

    x1 = ffn(x, "ffn1")
    u, qkv, k_hm, v_hm, gate, *w_in_copy = _proj(x1, w["mix_norm"], w["w_in"], batch=batch,
                                                 seq=seq, tm=tm_mix)
    w_bf16.update(zip(["w_in"], w_in_copy))

    if k_past is None:
        o = _attn(qkv, 0, qkv, 1, qkv, 2, batch=batch, q_len=seq, k_len=seq, pos0=0, tq=tq, hq=hq)
    else:
        def with_past(past, col):
            new = qkv[:, col * D_SB:(col + 1) * D_SB].reshape(batch, seq, D_SB)
            return jnp.concatenate([past, new], axis=1).reshape(batch * (pos0 + seq), D_SB)
        o = _attn(qkv, 0, with_past(k_past, 1), 0, with_past(v_past, 2), 0,
                  batch=batch, q_len=seq, k_len=pos0 + seq, pos0=pos0, tq=tq, hq=hq)

    piece = min(tm_merge, seq)
    pieces = seq // piece
    u4 = u.reshape(batch, pieces, piece, D_POOL)
    first = jnp.pad(pool_prefix, ((0, 0), (1, 0), (0, 0)))[:, None]
    halo = jnp.concatenate([first, u4[:, :-1, piece - POOL_HALO:, :]], axis=1)
    halo = halo.reshape(batch * pieces, POOL_HALO, D_POOL)

    merge_names = ["pool_w", "w_branch_pool", "w_branch_sb", "w_out"]
    x2, *copies = _merge(x1, u, halo, o, gate, w["pool_w"], w["pool_scale"], w["w_branch_pool"],
                         w["w_branch_sb"], w["w_out"], tm=tm_merge, pos0=pos0, seq=seq)
    w_bf16.update(zip(merge_names, copies))
    y = ffn(x2, "ffn2", w["final_norm"])
    new_pool = u.reshape(batch, seq, D_POOL)[:, seq - POOL_STATE:, :]
    return (y.reshape(batch, seq, D_MODEL), k_hm, v_hm, new_pool), w_bf16


def kernel(x_prompt, x_sample, cache_k, cache_v, state_pool, ffn1_norm, ffn1_gate, ffn1_up, ffn1_down, mix_norm, w_in, pool_w, pool_scale, w_branch_pool, w_branch_sb, w_out, ffn2_norm, ffn2_gate, ffn2_up, ffn2_down, final_norm):
    assert ffn1_norm.shape[0] == 1, "single-layer kernel"
    w = dict(w_in=w_in[0], pool_w=pool_w[0], w_branch_pool=w_branch_pool[0],
             w_branch_sb=w_branch_sb[0], w_out=w_out[0],
             ffn1_gate=ffn1_gate[0], ffn1_up=ffn1_up[0], ffn1_down=ffn1_down[0],
             ffn2_gate=ffn2_gate[0], ffn2_up=ffn2_up[0], ffn2_down=ffn2_down[0])
    w.update(ffn1_norm=ffn1_norm, mix_norm=mix_norm, pool_scale=pool_scale, ffn2_norm=ffn2_norm,
             final_norm=final_norm.reshape(1, D_MODEL))

    b_p, s_p, _ = x_prompt.shape
    b_d, s_d, _ = x_sample.shape
    past = cache_k.shape[3]

    def token_major(cache):
        return cache[0].astype(BF16).transpose(0, 2, 1, 3).reshape(b_d, past, D_SB)

    (y_d, k_d, v_d, pool_d), w_bf16 = _layer(
        x_sample.reshape(b_d * s_d, D_MODEL), w, batch=b_d, seq=s_d, pos0=past,
        k_past=token_major(cache_k), v_past=token_major(cache_v), pool_prefix=state_pool[0],
        tm_ffn=b_d * s_d, tm_mix=b_d * s_d, tm_merge=b_d * s_d, tq=s_d, hq=s_d)

    (y_p, k_p, v_p, pool_p), _ = _layer(
        x_prompt.reshape(b_p * s_p, D_MODEL), {**w, **w_bf16}, batch=b_p, seq=s_p, pos0=0,
        k_past=None, v_past=None, pool_prefix=jnp.zeros((b_p, POOL_STATE, D_POOL), F32),
        tm_ffn=1024, tm_mix=1024, tm_merge=1024, tq=512, hq=64)

    return (y_p, y_d, k_p[None], v_p[None], pool_p[None], k_d[None], v_d[None], pool_d[None])
```

```python
import functools

import jax
import jax.numpy as jnp
from jax import lax
from jax.experimental import pallas as pl
from jax.experimental.pallas import tpu as pltpu

F32 = jnp.float32
BF16 = jnp.bfloat16

D_MODEL = 1024
D_FF = 4 * D_MODEL
D_POOL = D_MODEL // 2
POOL_WINDOWS = (2, 4, 8, 16)
POOL_GROUP = D_POOL // len(POOL_WINDOWS)
POOL_STATE = max(POOL_WINDOWS) - 1
POOL_HALO = POOL_STATE + 1
SB_HEADS = 8
SB_HEAD_DIM = 64
D_SB = SB_HEADS * SB_HEAD_DIM
D_IN = D_POOL + 3 * D_SB + 2 * D_MODEL
RMS_EPS = 1e-6
LANES = 128
BF16_SUBLANES = 16
VMEM_LIMIT_BYTES = 56 * 1024 * 1024
FFN_CHUNK = 1024
ATTN_WINDOW = 2 * LANES

LOG2E = 1.4426950408889634
SURVIVAL_UNDERFLOW_LOG2 = 150.0 * (1.0 + 2.0 ** -8)
MASKED_LOGIT = -1e30


def _rmsnorm(x, g):
    ms = jnp.mean(x * x, axis=-1, keepdims=True)
    return (x * lax.rsqrt(ms + RMS_EPS)) * g


def _sigmoid(x):
    return 0.5 * jnp.tanh(0.5 * x) + 0.5


def _ffn_kernel(*refs, n_ff, final, cast):
    refs = list(refs)
    x_ref, xnext_ref, g_ref, wg_ref, wu_ref, wd_ref = refs[:6]
    del refs[:6]
    fg_ref = refs.pop(0) if final else None
    o_ref = refs.pop(0)
    if cast:
        wg_out, wu_out, wd_out = refs[:3]
        del refs[:3]
    xn_ref, acc_ref = refs
    i = pl.program_id(0)
    j = pl.program_id(1)
    slot = i % 2

    @pl.when(jnp.logical_and(i == 0, j == 0))
    def _():
        xn_ref[0] = _rmsnorm(x_ref[...], g_ref[...]).astype(BF16)

    def chunk():
        wg, wu, wd = wg_ref[...], wu_ref[...], wd_ref[...]
        if cast:
            wg, wu, wd = wg.astype(BF16), wu.astype(BF16), wd.astype(BF16)
            wg_out[...] = wg
            wu_out[...] = wu
            wd_out[...] = wd
        xn = xn_ref[slot]
        gate = jnp.dot(xn, wg, preferred_element_type=F32)
        up = jnp.dot(xn, wu, preferred_element_type=F32)
        h = (gate * _sigmoid(gate) * up).astype(BF16)
        return jnp.dot(h, wd, preferred_element_type=F32)

    def prepare_next(rows):
        xn_ref[1 - slot, rows, :] = _rmsnorm(xnext_ref[rows, :], g_ref[...]).astype(BF16)

    piece = x_ref.shape[0] // n_ff

    def first():
        acc_ref[...] = chunk()
        prepare_next(pl.ds(0, piece))

    def middle():
        acc_ref[...] += chunk()
        prepare_next(pl.ds(pl.multiple_of(j * piece, piece), piece))

    def last():
        y = x_ref[...] + 0.5 * (acc_ref[...] + chunk())
        if final:
            y = _rmsnorm(y, fg_ref[...])
        o_ref[...] = y
        prepare_next(pl.ds((n_ff - 1) * piece, piece))

    case = jnp.where(j == 0, 0, jnp.where(j == n_ff - 1, 2, 1))
    lax.switch(case, [first, middle, last])


def _ffn(x, norm_g, wg, wu, wd, final_g=None, *, tm, tf):
    n = x.shape[0]
    n_rows = n // tm
    n_ff = D_FF // tf
    assert n_ff >= 2 and tm % (BF16_SUBLANES * n_ff) == 0
    final = final_g is not None
    cast = wg.dtype == F32
    assert not cast or n_rows == 1
    row = pl.BlockSpec((1, D_MODEL), lambda i, j: (0, 0))
    w_specs = [
        pl.BlockSpec((D_MODEL, tf), lambda i, j: (0, j)),
        pl.BlockSpec((D_MODEL, tf), lambda i, j: (0, j)),
        pl.BlockSpec((tf, D_MODEL), lambda i, j: (j, 0)),
    ]
    in_specs = [
        pl.BlockSpec((tm, D_MODEL), lambda i, j: (i, 0)),
        pl.BlockSpec((tm, D_MODEL), lambda i, j: (jnp.minimum(i + 1, n_rows - 1), 0)),
        row,
    ] + w_specs
    args = [x, x, norm_g, wg, wu, wd]
    if final:
        in_specs.append(row)
        args.append(final_g)
    out_specs = [pl.BlockSpec((tm, D_MODEL), lambda i, j: (i, 0))]
    out_shape = [jax.ShapeDtypeStruct((n, D_MODEL), F32)]
    if cast:
        out_specs += w_specs
        out_shape += [jax.ShapeDtypeStruct(m.shape, BF16) for m in (wg, wu, wd)]
    outs = pl.pallas_call(
        functools.partial(_ffn_kernel, n_ff=n_ff, final=final, cast=cast),
        grid=(n_rows, n_ff),
        in_specs=in_specs,
        out_specs=out_specs,
        out_shape=out_shape,
        scratch_shapes=[pltpu.VMEM((2, tm, D_MODEL), BF16), pltpu.VMEM((tm, D_MODEL), F32)],
        compiler_params=pltpu.CompilerParams(
            dimension_semantics=("arbitrary", "arbitrary"), vmem_limit_bytes=VMEM_LIMIT_BYTES),
        name=("ffn_final" if final else "ffn") + ("_cast" if cast else ""),
    )(*args)
    return outs if cast else outs[0]


def _proj_kernel(x_ref, g_ref, w_ref, u_ref, qkv_ref, k_ref, v_ref, gate_ref, w_out_ref=None):
    h = _rmsnorm(x_ref[...], g_ref[...]).astype(BF16)

    def mm(c0, c1):
        w = w_ref[:, c0:c1]
        if w_out_ref is not None:
            w = w.astype(BF16)
            w_out_ref[:, c0:c1] = w
        return jnp.dot(h, w, preferred_element_type=F32)

    c = 0
    u_ref[...] = mm(c, c + D_POOL)
    c += D_POOL
    qkv_ref[:, 0:D_SB] = (mm(c, c + D_SB) * (LOG2E * SB_HEAD_DIM ** -0.5)).astype(BF16)
    c += D_SB
    for idx, hm_ref in ((1, k_ref), (2, v_ref)):
        t = mm(c, c + D_SB)
        c += D_SB
        qkv_ref[:, idx * D_SB:(idx + 1) * D_SB] = t.astype(BF16)
        streams, _, rows, _ = hm_ref.shape
        for b in range(streams):
            for hd in range(SB_HEADS):
                hm_ref[b, hd, :, :] = t[b * rows:(b + 1) * rows,
                                        hd * SB_HEAD_DIM:(hd + 1) * SB_HEAD_DIM]
    gate_ref[:, 0:D_MODEL] = mm(c, c + D_MODEL).astype(BF16)
    c += D_MODEL
    gate_ref[:, D_MODEL:2 * D_MODEL] = mm(c, c + D_MODEL).astype(BF16)


def _proj(x, norm_g, w_in, *, batch, seq, tm):
    n = x.shape[0]
    cast = w_in.dtype == F32
    assert not cast or n == tm
    if tm <= seq:
        tiles_per_seq = seq // tm
        hm_spec = pl.BlockSpec((1, SB_HEADS, tm, SB_HEAD_DIM),
                               lambda i: (i // tiles_per_seq, 0, i % tiles_per_seq, 0))
    else:
        assert tm % seq == 0
        hm_spec = pl.BlockSpec((tm // seq, SB_HEADS, seq, SB_HEAD_DIM), lambda i: (i, 0, 0, 0))
    hm_shape = jax.ShapeDtypeStruct((batch, SB_HEADS, seq, SB_HEAD_DIM), F32)
    w_spec = pl.BlockSpec((D_MODEL, D_IN), lambda i: (0, 0), pipeline_mode=pl.Buffered(1))
    out_specs = [
        pl.BlockSpec((tm, D_POOL), lambda i: (i, 0)),
        pl.BlockSpec((tm, 3 * D_SB), lambda i: (i, 0)),
        hm_spec,
        hm_spec,
        pl.BlockSpec((tm, 2 * D_MODEL), lambda i: (i, 0)),
    ]
    out_shape = [
        jax.ShapeDtypeStruct((n, D_POOL), F32),
        jax.ShapeDtypeStruct((n, 3 * D_SB), BF16),
        hm_shape,
        hm_shape,
        jax.ShapeDtypeStruct((n, 2 * D_MODEL), BF16),
    ]
    if cast:
        out_specs.append(pl.BlockSpec((D_MODEL, D_IN), lambda i: (0, 0)))
        out_shape.append(jax.ShapeDtypeStruct((D_MODEL, D_IN), BF16))
    return pl.pallas_call(
        _proj_kernel,
        grid=(n // tm,),
        in_specs=[
            pl.BlockSpec((tm, D_MODEL), lambda i: (i, 0)),
            pl.BlockSpec((1, D_MODEL), lambda i: (0, 0)),
            w_spec,
        ],
        out_specs=out_specs,
        out_shape=out_shape,
        compiler_params=pltpu.CompilerParams(
            dimension_semantics=("parallel",), vmem_limit_bytes=VMEM_LIMIT_BYTES),
        name="proj_cast" if cast else "proj",
    )(x, norm_g, w_in)


def _attn_kernel(q_ref, k_ref, v_ref, o_ref, acc_ref, car_ref, *, tq, hq, pos0):
    win = ATTN_WINDOW
    q_start = pos0 + pl.program_id(1) * tq
    subs = range(tq // hq)
    pairs = range(SB_HEADS // 2)
    pair_cols = [slice(p * LANES, (p + 1) * LANES) for p in pairs]

    lane = lax.broadcasted_iota(jnp.int32, (1, LANES), 1)
    first_head = lane < SB_HEAD_DIM
    r = lax.broadcasted_iota(jnp.int32, (win, win), 0)
    c = lax.broadcasted_iota(jnp.int32, (win, win), 1)
    newer = jnp.where(r > c, 1.0, 0.0).astype(BF16)

    def softplus2(z):
        return jnp.maximum(z, 0.0) + jnp.log2(1.0 + jnp.exp2(-jnp.abs(z)))

    def visit(m, mode):
        k_ws, v_ws, valids = [], [], []
        for j in subs:
            row0 = q_start + j * hq
            win_end = row0 + hq - m * win
            q_pos = row0 + lax.broadcasted_iota(jnp.int32, (hq, 1), 0)
            if mode == "inner":
                start = pl.multiple_of(win_end - win, hq)
                k_pos = (win_end - LANES) + lax.broadcasted_iota(jnp.int32, (1, LANES), 1)
                valids.append(k_pos < q_pos)
            else:
                start = pl.multiple_of(jnp.maximum(win_end - win, 0), hq)
                k_pos = start + lax.broadcasted_iota(jnp.int32, (1, win), 1)
                valids.append(k_pos < (q_pos if mode == "edge" else win_end))
            k_ws.append(k_ref[pl.ds(start, win), :])
            v_ws.append(v_ref[pl.ds(start, win), :])

        def mask(x, j):
            if mode == "inner":
                return jnp.concatenate(
                    [x[:, :win - LANES],
                     jnp.where(valids[j], x[:, win - LANES:], MASKED_LOGIT)], axis=1)
            return jnp.where(valids[j], x, MASKED_LOGIT)

        z_parts = []
        for j in subs:
            rows = slice(j * hq, (j + 1) * hq)
            for p in pairs:
                q_p = q_ref[rows, pair_cols[p]]
                zero = jnp.zeros_like(q_p)
                q_2 = jnp.concatenate([jnp.where(first_head, q_p, zero),
                                       jnp.where(first_head, zero, q_p)], axis=0)
                z_2 = lax.dot_general(q_2, k_ws[j][:, pair_cols[p]], (((1,), (1,)), ((), ())),
                                      preferred_element_type=F32)
                z_parts += [mask(z_2[:hq], j), mask(z_2[hq:], j)]
        z = jnp.concatenate(z_parts, axis=0)
        s = softplus2(z)
        sums = jnp.dot(s.astype(BF16), newer, preferred_element_type=F32)
        arg = (z - s) - sums
        total = sums[:, 0:LANES] + s[:, 0:LANES]
        if mode == "older":
            car = car_ref[...]
            car_row = jnp.broadcast_to(car[:, 0:1], car.shape)
            arg = arg - jnp.concatenate([car_row] * (win // LANES), axis=1)
            total = car + total
        car_ref[...] = total
        w = jnp.exp2(arg).astype(BF16)

        i = 0
        for j in subs:
            rows = slice(j * hq, (j + 1) * hq)
            for p in pairs:
                o_2 = jnp.dot(w[i * hq:(i + 2) * hq], v_ws[j][:, pair_cols[p]],
                              preferred_element_type=F32)
                i += 2
                o_p = jnp.where(first_head, o_2[:hq], o_2[hq:])
                if mode == "older":
                    acc_ref[rows, pair_cols[p]] += o_p
                else:
                    acc_ref[rows, pair_cols[p]] = o_p
        least = jnp.min(total, axis=0, keepdims=True)
        least = jnp.min(jnp.where(lane == 0, least, SURVIVAL_UNDERFLOW_LOG2))
        return (least >= SURVIVAL_UNDERFLOW_LOG2).astype(jnp.int32)

    done = lax.cond(q_start + hq >= win, lambda: visit(0, "inner"), lambda: visit(0, "edge"))

    def cond(carry):
        m, done = carry
        return jnp.logical_and(q_start + tq - m * win > 0, done == 0)

    def body(carry):
        m, _ = carry
        return m + 1, visit(m, "older")

    lax.while_loop(cond, body, (jnp.int32(1), done))
    o_ref[...] = acc_ref[...].astype(BF16)


def _attn(q_arr, q_col, k_arr, k_col, v_arr, v_col, *, batch, q_len, k_len, pos0, tq, hq):
    assert tq % hq == 0 and q_len % tq == 0 and pos0 % tq == 0 and ATTN_WINDOW % hq == 0
    assert pos0 + q_len <= k_len and k_len >= ATTN_WINDOW and hq % BF16_SUBLANES == 0
    nq = q_len // tq
    stacked_rows = (tq // hq) * SB_HEADS * hq
    return pl.pallas_call(
        functools.partial(_attn_kernel, tq=tq, hq=hq, pos0=pos0),
        grid=(batch, nq),
        in_specs=[
            pl.BlockSpec((tq, D_SB), lambda b, i: (b * nq + i, q_col)),
            pl.BlockSpec((k_len, D_SB), lambda b, i: (b, k_col)),
            pl.BlockSpec((k_len, D_SB), lambda b, i: (b, v_col)),
        ],
        out_specs=pl.BlockSpec((tq, D_SB), lambda b, i: (b * nq + i, 0)),
        out_shape=jax.ShapeDtypeStruct((batch * q_len, D_SB), BF16),
        scratch_shapes=[pltpu.VMEM((tq, D_SB), F32), pltpu.VMEM((stacked_rows, LANES), F32)],
        compiler_params=pltpu.CompilerParams(
            dimension_semantics=("parallel", "arbitrary"), vmem_limit_bytes=VMEM_LIMIT_BYTES),
        name="attn",
    )(q_arr, k_arr, v_arr)


def _merge_kernel(x_ref, u_ref, halo_ref, o_ref, gate_ref, pw_ref, ps_ref, wbp_ref, wbs_ref,
                  wout_ref, y_ref, *copy_refs, tm, pos0, tiles_per_seq):
    def weight(ref, position, *idx):
        w = ref[idx] if idx else ref[...]
        if copy_refs:
            w = w.astype(BF16)
            if idx:
                copy_refs[position][idx] = w
            else:
                copy_refs[position][...] = w
        return w

    w_sb = weight(wbs_ref, 2)
    streams = halo_ref.shape[0]
    rows = tm // streams
    if streams == 1:
        first_pos = pos0 + (pl.program_id(0) % tiles_per_seq) * tm
        pos = first_pos + lax.broadcasted_iota(jnp.int32, (tm, 1), 0)
    else:
        pos = jnp.concatenate([pos0 + lax.broadcasted_iota(jnp.int32, (rows, 1), 0)] * streams,
                              axis=0)

    n_groups = len(POOL_WINDOWS)
    n_chunks = n_groups if tm % (BF16_SUBLANES * n_groups) == 0 else 1
    quarter = tm // n_chunks
    span = POOL_HALO + rows
    run = jnp.concatenate(
        [part for b in range(streams)
         for part in (halo_ref[b], u_ref[b * rows:(b + 1) * rows, :])], axis=0)
    width, mixed, gated_sb = 1, [], []
    for gi, win in enumerate(POOL_WINDOWS):
        while width < win:
            run = run + pltpu.roll(run, width, 0)
            width *= 2
        assert width == win, "pool windows must be ascending powers of two"
        cols = slice(gi * POOL_GROUP, (gi + 1) * POOL_GROUP)
        cnt = jnp.minimum(win, pos + 1).astype(F32)
        tot = jnp.concatenate([run[b * span + POOL_HALO:(b + 1) * span, 0:POOL_GROUP]
                               for b in range(streams)], axis=0)
        d = (tot / cnt - u_ref[:, cols]).astype(BF16)
        mixed.append(jnp.dot(d, weight(pw_ref, 0, gi), preferred_element_type=F32)
                     * ps_ref[:, cols])
        if gi + 1 < n_groups:
            run = run[:, POOL_GROUP:]
        if gi < n_chunks:
            part = slice(gi * quarter, (gi + 1) * quarter)
            br_sb = jnp.dot(o_ref[part, :], w_sb, preferred_element_type=F32)
            gated_sb.append(
                _sigmoid(gate_ref[part, D_MODEL:2 * D_MODEL].astype(F32)) * br_sb)
    a = jnp.concatenate(mixed, axis=1).astype(BF16)

    br_pool = jnp.dot(a, weight(wbp_ref, 1), preferred_element_type=F32)
    merged = (_sigmoid(gate_ref[:, 0:D_MODEL].astype(F32)) * br_pool
              + jnp.concatenate(gated_sb, axis=0))
    y_ref[...] = x_ref[...] + jnp.dot(merged.astype(BF16), weight(wout_ref, 3),
                                      preferred_element_type=F32)


def _merge(x, u, halo, o, gate, pool_w, pool_scale, w_bp, w_bs, w_out, *, tm, pos0, seq):
    n = x.shape[0]
    const2 = lambda i: (0, 0)
    resident = pl.Buffered(1)
    streams = max(tm // seq, 1)
    assert tm % seq == 0 or seq % tm == 0
    weights = (pool_w, w_bp, w_bs, w_out)
    cast = pool_w.dtype == F32
    assert not cast or n == tm
    out_specs = [pl.BlockSpec((tm, D_MODEL), lambda i: (i, 0))]
    out_shape = [jax.ShapeDtypeStruct((n, D_MODEL), F32)]
    if cast:
        out_specs += [pl.BlockSpec(m.shape, lambda i, nd=m.ndim: (0,) * nd) for m in weights]
        out_shape += [jax.ShapeDtypeStruct(m.shape, BF16) for m in weights]
    return pl.pallas_call(
        functools.partial(_merge_kernel, tm=tm, pos0=pos0, tiles_per_seq=max(seq // tm, 1)),
        grid=(n // tm,),
        in_specs=[
            pl.BlockSpec((tm, D_MODEL), lambda i: (i, 0)),
            pl.BlockSpec((tm, D_POOL), lambda i: (i, 0)),
            pl.BlockSpec((streams, POOL_HALO, D_POOL), lambda i: (i, 0, 0)),
            pl.BlockSpec((tm, D_SB), lambda i: (i, 0)),
            pl.BlockSpec((tm, 2 * D_MODEL), lambda i: (i, 0)),
            pl.BlockSpec((len(POOL_WINDOWS), POOL_GROUP, POOL_GROUP), lambda i: (0, 0, 0)),
            pl.BlockSpec((1, D_POOL), const2),
            pl.BlockSpec((D_POOL, D_MODEL), const2, pipeline_mode=resident),
            pl.BlockSpec((D_SB, D_MODEL), const2, pipeline_mode=resident),
            pl.BlockSpec((D_MODEL, D_MODEL), const2, pipeline_mode=resident),
        ],
        out_specs=out_specs,
        out_shape=out_shape,
        compiler_params=pltpu.CompilerParams(
            dimension_semantics=("parallel",), vmem_limit_bytes=VMEM_LIMIT_BYTES),
        name="merge_cast" if cast else "merge",
    )(x, u, halo, o, gate, pool_w, pool_scale, w_bp, w_bs, w_out)


def _layer(x, w, *, batch, seq, pos0, k_past, v_past, pool_prefix, tm_ffn, tm_mix, tm_merge, tq,
           hq):
    assert seq >= POOL_STATE
    w_bf16 = {}

    def ffn(x_in, prefix, final_g=None):
        names = [prefix + "_gate", prefix + "_up", prefix + "_down"]
        res = _ffn(x_in, w[prefix + "_norm"], *[w[nm] for nm in names], final_g,
                   tm=tm_ffn, tf=FFN_CHUNK)
        if isinstance(res, (list, tuple)):
            w_bf16.update(zip(names, res[1:]))
            return res[0]
        return res

    x1 = ffn(x, "ffn1")
    u, qkv, k_hm, v_hm, gate, *w_in_copy = _proj(x1, w["mix_norm"], w["w_in"], batch=batch,
                                                 seq=seq, tm=tm_mix)
    w_bf16.update(zip(["w_in"], w_in_copy))

    if k_past is None:
        o = _attn(qkv, 0, qkv, 1, qkv, 2, batch=batch, q_len=seq, k_len=seq, pos0=0, tq=tq, hq=hq)
    else:
        def with_past(past, col):
            new = qkv[:, col * D_SB:(col + 1) * D_SB].reshape(batch, seq, D_SB)
            return jnp.concatenate([past, new], axis=1).reshape(batch * (pos0 + seq), D_SB)
        o = _attn(qkv, 0, with_past(k_past, 1), 0, with_past(v_past, 2), 0,
                  batch=batch, q_len=seq, k_len=pos0 + seq, pos0=pos0, tq=tq, hq=hq)

    piece = min(tm_merge, seq)
    pieces = seq // piece
    u4 = u.reshape(batch, pieces, piece, D_POOL)
    first = jnp.pad(pool_prefix, ((0, 0), (1, 0), (0, 0)))[:, None]
    halo = jnp.concatenate([first, u4[:, :-1, piece - POOL_HALO:, :]], axis=1)
    halo = halo.reshape(batch * pieces, POOL_HALO, D_POOL)

    merge_names = ["pool_w", "w_branch_pool", "w_branch_sb", "w_out"]
    x2, *copies = _merge(x1, u, halo, o, gate, w["pool_w"], w["pool_scale"], w["w_branch_pool"],
                         w["w_branch_sb"], w["w_out"], tm=tm_merge, pos0=pos0, seq=seq)
    w_bf16.update(zip(merge_names, copies))
    y = ffn(x2, "ffn2", w["final_norm"])
    new_pool = u.reshape(batch, seq, D_POOL)[:, seq - POOL_STATE:, :]
    return (y.reshape(batch, seq, D_MODEL), k_hm, v_hm, new_pool), w_bf16


def kernel(x_prompt, x_sample, cache_k, cache_v, state_pool, ffn1_norm, ffn1_gate, ffn1_up, ffn1_down, mix_norm, w_in, pool_w, pool_scale, w_branch_pool, w_branch_sb, w_out, ffn2_norm, ffn2_gate, ffn2_up, ffn2_down, final_norm):
    assert ffn1_norm.shape[0] == 1, "single-layer kernel"
    w = dict(w_in=w_in[0], pool_w=pool_w[0], w_branch_pool=w_branch_pool[0],
             w_branch_sb=w_branch_sb[0], w_out=w_out[0],
             ffn1_gate=ffn1_gate[0], ffn1_up=ffn1_up[0], ffn1_down=ffn1_down[0],
             ffn2_gate=ffn2_gate[0], ffn2_up=ffn2_up[0], ffn2_down=ffn2_down[0])
    w.update(ffn1_norm=ffn1_norm, mix_norm=mix_norm, pool_scale=pool_scale, ffn2_norm=ffn2_norm,
             final_norm=final_norm.reshape(1, D_MODEL))

    b_p, s_p, _ = x_prompt.shape
    b_d, s_d, _ = x_sample.shape
    past = cache_k.shape[3]

    def token_major(cache):
        return cache[0].transpose(0, 2, 1, 3).reshape(b_d, past, D_SB).astype(BF16)

    (y_d, k_d, v_d, pool_d), w_bf16 = _layer(
        x_sample.reshape(b_d * s_d, D_MODEL), w, batch=b_d, seq=s_d, pos0=past,
        k_past=token_major(cache_k), v_past=token_major(cache_v), pool_prefix=state_pool[0],
        tm_ffn=b_d * s_d, tm_mix=b_d * s_d, tm_merge=b_d * s_d, tq=s_d, hq=s_d)

    (y_p, k_p, v_p, pool_p), _ = _layer(
        x_prompt.reshape(b_p * s_p, D_MODEL), {**w, **w_bf16}, batch=b_p, seq=s_p, pos0=0,
        k_past=None, v_past=None, pool_prefix=jnp.zeros((b_p, POOL_STATE, D_POOL), F32),
        tm_ffn=1024, tm_mix=1024, tm_merge=1024, tq=512, hq=64)

    return (y_p, y_d, k_p[None], v_p[None], pool_p[None], k_d[None], v_d[None], pool_d[None])
```

```python
import functools

import jax
import jax.numpy as jnp
from jax import lax
from jax.experimental import pallas as pl
from jax.experimental.pallas import tpu as pltpu

F32 = jnp.float32
BF16 = jnp.bfloat16

D_MODEL = 1024
D_FF = 4 * D_MODEL
D_POOL = D_MODEL // 2
POOL_WINDOWS = (2, 4, 8, 16)
POOL_GROUP = D_POOL // len(POOL_WINDOWS)
POOL_STATE = max(POOL_WINDOWS) - 1
POOL_HALO = POOL_STATE + 1
SB_HEADS = 8
SB_HEAD_DIM = 64
D_SB = SB_HEADS * SB_HEAD_DIM
D_IN = D_POOL + 3 * D_SB + 2 * D_MODEL
RMS_EPS = 1e-6
LANES = 128
BF16_SUBLANES = 16
VMEM_LIMIT_BYTES = 56 * 1024 * 1024
FFN_CHUNK = 1024
ATTN_WINDOW = 2 * LANES

LOG2E = 1.4426950408889634
SURVIVAL_UNDERFLOW_LOG2 = 150.0 * (1.0 + 2.0 ** -8)
MASKED_LOGIT = -1e30


def _rmsnorm(x, g):
    ms = jnp.mean(x * x, axis=-1, keepdims=True)
    return (x * lax.rsqrt(ms + RMS_EPS)) * g


def _sigmoid(x):
    return 0.5 * jnp.tanh(0.5 * x) + 0.5


def _ffn_kernel(*refs, n_ff, final, cast):
    refs = list(refs)
    x_ref, xnext_ref, g_ref, wg_ref, wu_ref, wd_ref = refs[:6]
    del refs[:6]
    fg_ref = refs.pop(0) if final else None
    o_ref = refs.pop(0)
    if cast:
        wg_out, wu_out, wd_out = refs[:3]
        del refs[:3]
    xn_ref, acc_ref = refs
    i = pl.program_id(0)
    j = pl.program_id(1)
    slot = i % 2

    @pl.when(jnp.logical_and(i == 0, j == 0))
    def _():
        xn_ref[0] = _rmsnorm(x_ref[...], g_ref[...]).astype(BF16)

    def chunk():
        wg, wu, wd = wg_ref[...], wu_ref[...], wd_ref[...]
        if cast:
            wg, wu, wd = wg.astype(BF16), wu.astype(BF16), wd.astype(BF16)
            wg_out[...] = wg
            wu_out[...] = wu
            wd_out[...] = wd
        xn = xn_ref[slot]
        gate = jnp.dot(xn, wg, preferred_element_type=F32)
        up = jnp.dot(xn, wu, preferred_element_type=F32)
        h = (gate * _sigmoid(gate) * up).astype(BF16)
        return jnp.dot(h, wd, preferred_element_type=F32)

    def prepare_next(rows):
        xn_ref[1 - slot, rows, :] = _rmsnorm(xnext_ref[rows, :], g_ref[...]).astype(BF16)

    piece = x_ref.shape[0] // n_ff

    def first():
        acc_ref[...] = chunk()
        prepare_next(pl.ds(0, piece))

    def middle():
        acc_ref[...] += chunk()
        prepare_next(pl.ds(pl.multiple_of(j * piece, piece), piece))

    def last():
        y = x_ref[...] + 0.5 * (acc_ref[...] + chunk())
        if final:
            y = _rmsnorm(y, fg_ref[...])
        o_ref[...] = y
        prepare_next(pl.ds((n_ff - 1) * piece, piece))

    case = jnp.where(j == 0, 0, jnp.where(j == n_ff - 1, 2, 1))
    lax.switch(case, [first, middle, last])


def _ffn(x, norm_g, wg, wu, wd, final_g=None, *, tm, tf):
    n = x.shape[0]
    n_rows = n // tm
    n_ff = D_FF // tf
    assert n_ff >= 2 and tm % (BF16_SUBLANES * n_ff) == 0
    final = final_g is not None
    cast = wg.dtype == F32
    assert not cast or n_rows == 1
    row = pl.BlockSpec((1, D_MODEL), lambda i, j: (0, 0))
    w_specs = [
        pl.BlockSpec((D_MODEL, tf), lambda i, j: (0, j)),
        pl.BlockSpec((D_MODEL, tf), lambda i, j: (0, j)),
        pl.BlockSpec((tf, D_MODEL), lambda i, j: (j, 0)),
    ]
    in_specs = [
        pl.BlockSpec((tm, D_MODEL), lambda i, j: (i, 0)),
        pl.BlockSpec((tm, D_MODEL), lambda i, j: (jnp.minimum(i + 1, n_rows - 1), 0)),
        row,
    ] + w_specs
    args = [x, x, norm_g, wg, wu, wd]
    if final:
        in_specs.append(row)
        args.append(final_g)
    out_specs = [pl.BlockSpec((tm, D_MODEL), lambda i, j: (i, 0))]
    out_shape = [jax.ShapeDtypeStruct((n, D_MODEL), F32)]
    if cast:
        out_specs += w_specs
        out_shape += [jax.ShapeDtypeStruct(m.shape, BF16) for m in (wg, wu, wd)]
    outs = pl.pallas_call(
        functools.partial(_ffn_kernel, n_ff=n_ff, final=final, cast=cast),
        grid=(n_rows, n_ff),
        in_specs=in_specs,
        out_specs=out_specs,
        out_shape=out_shape,
        scratch_shapes=[pltpu.VMEM((2, tm, D_MODEL), BF16), pltpu.VMEM((tm, D_MODEL), F32)],
        compiler_params=pltpu.CompilerParams(
            dimension_semantics=("arbitrary", "arbitrary"), vmem_limit_bytes=VMEM_LIMIT_BYTES),
        name=("ffn_final" if final else "ffn") + ("_cast" if cast else ""),
    )(*args)
    return outs if cast else outs[0]


def _ffn_resident_kernel(*refs, tf, final):
    x_ref, xnext_ref, g_ref, wg_ref, wu_ref, wd_ref = refs[:6]
    fg_ref = refs[6] if final else None
    o_ref, xn_ref = refs[-2:]
    i = pl.program_id(0)
    slot = i % 2

    @pl.when(i == 0)
    def _():
        xn_ref[0] = _rmsnorm(x_ref[...], g_ref[...]).astype(BF16)

    xn = xn_ref[slot]
    acc = None
    for c in range(D_FF // tf):
        cols = slice(c * tf, (c + 1) * tf)
        gate = jnp.dot(xn, wg_ref[:, cols], preferred_element_type=F32)
        up = jnp.dot(xn, wu_ref[:, cols], preferred_element_type=F32)
        h = (gate * _sigmoid(gate) * up).astype(BF16)
        part = jnp.dot(h, wd_ref[cols, :], preferred_element_type=F32)
        acc = part if acc is None else acc + part
    y = x_ref[...] + 0.5 * acc
    if final:
        y = _rmsnorm(y, fg_ref[...])
    o_ref[...] = y
    xn_ref[1 - slot] = _rmsnorm(xnext_ref[...], g_ref[...]).astype(BF16)


def _ffn_resident(x, norm_g, wg, wu, wd, final_g=None, *, tm, tf):
    n = x.shape[0]
    n_rows = n // tm
    final = final_g is not None
    row = pl.BlockSpec((1, D_MODEL), lambda i: (0, 0))
    resident = pl.Buffered(1)
    in_specs = [
        pl.BlockSpec((tm, D_MODEL), lambda i: (i, 0)),
        pl.BlockSpec((tm, D_MODEL), lambda i: (jnp.minimum(i + 1, n_rows - 1), 0)),
        row,
        pl.BlockSpec((D_MODEL, D_FF), lambda i: (0, 0), pipeline_mode=resident),
        pl.BlockSpec((D_MODEL, D_FF), lambda i: (0, 0), pipeline_mode=resident),
        pl.BlockSpec((D_FF, D_MODEL), lambda i: (0, 0), pipeline_mode=resident),
    ]
    args = [x, x, norm_g, wg, wu, wd]
    if final:
        in_specs.append(row)
        args.append(final_g)
    return pl.pallas_call(
        functools.partial(_ffn_resident_kernel, tf=tf, final=final),
        grid=(n_rows,),
        in_specs=in_specs,
        out_specs=pl.BlockSpec((tm, D_MODEL), lambda i: (i, 0)),
        out_shape=jax.ShapeDtypeStruct((n, D_MODEL), F32),
        scratch_shapes=[pltpu.VMEM((2, tm, D_MODEL), BF16)],
        compiler_params=pltpu.CompilerParams(
            dimension_semantics=("arbitrary",), vmem_limit_bytes=VMEM_LIMIT_BYTES),
        name="ffn_final_resident" if final else "ffn_resident",
    )(*args)


def _proj_kernel(x_ref, g_ref, w_ref, u_ref, qkv_ref, k_ref, v_ref, gate_ref, w_out_ref=None):
    h = _rmsnorm(x_ref[...], g_ref[...]).astype(BF16)

    def mm(c0, c1):
        w = w_ref[:, c0:c1]
        if w_out_ref is not None:
            w = w.astype(BF16)
            w_out_ref[:, c0:c1] = w
        return jnp.dot(h, w, preferred_element_type=F32)

    c = 0
    u_ref[...] = mm(c, c + D_POOL)
    c += D_POOL
    qkv_ref[:, 0:D_SB] = (mm(c, c + D_SB) * (LOG2E * SB_HEAD_DIM ** -0.5)).astype(BF16)
    c += D_SB
    for idx, hm_ref in ((1, k_ref), (2, v_ref)):
        t = mm(c, c + D_SB)
        c += D_SB
        qkv_ref[:, idx * D_SB:(idx + 1) * D_SB] = t.astype(BF16)
        streams, _, rows, _ = hm_ref.shape
        for b in range(streams):
            for hd in range(SB_HEADS):
                hm_ref[b, hd, :, :] = t[b * rows:(b + 1) * rows,
                                        hd * SB_HEAD_DIM:(hd + 1) * SB_HEAD_DIM]
    gate_ref[:, 0:D_MODEL] = mm(c, c + D_MODEL).astype(BF16)
    c += D_MODEL
    gate_ref[:, D_MODEL:2 * D_MODEL] = mm(c, c + D_MODEL).astype(BF16)


def _proj(x, norm_g, w_in, *, batch, seq, tm):
    n = x.shape[0]
    cast = w_in.dtype == F32
    assert not cast or n == tm
    if tm <= seq:
        tiles_per_seq = seq // tm
        hm_spec = pl.BlockSpec((1, SB_HEADS, tm, SB_HEAD_DIM),
                               lambda i: (i // tiles_per_seq, 0, i % tiles_per_seq, 0))
    else:
        assert tm % seq == 0
        hm_spec = pl.BlockSpec((tm // seq, SB_HEADS, seq, SB_HEAD_DIM), lambda i: (i, 0, 0, 0))
    hm_shape = jax.ShapeDtypeStruct((batch, SB_HEADS, seq, SB_HEAD_DIM), F32)
    w_spec = pl.BlockSpec((D_MODEL, D_IN), lambda i: (0, 0), pipeline_mode=pl.Buffered(1))
    out_specs = [
        pl.BlockSpec((tm, D_POOL), lambda i: (i, 0)),
        pl.BlockSpec((tm, 3 * D_SB), lambda i: (i, 0)),
        hm_spec,
        hm_spec,
        pl.BlockSpec((tm, 2 * D_MODEL), lambda i: (i, 0)),
    ]
    out_shape = [
        jax.ShapeDtypeStruct((n, D_POOL), F32),
        jax.ShapeDtypeStruct((n, 3 * D_SB), BF16),
        hm_shape,
        hm_shape,
        jax.ShapeDtypeStruct((n, 2 * D_MODEL), BF16),
    ]
    if cast:
        out_specs.append(pl.BlockSpec((D_MODEL, D_IN), lambda i: (0, 0)))
        out_shape.append(jax.ShapeDtypeStruct((D_MODEL, D_IN), BF16))
    return pl.pallas_call(
        _proj_kernel,
        grid=(n // tm,),
        in_specs=[
            pl.BlockSpec((tm, D_MODEL), lambda i: (i, 0)),
            pl.BlockSpec((1, D_MODEL), lambda i: (0, 0)),
            w_spec,
        ],
        out_specs=out_specs,
        out_shape=out_shape,
        compiler_params=pltpu.CompilerParams(
            dimension_semantics=("parallel",), vmem_limit_bytes=VMEM_LIMIT_BYTES),
        name="proj_cast" if cast else "proj",
    )(x, norm_g, w_in)


def _attn_kernel(q_ref, k_ref, v_ref, o_ref, acc_ref, car_ref, *, tq, hq, pos0):
    win = ATTN_WINDOW
    q_start = pos0 + pl.program_id(1) * tq
    subs = range(tq // hq)
    pairs = range(SB_HEADS // 2)
    pair_cols = [slice(p * LANES, (p + 1) * LANES) for p in pairs]

    lane = lax.broadcasted_iota(jnp.int32, (1, LANES), 1)
    first_head = lane < SB_HEAD_DIM
    r = lax.broadcasted_iota(jnp.int32, (win, win), 0)
    c = lax.broadcasted_iota(jnp.int32, (win, win), 1)
    newer = jnp.where(r > c, 1.0, 0.0).astype(BF16)

    def softplus2(z):
        return jnp.maximum(z, 0.0) + jnp.log2(1.0 + jnp.exp2(-jnp.abs(z)))

    def visit(m, mode):
        k_ws, v_ws, valids = [], [], []
        for j in subs:
            row0 = q_start + j * hq
            win_end = row0 + hq - m * win
            q_pos = row0 + lax.broadcasted_iota(jnp.int32, (hq, 1), 0)
            if mode == "inner":
                start = pl.multiple_of(win_end - win, hq)
                k_pos = (win_end - LANES) + lax.broadcasted_iota(jnp.int32, (1, LANES), 1)
                valids.append(k_pos < q_pos)
            else:
                start = pl.multiple_of(jnp.maximum(win_end - win, 0), hq)
                k_pos = start + lax.broadcasted_iota(jnp.int32, (1, win), 1)
                valids.append(k_pos < (q_pos if mode == "edge" else win_end))
            k_ws.append(k_ref[pl.ds(start, win), :])
            v_ws.append(v_ref[pl.ds(start, win), :])

        def mask(x, j):
            if mode == "inner":
                return jnp.concatenate(
                    [x[:, :win - LANES],
                     jnp.where(valids[j], x[:, win - LANES:], MASKED_LOGIT)], axis=1)
            return jnp.where(valids[j], x, MASKED_LOGIT)

        z_parts = []
        for j in subs:
            rows = slice(j * hq, (j + 1) * hq)
            for p in pairs:
                q_p = q_ref[rows, pair_cols[p]]
                zero = jnp.zeros_like(q_p)
                q_2 = jnp.concatenate([jnp.where(first_head, q_p, zero),
                                       jnp.where(first_head, zero, q_p)], axis=0)
                z_2 = lax.dot_general(q_2, k_ws[j][:, pair_cols[p]], (((1,), (1,)), ((), ())),
                                      preferred_element_type=F32)
                z_parts += [mask(z_2[:hq], j), mask(z_2[hq:], j)]
        z = jnp.concatenate(z_parts, axis=0)
        s = softplus2(z)
        sums = jnp.dot(s.astype(BF16), newer, preferred_element_type=F32)
        arg = (z - s) - sums
        total = sums[:, 0:LANES] + s[:, 0:LANES]
        if mode == "older":
            car = car_ref[...]
            car_row = jnp.broadcast_to(car[:, 0:1], car.shape)
            arg = arg - jnp.concatenate([car_row] * (win // LANES), axis=1)
            total = car + total
        car_ref[...] = total
        w = jnp.exp2(arg).astype(BF16)

        i = 0
        for j in subs:
            rows = slice(j * hq, (j + 1) * hq)
            for p in pairs:
                o_2 = jnp.dot(w[i * hq:(i + 2) * hq], v_ws[j][:, pair_cols[p]],
                              preferred_element_type=F32)
                i += 2
                o_p = jnp.where(first_head, o_2[:hq], o_2[hq:])
                if mode == "older":
                    acc_ref[rows, pair_cols[p]] += o_p
                else:
                    acc_ref[rows, pair_cols[p]] = o_p
        least = jnp.min(total, axis=0, keepdims=True)
        least = jnp.min(jnp.where(lane == 0, least, SURVIVAL_UNDERFLOW_LOG2))
        return (least >= SURVIVAL_UNDERFLOW_LOG2).astype(jnp.int32)

    done = lax.cond(q_start + hq >= win, lambda: visit(0, "inner"), lambda: visit(0, "edge"))

    def cond(carry):
        m, done = carry
        return jnp.logical_and(q_start + tq - m * win > 0, done == 0)

    def body(carry):
        m, _ = carry
        return m + 1, visit(m, "older")

    lax.while_loop(cond, body, (jnp.int32(1), done))
    o_ref[...] = acc_ref[...].astype(BF16)


def _attn(q_arr, q_col, k_arr, k_col, v_arr, v_col, *, batch, q_len, k_len, pos0, tq, hq):
    assert tq % hq == 0 and q_len % tq == 0 and pos0 % tq == 0 and ATTN_WINDOW % hq == 0
    assert pos0 + q_len <= k_len and k_len >= ATTN_WINDOW and hq % BF16_SUBLANES == 0
    nq = q_len // tq
    stacked_rows = (tq // hq) * SB_HEADS * hq
    return pl.pallas_call(
        functools.partial(_attn_kernel, tq=tq, hq=hq, pos0=pos0),
        grid=(batch, nq),
        in_specs=[
            pl.BlockSpec((tq, D_SB), lambda b, i: (b * nq + i, q_col)),
            pl.BlockSpec((k_len, D_SB), lambda b, i: (b, k_col)),
            pl.BlockSpec((k_len, D_SB), lambda b, i: (b, v_col)),
        ],
        out_specs=pl.BlockSpec((tq, D_SB), lambda b, i: (b * nq + i, 0)),
        out_shape=jax.ShapeDtypeStruct((batch * q_len, D_SB), BF16),
        scratch_shapes=[pltpu.VMEM((tq, D_SB), F32), pltpu.VMEM((stacked_rows, LANES), F32)],
        compiler_params=pltpu.CompilerParams(
            dimension_semantics=("parallel", "arbitrary"), vmem_limit_bytes=VMEM_LIMIT_BYTES),
        name="attn",
    )(q_arr, k_arr, v_arr)


def _merge_kernel(x_ref, u_ref, halo_ref, o_ref, gate_ref, pw_ref, ps_ref, wbp_ref, wbs_ref,
                  wout_ref, y_ref, *copy_refs, tm, pos0, tiles_per_seq):
    def weight(ref, position, *idx):
        w = ref[idx] if idx else ref[...]
        if copy_refs:
            w = w.astype(BF16)
            if idx:
                copy_refs[position][idx] = w
            else:
                copy_refs[position][...] = w
        return w

    w_sb = weight(wbs_ref, 2)
    streams = halo_ref.shape[0]
    rows = tm // streams
    if streams == 1:
        first_pos = pos0 + (pl.program_id(0) % tiles_per_seq) * tm
        pos = first_pos + lax.broadcasted_iota(jnp.int32, (tm, 1), 0)
    else:
        pos = jnp.concatenate([pos0 + lax.broadcasted_iota(jnp.int32, (rows, 1), 0)] * streams,
                              axis=0)

    n_groups = len(POOL_WINDOWS)
    n_chunks = n_groups if tm % (BF16_SUBLANES * n_groups) == 0 else 1
    quarter = tm // n_chunks
    span = POOL_HALO + rows
    run = jnp.concatenate(
        [part for b in range(streams)
         for part in (halo_ref[b], u_ref[b * rows:(b + 1) * rows, :])], axis=0)
    width, mixed, gated_sb = 1, [], []
    for gi, win in enumerate(POOL_WINDOWS):
        while width < win:
            run = run + pltpu.roll(run, width, 0)
            width *= 2
        assert width == win, "pool windows must be ascending powers of two"
        cols = slice(gi * POOL_GROUP, (gi + 1) * POOL_GROUP)
        cnt = jnp.minimum(win, pos + 1).astype(F32)
        tot = jnp.concatenate([run[b * span + POOL_HALO:(b + 1) * span, 0:POOL_GROUP]
                               for b in range(streams)], axis=0)
        d = (tot / cnt - u_ref[:, cols]).astype(BF16)
        mixed.append(jnp.dot(d, weight(pw_ref, 0, gi), preferred_element_type=F32)
                     * ps_ref[:, cols])
        if gi + 1 < n_groups:
            run = run[:, POOL_GROUP:]
        if gi < n_chunks:
            part = slice(gi * quarter, (gi + 1) * quarter)
            br_sb = jnp.dot(o_ref[part, :], w_sb, preferred_element_type=F32)
            gated_sb.append(
                _sigmoid(gate_ref[part, D_MODEL:2 * D_MODEL].astype(F32)) * br_sb)
    a = jnp.concatenate(mixed, axis=1).astype(BF16)

    br_pool = jnp.dot(a, weight(wbp_ref, 1), preferred_element_type=F32)
    merged = (_sigmoid(gate_ref[:, 0:D_MODEL].astype(F32)) * br_pool
              + jnp.concatenate(gated_sb, axis=0))
    y_ref[...] = x_ref[...] + jnp.dot(merged.astype(BF16), weight(wout_ref, 3),
                                      preferred_element_type=F32)


def _merge(x, u, halo, o, gate, pool_w, pool_scale, w_bp, w_bs, w_out, *, tm, pos0, seq):
    n = x.shape[0]
    const2 = lambda i: (0, 0)
    resident = pl.Buffered(1)
    streams = max(tm // seq, 1)
    assert tm % seq == 0 or seq % tm == 0
    weights = (pool_w, w_bp, w_bs, w_out)
    cast = pool_w.dtype == F32
    assert not cast or n == tm
    out_specs = [pl.BlockSpec((tm, D_MODEL), lambda i: (i, 0))]
    out_shape = [jax.ShapeDtypeStruct((n, D_MODEL), F32)]
    if cast:
        out_specs += [pl.BlockSpec(m.shape, lambda i, nd=m.ndim: (0,) * nd) for m in weights]
        out_shape += [jax.ShapeDtypeStruct(m.shape, BF16) for m in weights]
    return pl.pallas_call(
        functools.partial(_merge_kernel, tm=tm, pos0=pos0, tiles_per_seq=max(seq // tm, 1)),
        grid=(n // tm,),
        in_specs=[
            pl.BlockSpec((tm, D_MODEL), lambda i: (i, 0)),
            pl.BlockSpec((tm, D_POOL), lambda i: (i, 0)),
            pl.BlockSpec((streams, POOL_HALO, D_POOL), lambda i: (i, 0, 0)),
            pl.BlockSpec((tm, D_SB), lambda i: (i, 0)),
            pl.BlockSpec((tm, 2 * D_MODEL), lambda i: (i, 0)),
            pl.BlockSpec((len(POOL_WINDOWS), POOL_GROUP, POOL_GROUP), lambda i: (0, 0, 0)),
            pl.BlockSpec((1, D_POOL), const2),
            pl.BlockSpec((D_POOL, D_MODEL), const2, pipeline_mode=resident),
            pl.BlockSpec((D_SB, D_MODEL), const2, pipeline_mode=resident),
            pl.BlockSpec((D_MODEL, D_MODEL), const2, pipeline_mode=resident),
        ],
        out_specs=out_specs,
        out_shape=out_shape,
        compiler_params=pltpu.CompilerParams(
            dimension_semantics=("parallel",), vmem_limit_bytes=VMEM_LIMIT_BYTES),
        name="merge_cast" if cast else "merge",
    )(x, u, halo, o, gate, pool_w, pool_scale, w_bp, w_bs, w_out)


def _layer(x, w, *, batch, seq, pos0, k_past, v_past, pool_prefix, tm_ffn, tm_mix, tm_merge, tq,
           hq):
    assert seq >= POOL_STATE
    w_bf16 = {}

    def ffn(x_in, prefix, final_g=None):
        names = [prefix + "_gate", prefix + "_up", prefix + "_down"]
        run = _ffn_resident if w[names[0]].dtype == BF16 else _ffn
        res = run(x_in, w[prefix + "_norm"], *[w[nm] for nm in names], final_g,
                  tm=tm_ffn, tf=FFN_CHUNK)
        if isinstance(res, (list, tuple)):
            w_bf16.update(zip(names, res[1:]))
            return res[0]
        return res

    x1 = ffn(x, "ffn1")
    u, qkv, k_hm, v_hm, gate, *w_in_copy = _proj(x1, w["mix_norm"], w["w_in"], batch=batch,
                                                 seq=seq, tm=tm_mix)
    w_bf16.update(zip(["w_in"], w_in_copy))

    if k_past is None:
        o = _attn(qkv, 0, qkv, 1, qkv, 2, batch=batch, q_len=seq, k_len=seq, pos0=0, tq=tq, hq=hq)
    else:
        def with_past(past, col):
            new = qkv[:, col * D_SB:(col + 1) * D_SB].reshape(batch, seq, D_SB)
            return jnp.concatenate([past, new], axis=1).reshape(batch * (pos0 + seq), D_SB)
        o = _attn(qkv, 0, with_past(k_past, 1), 0, with_past(v_past, 2), 0,
                  batch=batch, q_len=seq, k_len=pos0 + seq, pos0=pos0, tq=tq, hq=hq)

    piece = min(tm_merge, seq)
    pieces = seq // piece
    u4 = u.reshape(batch, pieces, piece, D_POOL)
    first = jnp.pad(pool_prefix, ((0, 0), (1, 0), (0, 0)))[:, None]
    halo = jnp.concatenate([first, u4[:, :-1, piece - POOL_HALO:, :]], axis=1)
    halo = halo.reshape(batch * pieces, POOL_HALO, D_POOL)

    merge_names = ["pool_w", "w_branch_pool", "w_branch_sb", "w_out"]
    x2, *copies = _merge(x1, u, halo, o, gate, w["pool_w"], w["pool_scale"], w["w_branch_pool"],
                         w["w_branch_sb"], w["w_out"], tm=tm_merge, pos0=pos0, seq=seq)
    w_bf16.update(zip(merge_names, copies))
    y = ffn(x2, "ffn2", w["final_norm"])
    new_pool = u.reshape(batch, seq, D_POOL)[:, seq - POOL_STATE:, :]
    return (y.reshape(batch, seq, D_MODEL), k_hm, v_hm, new_pool), w_bf16


def kernel(x_prompt, x_sample, cache_k, cache_v, state_pool, ffn1_norm, ffn1_gate, ffn1_up, ffn1_down, mix_norm, w_in, pool_w, pool_scale, w_branch_pool, w_branch_sb, w_out, ffn2_norm, ffn2_gate, ffn2_up, ffn2_down, final_norm):
    assert ffn1_norm.shape[0] == 1, "single-layer kernel"
    w = dict(w_in=w_in[0], pool_w=pool_w[0], w_branch_pool=w_branch_pool[0],
             w_branch_sb=w_branch_sb[0], w_out=w_out[0],
             ffn1_gate=ffn1_gate[0], ffn1_up=ffn1_up[0], ffn1_down=ffn1_down[0],
             ffn2_gate=ffn2_gate[0], ffn2_up=ffn2_up[0], ffn2_down=ffn2_down[0])
    w.update(ffn1_norm=ffn1_norm, mix_norm=mix_norm, pool_scale=pool_scale, ffn2_norm=ffn2_norm,
             final_norm=final_norm.reshape(1, D_MODEL))

    b_p, s_p, _ = x_prompt.shape
    b_d, s_d, _ = x_sample.shape
    past = cache_k.shape[3]

    def token_major(cache):
        return cache[0].transpose(0, 2, 1, 3).reshape(b_d, past, D_SB).astype(BF16)

    (y_d, k_d, v_d, pool_d), w_bf16 = _layer(
        x_sample.reshape(b_d * s_d, D_MODEL), w, batch=b_d, seq=s_d, pos0=past,
        k_past=token_major(cache_k), v_past=token_major(cache_v), pool_prefix=state_pool[0],
        tm_ffn=b_d * s_d, tm_mix=b_d * s_d, tm_merge=b_d * s_d, tq=s_d, hq=s_d)

    (y_p, k_p, v_p, pool_p), _ = _layer(
        x_prompt.reshape(b_p * s_p, D_MODEL), {**w, **w_bf16}, batch=b_p, seq=s_p, pos0=0,
        k_past=None, v_past=None, pool_prefix=jnp.zeros((b_p, POOL_STATE, D_POOL), F32),
        tm_ffn=512, tm_mix=1024, tm_merge=1024, tq=512, hq=64)

    return (y_p, y_d, k_p[None], v_p[None], pool_p[None], k_d[None], v_d[None], pool_d[None])
```

```python
import functools

import jax
import jax.numpy as jnp
from jax import lax
from jax.experimental import pallas as pl
from jax.experimental.pallas import tpu as pltpu

F32 = jnp.float32
BF16 = jnp.bfloat16

D_MODEL = 1024
D_FF = 4 * D_MODEL
D_POOL = D_MODEL // 2
POOL_WINDOWS = (2, 4, 8, 16)
POOL_GROUP = D_POOL // len(POOL_WINDOWS)
POOL_STATE = max(POOL_WINDOWS) - 1
POOL_HALO = POOL_STATE + 1
SB_HEADS = 8
SB_HEAD_DIM = 64
D_SB = SB_HEADS * SB_HEAD_DIM
D_IN = D_POOL + 3 * D_SB + 2 * D_MODEL
RMS_EPS = 1e-6
LANES = 128
BF16_SUBLANES = 16
VMEM_LIMIT_BYTES = 56 * 1024 * 1024
FFN_CHUNK = 1024
ATTN_WINDOW = 2 * LANES

LOG2E = 1.4426950408889634
SURVIVAL_UNDERFLOW_LOG2 = 150.0 * (1.0 + 2.0 ** -8)
MASKED_LOGIT = -1e30


def _rmsnorm(x, g):
    ms = jnp.mean(x * x, axis=-1, keepdims=True)
    return (x * lax.rsqrt(ms + RMS_EPS)) * g


def _sigmoid(x):
    return 0.5 * jnp.tanh(0.5 * x) + 0.5


def _ffn_kernel(*refs, n_ff, final, cast):
    refs = list(refs)
    x_ref, xnext_ref, g_ref, wg_ref, wu_ref, wd_ref = refs[:6]
    del refs[:6]
    fg_ref = refs.pop(0) if final else None
    o_ref = refs.pop(0)
    if cast:
        wg_out, wu_out, wd_out = refs[:3]
        del refs[:3]
    xn_ref, acc_ref = refs
    i = pl.program_id(0)
    j = pl.program_id(1)
    slot = i % 2

    @pl.when(jnp.logical_and(i == 0, j == 0))
    def _():
        xn_ref[0] = _rmsnorm(x_ref[...], g_ref[...]).astype(BF16)

    def chunk():
        wg, wu, wd = wg_ref[...], wu_ref[...], wd_ref[...]
        if cast:
            wg, wu, wd = wg.astype(BF16), wu.astype(BF16), wd.astype(BF16)
            wg_out[...] = wg
            wu_out[...] = wu
            wd_out[...] = wd
        xn = xn_ref[slot]
        gate = jnp.dot(xn, wg, preferred_element_type=F32)
        up = jnp.dot(xn, wu, preferred_element_type=F32)
        h = (gate * _sigmoid(gate) * up).astype(BF16)
        return jnp.dot(h, wd, preferred_element_type=F32)

    def prepare_next(rows):
        xn_ref[1 - slot, rows, :] = _rmsnorm(xnext_ref[rows, :], g_ref[...]).astype(BF16)

    piece = x_ref.shape[0] // n_ff

    def first():
        acc_ref[...] = chunk()
        prepare_next(pl.ds(0, piece))

    def middle():
        acc_ref[...] += chunk()
        prepare_next(pl.ds(pl.multiple_of(j * piece, piece), piece))

    def last():
        y = x_ref[...] + 0.5 * (acc_ref[...] + chunk())
        if final:
            y = _rmsnorm(y, fg_ref[...])
        o_ref[...] = y
        prepare_next(pl.ds((n_ff - 1) * piece, piece))

    case = jnp.where(j == 0, 0, jnp.where(j == n_ff - 1, 2, 1))
    lax.switch(case, [first, middle, last])


def _ffn(x, norm_g, wg, wu, wd, final_g=None, *, tm, tf):
    n = x.shape[0]
    n_rows = n // tm
    n_ff = D_FF // tf
    assert n_ff >= 2 and tm % (BF16_SUBLANES * n_ff) == 0
    final = final_g is not None
    cast = wg.dtype == F32
    assert not cast or n_rows == 1
    row = pl.BlockSpec((1, D_MODEL), lambda i, j: (0, 0))
    w_specs = [
        pl.BlockSpec((D_MODEL, tf), lambda i, j: (0, j)),
        pl.BlockSpec((D_MODEL, tf), lambda i, j: (0, j)),
        pl.BlockSpec((tf, D_MODEL), lambda i, j: (j, 0)),
    ]
    in_specs = [
        pl.BlockSpec((tm, D_MODEL), lambda i, j: (i, 0)),
        pl.BlockSpec((tm, D_MODEL), lambda i, j: (jnp.minimum(i + 1, n_rows - 1), 0)),
        row,
    ] + w_specs
    args = [x, x, norm_g, wg, wu, wd]
    if final:
        in_specs.append(row)
        args.append(final_g)
    out_specs = [pl.BlockSpec((tm, D_MODEL), lambda i, j: (i, 0))]
    out_shape = [jax.ShapeDtypeStruct((n, D_MODEL), F32)]
    if cast:
        out_specs += w_specs
        out_shape += [jax.ShapeDtypeStruct(m.shape, BF16) for m in (wg, wu, wd)]
    outs = pl.pallas_call(
        functools.partial(_ffn_kernel, n_ff=n_ff, final=final, cast=cast),
        grid=(n_rows, n_ff),
        in_specs=in_specs,
        out_specs=out_specs,
        out_shape=out_shape,
        scratch_shapes=[pltpu.VMEM((2, tm, D_MODEL), BF16), pltpu.VMEM((tm, D_MODEL), F32)],
        compiler_params=pltpu.CompilerParams(
            dimension_semantics=("arbitrary", "arbitrary"), vmem_limit_bytes=VMEM_LIMIT_BYTES),
        name=("ffn_final" if final else "ffn") + ("_cast" if cast else ""),
    )(*args)
    return outs if cast else outs[0]


def _ffn_resident_kernel(*refs, tf, final):
    x_ref, g_ref, wg_ref, wu_ref, wd_ref = refs[:5]
    fg_ref = refs[5] if final else None
    o_ref = refs[-1]
    xn = _rmsnorm(x_ref[...], g_ref[...]).astype(BF16)
    acc = None
    for c in range(D_FF // tf):
        cols = slice(c * tf, (c + 1) * tf)
        gate = jnp.dot(xn, wg_ref[:, cols], preferred_element_type=F32)
        up = jnp.dot(xn, wu_ref[:, cols], preferred_element_type=F32)
        h = (gate * _sigmoid(gate) * up).astype(BF16)
        part = jnp.dot(h, wd_ref[cols, :], preferred_element_type=F32)
        acc = part if acc is None else acc + part
    y = x_ref[...] + 0.5 * acc
    if final:
        y = _rmsnorm(y, fg_ref[...])
    o_ref[...] = y


def _ffn_resident(x, norm_g, wg, wu, wd, final_g=None, *, tm, tf):
    n = x.shape[0]
    n_rows = n // tm
    final = final_g is not None
    row = pl.BlockSpec((1, D_MODEL), lambda i: (0, 0))
    resident = pl.Buffered(1)
    in_specs = [
        pl.BlockSpec((tm, D_MODEL), lambda i: (i, 0)),
        row,
        pl.BlockSpec((D_MODEL, D_FF), lambda i: (0, 0), pipeline_mode=resident),
        pl.BlockSpec((D_MODEL, D_FF), lambda i: (0, 0), pipeline_mode=resident),
        pl.BlockSpec((D_FF, D_MODEL), lambda i: (0, 0), pipeline_mode=resident),
    ]
    args = [x, norm_g, wg, wu, wd]
    if final:
        in_specs.append(row)
        args.append(final_g)
    return pl.pallas_call(
        functools.partial(_ffn_resident_kernel, tf=tf, final=final),
        grid=(n_rows,),
        in_specs=in_specs,
        out_specs=pl.BlockSpec((tm, D_MODEL), lambda i: (i, 0)),
        out_shape=jax.ShapeDtypeStruct((n, D_MODEL), F32),
        compiler_params=pltpu.CompilerParams(
            dimension_semantics=("parallel",), vmem_limit_bytes=VMEM_LIMIT_BYTES),
        name="ffn_final_resident" if final else "ffn_resident",
    )(*args)


def _proj_kernel(x_ref, g_ref, w_ref, u_ref, qkv_ref, k_ref, v_ref, gate_ref, w_out_ref=None):
    h = _rmsnorm(x_ref[...], g_ref[...]).astype(BF16)

    def mm(c0, c1):
        w = w_ref[:, c0:c1]
        if w_out_ref is not None:
            w = w.astype(BF16)
            w_out_ref[:, c0:c1] = w
        return jnp.dot(h, w, preferred_element_type=F32)

    c = 0
    u_ref[...] = mm(c, c + D_POOL)
    c += D_POOL
    qkv_ref[:, 0:D_SB] = (mm(c, c + D_SB) * (LOG2E * SB_HEAD_DIM ** -0.5)).astype(BF16)
    c += D_SB
    for idx, hm_ref in ((1, k_ref), (2, v_ref)):
        t = mm(c, c + D_SB)
        c += D_SB
        qkv_ref[:, idx * D_SB:(idx + 1) * D_SB] = t.astype(BF16)
        streams, _, rows, _ = hm_ref.shape
        for b in range(streams):
            for hd in range(SB_HEADS):
                hm_ref[b, hd, :, :] = t[b * rows:(b + 1) * rows,
                                        hd * SB_HEAD_DIM:(hd + 1) * SB_HEAD_DIM]
    gate_ref[:, 0:D_MODEL] = mm(c, c + D_MODEL).astype(BF16)
    c += D_MODEL
    gate_ref[:, D_MODEL:2 * D_MODEL] = mm(c, c + D_MODEL).astype(BF16)


def _proj(x, norm_g, w_in, *, batch, seq, tm):
    n = x.shape[0]
    cast = w_in.dtype == F32
    assert not cast or n == tm
    if tm <= seq:
        tiles_per_seq = seq // tm
        hm_spec = pl.BlockSpec((1, SB_HEADS, tm, SB_HEAD_DIM),
                               lambda i: (i // tiles_per_seq, 0, i % tiles_per_seq, 0))
    else:
        assert tm % seq == 0
        hm_spec = pl.BlockSpec((tm // seq, SB_HEADS, seq, SB_HEAD_DIM), lambda i: (i, 0, 0, 0))
    hm_shape = jax.ShapeDtypeStruct((batch, SB_HEADS, seq, SB_HEAD_DIM), F32)
    w_spec = pl.BlockSpec((D_MODEL, D_IN), lambda i: (0, 0), pipeline_mode=pl.Buffered(1))
    out_specs = [
        pl.BlockSpec((tm, D_POOL), lambda i: (i, 0)),
        pl.BlockSpec((tm, 3 * D_SB), lambda i: (i, 0)),
        hm_spec,
        hm_spec,
        pl.BlockSpec((tm, 2 * D_MODEL), lambda i: (i, 0)),
    ]
    out_shape = [
        jax.ShapeDtypeStruct((n, D_POOL), F32),
        jax.ShapeDtypeStruct((n, 3 * D_SB), BF16),
        hm_shape,
        hm_shape,
        jax.ShapeDtypeStruct((n, 2 * D_MODEL), BF16),
    ]
    if cast:
        out_specs.append(pl.BlockSpec((D_MODEL, D_IN), lambda i: (0, 0)))
        out_shape.append(jax.ShapeDtypeStruct((D_MODEL, D_IN), BF16))
    return pl.pallas_call(
        _proj_kernel,
        grid=(n // tm,),
        in_specs=[
            pl.BlockSpec((tm, D_MODEL), lambda i: (i, 0)),
            pl.BlockSpec((1, D_MODEL), lambda i: (0, 0)),
            w_spec,
        ],
        out_specs=out_specs,
        out_shape=out_shape,
        compiler_params=pltpu.CompilerParams(
            dimension_semantics=("parallel",), vmem_limit_bytes=VMEM_LIMIT_BYTES),
        name="proj_cast" if cast else "proj",
    )(x, norm_g, w_in)


def _attn_kernel(q_ref, k_ref, v_ref, o_ref, acc_ref, car_ref, *, tq, hq, pos0):
    win = ATTN_WINDOW
    q_start = pos0 + pl.program_id(1) * tq
    subs = range(tq // hq)
    pairs = range(SB_HEADS // 2)
    pair_cols = [slice(p * LANES, (p + 1) * LANES) for p in pairs]

    lane = lax.broadcasted_iota(jnp.int32, (1, LANES), 1)
    first_head = lane < SB_HEAD_DIM
    r = lax.broadcasted_iota(jnp.int32, (win, win), 0)
    c = lax.broadcasted_iota(jnp.int32, (win, win), 1)
    newer = jnp.where(r > c, 1.0, 0.0).astype(BF16)

    def softplus2(z):
        return jnp.maximum(z, 0.0) + jnp.log2(1.0 + jnp.exp2(-jnp.abs(z)))

    def visit(m, mode):
        k_ws, v_ws, valids = [], [], []
        for j in subs:
            row0 = q_start + j * hq
            win_end = row0 + hq - m * win
            q_pos = row0 + lax.broadcasted_iota(jnp.int32, (hq, 1), 0)
            if mode == "inner":
                start = pl.multiple_of(win_end - win, hq)
                k_pos = (win_end - LANES) + lax.broadcasted_iota(jnp.int32, (1, LANES), 1)
                valids.append(k_pos < q_pos)
            else:
                start = pl.multiple_of(jnp.maximum(win_end - win, 0), hq)
                k_pos = start + lax.broadcasted_iota(jnp.int32, (1, win), 1)
                valids.append(k_pos < (q_pos if mode == "edge" else win_end))
            k_ws.append(k_ref[pl.ds(start, win), :])
            v_ws.append(v_ref[pl.ds(start, win), :])

        def mask(x, j):
            if mode == "inner":
                return jnp.concatenate(
                    [x[:, :win - LANES],
                     jnp.where(valids[j], x[:, win - LANES:], MASKED_LOGIT)], axis=1)
            return jnp.where(valids[j], x, MASKED_LOGIT)

        z_parts = []
        for j in subs:
            rows = slice(j * hq, (j + 1) * hq)
            for p in pairs:
                q_p = q_ref[rows, pair_cols[p]]
                zero = jnp.zeros_like(q_p)
                q_2 = jnp.concatenate([jnp.where(first_head, q_p, zero),
                                       jnp.where(first_head, zero, q_p)], axis=0)
                z_2 = lax.dot_general(q_2, k_ws[j][:, pair_cols[p]], (((1,), (1,)), ((), ())),
                                      preferred_element_type=F32)
                z_parts += [mask(z_2[:hq], j), mask(z_2[hq:], j)]
        z = jnp.concatenate(z_parts, axis=0)
        s = softplus2(z)
        sums = jnp.dot(s.astype(BF16), newer, preferred_element_type=F32)
        arg = (z - s) - sums
        total = sums[:, 0:LANES] + s[:, 0:LANES]
        if mode == "older":
            car = car_ref[...]
            car_row = jnp.broadcast_to(car[:, 0:1], car.shape)
            arg = arg - jnp.concatenate([car_row] * (win // LANES), axis=1)
            total = car + total
        car_ref[...] = total
        w = jnp.exp2(arg).astype(BF16)

        i = 0
        for j in subs:
            rows = slice(j * hq, (j + 1) * hq)
            for p in pairs:
                o_2 = jnp.dot(w[i * hq:(i + 2) * hq], v_ws[j][:, pair_cols[p]],
                              preferred_element_type=F32)
                i += 2
                o_p = jnp.where(first_head, o_2[:hq], o_2[hq:])
                if mode == "older":
                    acc_ref[rows, pair_cols[p]] += o_p
                else:
                    acc_ref[rows, pair_cols[p]] = o_p
        least = jnp.min(total, axis=0, keepdims=True)
        least = jnp.min(jnp.where(lane == 0, least, SURVIVAL_UNDERFLOW_LOG2))
        return (least >= SURVIVAL_UNDERFLOW_LOG2).astype(jnp.int32)

    done = lax.cond(q_start + hq >= win, lambda: visit(0, "inner"), lambda: visit(0, "edge"))

    def cond(carry):
        m, done = carry
        return jnp.logical_and(q_start + tq - m * win > 0, done == 0)

    def body(carry):
        m, _ = carry
        return m + 1, visit(m, "older")

    lax.while_loop(cond, body, (jnp.int32(1), done))
    o_ref[...] = acc_ref[...].astype(BF16)


def _attn(q_arr, q_col, k_arr, k_col, v_arr, v_col, *, batch, q_len, k_len, pos0, tq, hq):
    assert tq % hq == 0 and q_len % tq == 0 and pos0 % tq == 0 and ATTN_WINDOW % hq == 0
    assert pos0 + q_len <= k_len and k_len >= ATTN_WINDOW and hq % BF16_SUBLANES == 0
    nq = q_len // tq
    stacked_rows = (tq // hq) * SB_HEADS * hq
    return pl.pallas_call(
        functools.partial(_attn_kernel, tq=tq, hq=hq, pos0=pos0),
        grid=(batch, nq),
        in_specs=[
            pl.BlockSpec((tq, D_SB), lambda b, i: (b * nq + i, q_col)),
            pl.BlockSpec((k_len, D_SB), lambda b, i: (b, k_col)),
            pl.BlockSpec((k_len, D_SB), lambda b, i: (b, v_col)),
        ],
        out_specs=pl.BlockSpec((tq, D_SB), lambda b, i: (b * nq + i, 0)),
        out_shape=jax.ShapeDtypeStruct((batch * q_len, D_SB), BF16),
        scratch_shapes=[pltpu.VMEM((tq, D_SB), F32), pltpu.VMEM((stacked_rows, LANES), F32)],
        compiler_params=pltpu.CompilerParams(
            dimension_semantics=("parallel", "arbitrary"), vmem_limit_bytes=VMEM_LIMIT_BYTES),
        name="attn",
    )(q_arr, k_arr, v_arr)


def _merge_kernel(x_ref, u_ref, halo_ref, o_ref, gate_ref, pw_ref, ps_ref, wbp_ref, wbs_ref,
                  wout_ref, y_ref, *copy_refs, tm, pos0, tiles_per_seq):
    def weight(ref, position, *idx):
        w = ref[idx] if idx else ref[...]
        if copy_refs:
            w = w.astype(BF16)
            if idx:
                copy_refs[position][idx] = w
            else:
                copy_refs[position][...] = w
        return w

    w_sb = weight(wbs_ref, 2)
    streams = halo_ref.shape[0]
    rows = tm // streams
    if streams == 1:
        first_pos = pos0 + (pl.program_id(0) % tiles_per_seq) * tm
        pos = first_pos + lax.broadcasted_iota(jnp.int32, (tm, 1), 0)
    else:
        pos = jnp.concatenate([pos0 + lax.broadcasted_iota(jnp.int32, (rows, 1), 0)] * streams,
                              axis=0)

    n_groups = len(POOL_WINDOWS)
    n_chunks = n_groups if tm % (BF16_SUBLANES * n_groups) == 0 else 1
    quarter = tm // n_chunks
    span = POOL_HALO + rows
    run = jnp.concatenate(
        [part for b in range(streams)
         for part in (halo_ref[b], u_ref[b * rows:(b + 1) * rows, :])], axis=0)
    width, mixed, gated_sb = 1, [], []
    for gi, win in enumerate(POOL_WINDOWS):
        while width < win:
            run = run + pltpu.roll(run, width, 0)
            width *= 2
        assert width == win, "pool windows must be ascending powers of two"
        cols = slice(gi * POOL_GROUP, (gi + 1) * POOL_GROUP)
        cnt = jnp.minimum(win, pos + 1).astype(F32)
        tot = jnp.concatenate([run[b * span + POOL_HALO:(b + 1) * span, 0:POOL_GROUP]
                               for b in range(streams)], axis=0)
        d = (tot / cnt - u_ref[:, cols]).astype(BF16)
        mixed.append(jnp.dot(d, weight(pw_ref, 0, gi), preferred_element_type=F32)
                     * ps_ref[:, cols])
        if gi + 1 < n_groups:
            run = run[:, POOL_GROUP:]
        if gi < n_chunks:
            part = slice(gi * quarter, (gi + 1) * quarter)
            br_sb = jnp.dot(o_ref[part, :], w_sb, preferred_element_type=F32)
            gated_sb.append(
                _sigmoid(gate_ref[part, D_MODEL:2 * D_MODEL].astype(F32)) * br_sb)
    a = jnp.concatenate(mixed, axis=1).astype(BF16)

    br_pool = jnp.dot(a, weight(wbp_ref, 1), preferred_element_type=F32)
    merged = (_sigmoid(gate_ref[:, 0:D_MODEL].astype(F32)) * br_pool
              + jnp.concatenate(gated_sb, axis=0))
    y_ref[...] = x_ref[...] + jnp.dot(merged.astype(BF16), weight(wout_ref, 3),
                                      preferred_element_type=F32)


def _merge(x, u, halo, o, gate, pool_w, pool_scale, w_bp, w_bs, w_out, *, tm, pos0, seq):
    n = x.shape[0]
    const2 = lambda i: (0, 0)
    resident = pl.Buffered(1)
    streams = max(tm // seq, 1)
    assert tm % seq == 0 or seq % tm == 0
    weights = (pool_w, w_bp, w_bs, w_out)
    cast = pool_w.dtype == F32
    assert not cast or n == tm
    out_specs = [pl.BlockSpec((tm, D_MODEL), lambda i: (i, 0))]
    out_shape = [jax.ShapeDtypeStruct((n, D_MODEL), F32)]
    if cast:
        out_specs += [pl.BlockSpec(m.shape, lambda i, nd=m.ndim: (0,) * nd) for m in weights]
        out_shape += [jax.ShapeDtypeStruct(m.shape, BF16) for m in weights]
    return pl.pallas_call(
        functools.partial(_merge_kernel, tm=tm, pos0=pos0, tiles_per_seq=max(seq // tm, 1)),
        grid=(n // tm,),
        in_specs=[
            pl.BlockSpec((tm, D_MODEL), lambda i: (i, 0)),
            pl.BlockSpec((tm, D_POOL), lambda i: (i, 0)),
            pl.BlockSpec((streams, POOL_HALO, D_POOL), lambda i: (i, 0, 0)),
            pl.BlockSpec((tm, D_SB), lambda i: (i, 0)),
            pl.BlockSpec((tm, 2 * D_MODEL), lambda i: (i, 0)),
            pl.BlockSpec((len(POOL_WINDOWS), POOL_GROUP, POOL_GROUP), lambda i: (0, 0, 0)),
            pl.BlockSpec((1, D_POOL), const2),
            pl.BlockSpec((D_POOL, D_MODEL), const2, pipeline_mode=resident),
            pl.BlockSpec((D_SB, D_MODEL), const2, pipeline_mode=resident),
            pl.BlockSpec((D_MODEL, D_MODEL), const2, pipeline_mode=resident),
        ],
        out_specs=out_specs,
        out_shape=out_shape,
        compiler_params=pltpu.CompilerParams(
            dimension_semantics=("parallel",), vmem_limit_bytes=VMEM_LIMIT_BYTES),
        name="merge_cast" if cast else "merge",
    )(x, u, halo, o, gate, pool_w, pool_scale, w_bp, w_bs, w_out)


def _layer(x, w, *, batch, seq, pos0, k_past, v_past, pool_prefix, tm_ffn, tm_mix, tm_merge, tq,
           hq):
    assert seq >= POOL_STATE
    w_bf16 = {}

    def ffn(x_in, prefix, final_g=None):
        names = [prefix + "_gate", prefix + "_up", prefix + "_down"]
        run = _ffn_resident if w[names[0]].dtype == BF16 else _ffn
        res = run(x_in, w[prefix + "_norm"], *[w[nm] for nm in names], final_g,
                  tm=tm_ffn, tf=FFN_CHUNK)
        if isinstance(res, (list, tuple)):
            w_bf16.update(zip(names, res[1:]))
            return res[0]
        return res

    x1 = ffn(x, "ffn1")
    u, qkv, k_hm, v_hm, gate, *w_in_copy = _proj(x1, w["mix_norm"], w["w_in"], batch=batch,
                                                 seq=seq, tm=tm_mix)
    w_bf16.update(zip(["w_in"], w_in_copy))

    if k_past is None:
        o = _attn(qkv, 0, qkv, 1, qkv, 2, batch=batch, q_len=seq, k_len=seq, pos0=0, tq=tq, hq=hq)
    else:
        def with_past(past, col):
            new = qkv[:, col * D_SB:(col + 1) * D_SB].reshape(batch, seq, D_SB)
            return jnp.concatenate([past, new], axis=1).reshape(batch * (pos0 + seq), D_SB)
        o = _attn(qkv, 0, with_past(k_past, 1), 0, with_past(v_past, 2), 0,
                  batch=batch, q_len=seq, k_len=pos0 + seq, pos0=pos0, tq=tq, hq=hq)

    piece = min(tm_merge, seq)
    pieces = seq // piece
    u4 = u.reshape(batch, pieces, piece, D_POOL)
    first = jnp.pad(pool_prefix, ((0, 0), (1, 0), (0, 0)))[:, None]
    halo = jnp.concatenate([first, u4[:, :-1, piece - POOL_HALO:, :]], axis=1)
    halo = halo.reshape(batch * pieces, POOL_HALO, D_POOL)

    merge_names = ["pool_w", "w_branch_pool", "w_branch_sb", "w_out"]
    x2, *copies = _merge(x1, u, halo, o, gate, w["pool_w"], w["pool_scale"], w["w_branch_pool"],
                         w["w_branch_sb"], w["w_out"], tm=tm_merge, pos0=pos0, seq=seq)
    w_bf16.update(zip(merge_names, copies))
    y = ffn(x2, "ffn2", w["final_norm"])
    new_pool = u.reshape(batch, seq, D_POOL)[:, seq - POOL_STATE:, :]
    return (y.reshape(batch, seq, D_MODEL), k_hm, v_hm, new_pool), w_bf16


def kernel(x_prompt, x_sample, cache_k, cache_v, state_pool, ffn1_norm, ffn1_gate, ffn1_up, ffn1_down, mix_norm, w_in, pool_w, pool_scale, w_branch_pool, w_branch_sb, w_out, ffn2_norm, ffn2_gate, ffn2_up, ffn2_down, final_norm):
    assert ffn1_norm.shape[0] == 1, "single-layer kernel"
    w = dict(w_in=w_in[0], pool_w=pool_w[0], w_branch_pool=w_branch_pool[0],
             w_branch_sb=w_branch_sb[0], w_out=w_out[0],
             ffn1_gate=ffn1_gate[0], ffn1_up=ffn1_up[0], ffn1_down=ffn1_down[0],
             ffn2_gate=ffn2_gate[0], ffn2_up=ffn2_up[0], ffn2_down=ffn2_down[0])
    w.update(ffn1_norm=ffn1_norm, mix_norm=mix_norm, pool_scale=pool_scale, ffn2_norm=ffn2_norm,
             final_norm=final_norm.reshape(1, D_MODEL))

    b_p, s_p, _ = x_prompt.shape
    b_d, s_d, _ = x_sample.shape
    past = cache_k.shape[3]

    def token_major(cache):
        return cache[0].transpose(0, 2, 1, 3).reshape(b_d, past, D_SB).astype(BF16)

    (y_d, k_d, v_d, pool_d), w_bf16 = _layer(
        x_sample.reshape(b_d * s_d, D_MODEL), w, batch=b_d, seq=s_d, pos0=past,
        k_past=token_major(cache_k), v_past=token_major(cache_v), pool_prefix=state_pool[0],
        tm_ffn=b_d * s_d, tm_mix=b_d * s_d, tm_merge=b_d * s_d, tq=s_d, hq=s_d)

    (y_p, k_p, v_p, pool_p), _ = _layer(
        x_prompt.reshape(b_p * s_p, D_MODEL), {**w, **w_bf16}, batch=b_p, seq=s_p, pos0=0,
        k_past=None, v_past=None, pool_prefix=jnp.zeros((b_p, POOL_STATE, D_POOL), F32),
        tm_ffn=1024, tm_mix=1024, tm_merge=1024, tq=512, hq=64)

    return (y_p, y_d, k_p[None], v_p[None], pool_p[None], k_d[None], v_d[None], pool_d[None])
```
